```python
import jax, jax.numpy as jnp
from jax import lax
import numpy as np

D_MODEL = 2048
BATCH = 4
SEQ = 2048
DEPTH = 2
DEC_BATCH = 32
DEC_SEQ = 8
PAST_LEN = 8192
PAGE_SIZE = 128

HEAD_DIM = 128
MIX_WIDTH = D_MODEL
N_ATT_HEADS = (MIX_WIDTH // 2) // HEAD_DIM
N_MLP_HEADS = (MIX_WIDTH // 2) // HEAD_DIM
ATT_W = N_ATT_HEADS * HEAD_DIM
MLP_W = N_MLP_HEADS * HEAD_DIM
CHUNK = 128
IN_COLS = 3 * ATT_W + N_ATT_HEADS + 2 * MLP_W
D_FF = 5504
N_EXPERTS = 8
TOP_K = 2
D_FF_EXPERT = 7168
N_DENSE = (DEPTH + 1) // 2
N_MOE = DEPTH // 2
BLOCK_Q = 128
EPS = 1e-6

kernel_name = 'hymba_chunkmlp_fox_decoder_step'


def rms_norm(x, g):
    xf = x.astype(jnp.float32)
    y = xf * lax.rsqrt(jnp.mean(xf * xf, axis=-1, keepdims=True) + EPS)
    return (y * g.astype(jnp.float32)).astype(x.dtype)


def split_projection(z, b_f, q_g, k_g, sgu_g):
    lead = z.shape[:-1]
    cuts = [ATT_W, 2 * ATT_W, 3 * ATT_W, 3 * ATT_W + N_ATT_HEADS, 3 * ATT_W + N_ATT_HEADS + MLP_W]
    q, k, v, f, u, vm = jnp.split(z, cuts, axis=-1)
    q = rms_norm(q.reshape(*lead, N_ATT_HEADS, HEAD_DIM), q_g)
    k = rms_norm(k.reshape(*lead, N_ATT_HEADS, HEAD_DIM), k_g)
    v = v.reshape(*lead, N_ATT_HEADS, HEAD_DIM)
    logf = jax.nn.log_sigmoid((f + b_f).astype(jnp.float32))
    u = u.reshape(*lead, N_MLP_HEADS, HEAD_DIM)
    vm = rms_norm(vm.reshape(*lead, N_MLP_HEADS, HEAD_DIM), sgu_g)
    return q, k, v, logf, u, vm


def fox_prompt(q, k, v, logf):
    B, S, H, Dh = q.shape
    scale = Dh ** -0.5
    c = jnp.cumsum(logf, axis=1)
    ck = jnp.transpose(c, (0, 2, 1))
    n_blk = S // BLOCK_Q
    qb = jnp.moveaxis(q.reshape(B, n_blk, BLOCK_Q, H, Dh), 1, 0)
    cb = jnp.moveaxis(c.reshape(B, n_blk, BLOCK_Q, H), 1, 0)
    k_pos = jnp.arange(S)

    def one_block(args):
        qi, ci, bi = args
        s = jnp.einsum('bqhd,bkhd->bhqk', qi, k).astype(jnp.float32) * scale
        s = s + jnp.transpose(ci, (0, 2, 1))[..., None] - ck[:, :, None, :]
        q_pos = bi * BLOCK_Q + jnp.arange(BLOCK_Q)
        s = jnp.where(k_pos[None, :] <= q_pos[:, None], s, -jnp.inf)
        p = jax.nn.softmax(s, axis=-1).astype(v.dtype)
        return jnp.einsum('bhqk,bkhd->bqhd', p, v)

    o = lax.map(one_block, (qb, cb, jnp.arange(n_blk)))
    return jnp.moveaxis(o, 0, 1).reshape(B, S, H, Dh)


def fox_sample(q, k_new, v_new, logf_new, k_past, v_past, logf_past):
    Dh = q.shape[-1]
    DS = q.shape[1]
    P = k_past.shape[1]
    scale = Dh ** -0.5
    lp = logf_past.astype(jnp.float32)
    r = lax.cumsum(lp, axis=1, reverse=True) - lp
    cn = jnp.cumsum(logf_new, axis=1)
    cq = jnp.transpose(cn, (0, 2, 1))[..., None]
    s_past = jnp.einsum('bqhd,bkhd->bhqk', q, k_past).astype(jnp.float32) * scale
    s_past = s_past + cq + jnp.transpose(r, (0, 2, 1))[:, :, None, :]
    s_new = jnp.einsum('bqhd,bkhd->bhqk', q, k_new).astype(jnp.float32) * scale
    s_new = s_new + cq - jnp.transpose(cn, (0, 2, 1))[:, :, None, :]
    causal = jnp.tril(jnp.ones((DS, DS), dtype=bool))
    s_new = jnp.where(causal, s_new, -jnp.inf)
    p = jax.nn.softmax(jnp.concatenate([s_past, s_new], axis=-1), axis=-1).astype(v_new.dtype)
    return (jnp.einsum('bhqk,bkhd->bqhd', p[..., :P], v_past)
            + jnp.einsum('bhqk,bkhd->bqhd', p[..., P:], v_new))


def chunk_mlp_prompt(u, vm, w_s, b_s):
    B, S, H, C = u.shape
    n = S // CHUNK
    wm = jnp.tril(w_s)
    vc = vm.reshape(B, n, CHUNK, H, C)
    s = jnp.einsum('hts,bnshc->bnthc', wm, vc) + jnp.transpose(b_s)[:, :, None]
    return u * s.reshape(B, S, H, C)


def chunk_mlp_sample(u, vm, w_s, b_s):
    DS = u.shape[1]
    wm = jnp.tril(w_s[:, :DS, :DS])
    s = jnp.einsum('hts,bshc->bthc', wm, vm) + jnp.transpose(b_s[:, :DS])[:, :, None]
    return u * s


def merge_heads(att_o, mlp_o, head_g, w_out):
    lead = att_o.shape[:-2]
    o = rms_norm(jnp.concatenate([att_o, mlp_o], axis=-2), head_g)
    return o.reshape(*lead, MIX_WIDTH) @ w_out


def swiglu(h, wg, wu, wd):
    return (jax.nn.silu(h @ wg) * (h @ wu)) @ wd


def moe_swiglu(h, router_w, wg, wu, wd):
    lead = h.shape[:-1]
    t = h.reshape(-1, D_MODEL)
    logits = (t @ router_w).astype(jnp.float32)
    top_l, top_i = lax.top_k(logits, TOP_K)
    top_p = jax.nn.softmax(top_l, axis=-1)
    gates = jnp.sum(jax.nn.one_hot(top_i, N_EXPERTS, dtype=jnp.float32) * top_p[..., None], axis=-2)
    out = jnp.zeros_like(t)
    for e in range(N_EXPERTS):
        out = out + gates[:, e:e + 1].astype(t.dtype) * swiglu(t, wg[e], wu[e], wd[e])
    return out.reshape(*lead, D_MODEL)


def setup_inputs(seed: int = 0) -> dict:
    key = jax.random.key(seed)
    ks = jax.random.split(key, 24)
    n_pages = PAST_LEN // PAGE_SIZE
    n_phys = (DEC_BATCH * n_pages * 5) // 4
    nrm = jax.random.normal
    f32 = jnp.float32
    x_prompt = nrm(ks[0], (BATCH, SEQ, D_MODEL), f32)
    x_sample = nrm(ks[1], (DEC_BATCH, DEC_SEQ, D_MODEL), f32)
    cache_k = nrm(ks[2], (DEPTH, n_phys, PAGE_SIZE, N_ATT_HEADS, HEAD_DIM), f32)
    cache_v = nrm(ks[3], (DEPTH, n_phys, PAGE_SIZE, N_ATT_HEADS, HEAD_DIM), f32)
    cache_logf = jax.nn.log_sigmoid(5.0 + 0.5 * nrm(ks[4], (DEPTH, n_phys, PAGE_SIZE, N_ATT_HEADS), f32))
    page_table = jax.random.permutation(ks[5], n_phys)[:DEC_BATCH * n_pages].reshape(DEC_BATCH, n_pages).astype(jnp.int32)
    norm_mix_g = 1.0 + 0.05 * nrm(ks[6], (DEPTH, D_MODEL), f32)
    col_scale = jnp.ones((IN_COLS,), f32).at[3 * ATT_W:3 * ATT_W + N_ATT_HEADS].set(0.1)
    w_in = nrm(ks[7], (DEPTH, D_MODEL, IN_COLS), f32) * (D_MODEL ** -0.5) * col_scale
    b_f = jnp.linspace(3.0, 7.0, N_ATT_HEADS, dtype=f32)[None, :] + 0.1 * nrm(ks[8], (DEPTH, N_ATT_HEADS), f32)
    q_norm_g = 1.0 + 0.05 * nrm(ks[9], (DEPTH, N_ATT_HEADS, HEAD_DIM), f32)
    k_norm_g = 1.0 + 0.05 * nrm(ks[10], (DEPTH, N_ATT_HEADS, HEAD_DIM), f32)
    sgu_norm_g = 1.0 + 0.05 * nrm(ks[11], (DEPTH, N_MLP_HEADS, HEAD_DIM), f32)
    w_spatial = nrm(ks[12], (DEPTH, N_MLP_HEADS, CHUNK, CHUNK), f32) * 0.5 * CHUNK ** -0.5
    b_spatial = 1.0 + 0.1 * nrm(ks[13], (DEPTH, N_MLP_HEADS, CHUNK), f32)
    head_norm_g = 1.0 + 0.05 * nrm(ks[14], (DEPTH, N_ATT_HEADS + N_MLP_HEADS, HEAD_DIM), f32)
    w_out = nrm(ks[15], (DEPTH, MIX_WIDTH, D_MODEL), f32) * MIX_WIDTH ** -0.5
    norm_ffn_g = 1.0 + 0.05 * nrm(ks[16], (DEPTH, D_MODEL), f32)
    dense_w_gate = nrm(ks[17], (N_DENSE, D_MODEL, D_FF), f32) * D_MODEL ** -0.5
    dense_w_up = nrm(ks[18], (N_DENSE, D_MODEL, D_FF), f32) * D_MODEL ** -0.5
    dense_w_down = nrm(ks[19], (N_DENSE, D_FF, D_MODEL), f32) * D_FF ** -0.5
    router_w = nrm(ks[20], (N_MOE, D_MODEL, N_EXPERTS), f32) * D_MODEL ** -0.5
    moe_w_gate = nrm(ks[21], (N_MOE, N_EXPERTS, D_MODEL, D_FF_EXPERT), f32) * D_MODEL ** -0.5
    moe_w_up = nrm(ks[22], (N_MOE, N_EXPERTS, D_MODEL, D_FF_EXPERT), f32) * D_MODEL ** -0.5
    moe_w_down = nrm(ks[23], (N_MOE, N_EXPERTS, D_FF_EXPERT, D_MODEL), f32) * D_FF_EXPERT ** -0.5
    return {'x_prompt': x_prompt, 'x_sample': x_sample, 'cache_k': cache_k, 'cache_v': cache_v,
            'cache_logf': cache_logf, 'page_table': page_table, 'norm_mix_g': norm_mix_g, 'w_in': w_in,
            'b_f': b_f, 'q_norm_g': q_norm_g, 'k_norm_g': k_norm_g, 'sgu_norm_g': sgu_norm_g,
            'w_spatial': w_spatial, 'b_spatial': b_spatial, 'head_norm_g': head_norm_g, 'w_out': w_out,
            'norm_ffn_g': norm_ffn_g, 'dense_w_gate': dense_w_gate, 'dense_w_up': dense_w_up,
            'dense_w_down': dense_w_down, 'router_w': router_w, 'moe_w_gate': moe_w_gate,
            'moe_w_up': moe_w_up, 'moe_w_down': moe_w_down}


def reference(x_prompt, x_sample, cache_k, cache_v, cache_logf, page_table, norm_mix_g, w_in, b_f,
              q_norm_g, k_norm_g, sgu_norm_g, w_spatial, b_spatial, head_norm_g, w_out, norm_ffn_g,
              dense_w_gate, dense_w_up, dense_w_down, router_w, moe_w_gate, moe_w_up, moe_w_down):
    DB = x_sample.shape[0]
    past = page_table.shape[1] * cache_k.shape[2]
    xp, xs = x_prompt, x_sample
    kp_l, vp_l, fp_l, ks_l, vs_l, fs_l, ms_l = [], [], [], [], [], [], []
    for l in range(DEPTH):
        hp = rms_norm(xp, norm_mix_g[l])
        hs = rms_norm(xs, norm_mix_g[l])
        qp, kp, vp, lfp, up, vmp = split_projection(hp @ w_in[l], b_f[l], q_norm_g[l], k_norm_g[l], sgu_norm_g[l])
        qs, kss, vss, lfs, us, vms = split_projection(hs @ w_in[l], b_f[l], q_norm_g[l], k_norm_g[l], sgu_norm_g[l])
        k_past = cache_k[l][page_table].reshape(DB, past, N_ATT_HEADS, HEAD_DIM)
        v_past = cache_v[l][page_table].reshape(DB, past, N_ATT_HEADS, HEAD_DIM)
        f_past = cache_logf[l][page_table].reshape(DB, past, N_ATT_HEADS)
        att_p = fox_prompt(qp, kp, vp, lfp)
        att_s = fox_sample(qs, kss, vss, lfs, k_past, v_past, f_past)
        mlp_p = chunk_mlp_prompt(up, vmp, w_spatial[l], b_spatial[l])
        mlp_s = chunk_mlp_sample(us, vms, w_spatial[l], b_spatial[l])
        xp = xp + merge_heads(att_p, mlp_p, head_norm_g[l], w_out[l])
        xs = xs + merge_heads(att_s, mlp_s, head_norm_g[l], w_out[l])
        hp = rms_norm(xp, norm_ffn_g[l])
        hs = rms_norm(xs, norm_ffn_g[l])
        i = l // 2
        if l % 2 == 0:
            xp = xp + swiglu(hp, dense_w_gate[i], dense_w_up[i], dense_w_down[i])
            xs = xs + swiglu(hs, dense_w_gate[i], dense_w_up[i], dense_w_down[i])
        else:
            xp = xp + moe_swiglu(hp, router_w[i], moe_w_gate[i], moe_w_up[i], moe_w_down[i])
            xs = xs + moe_swiglu(hs, router_w[i], moe_w_gate[i], moe_w_up[i], moe_w_down[i])
        kp_l.append(kp)
        vp_l.append(vp)
        fp_l.append(lfp)
        ks_l.append(kss)
        vs_l.append(vss)
        fs_l.append(lfs)
        ms_l.append(vms)
    return (xp, xs, jnp.stack(kp_l), jnp.stack(vp_l), jnp.stack(fp_l),
            jnp.stack(ks_l), jnp.stack(vs_l), jnp.stack(fs_l), jnp.stack(ms_l))
```

```python
import functools

import jax
import jax.numpy as jnp
from jax import lax
from jax.experimental import pallas as pl
from jax.experimental.pallas import tpu as pltpu

F32 = jnp.float32
BF16 = jnp.bfloat16
I32 = jnp.int32

HEAD_DIM = 128
N_HEADS = 8
HEAD_W = N_HEADS * HEAD_DIM
EPS = 1e-6
NEG_INF = float("-inf")
TOP_K = 2
VMEM_LIMIT = 56 * 1024 * 1024


def _params(*sem):
    return pltpu.CompilerParams(dimension_semantics=sem, vmem_limit_bytes=VMEM_LIMIT)


def _nt_dot(a, b):
    return lax.dot_general(a, b, (((1,), (1,)), ((), ())), preferred_element_type=F32)


def _rms(x):
    return x * lax.rsqrt(jnp.mean(x * x, axis=-1, keepdims=True) + EPS)


def _proj_body(x_ref, g_ref, w_ref, wf_ref, bf_ref, gq_ref, gk_ref, gm_ref,
               q_ref, k_ref, v_ref, u_ref, vm_ref, lf_ref, lft_ref, h_scr):
    j = pl.program_id(1)

    @pl.when(j == 0)
    def _():
        hb = (_rms(x_ref[...]) * g_ref[...]).astype(BF16)
        h_scr[...] = hb
        f = jnp.dot(hb, wf_ref[...], preferred_element_type=F32) + bf_ref[...]
        lf = jnp.minimum(f, 0.0) - jnp.log1p(jnp.exp(-jnp.abs(f)))
        lf_ref[...] = lf
        lft_ref[...] = lf.T[:N_HEADS, :]

    z = jnp.dot(h_scr[...], w_ref[...], preferred_element_type=F32)

    def head_norm(gain_ref, out_ref):
        for h in range(N_HEADS):
            sl = slice(h * HEAD_DIM, (h + 1) * HEAD_DIM)
            out_ref[:, sl] = (_rms(z[:, sl]) * gain_ref[:, sl]).astype(out_ref.dtype)

    @pl.when(j == 0)
    def _():
        head_norm(gq_ref, q_ref)

    @pl.when(j == 1)
    def _():
        head_norm(gk_ref, k_ref)

    @pl.when(j == 2)
    def _():
        v_ref[...] = z

    @pl.when(j == 3)
    def _():
        u_ref[...] = z

    @pl.when(j == 4)
    def _():
        head_norm(gm_ref, vm_ref)


def _project(x, g, w_main, wf, bf, gq, gk, gm, *, tm):
    T, D = x.shape
    n_sec = w_main.shape[1] // HEAD_W
    row = lambda i, j: (i, 0)
    const = lambda i, j: (0, 0)
    big = pl.BlockSpec((tm, HEAD_W), row)
    return pl.pallas_call(
        _proj_body,
        grid=(T // tm, n_sec),
        in_specs=[
            pl.BlockSpec((tm, D), row),
            pl.BlockSpec((1, D), const),
            pl.BlockSpec((D, HEAD_W), lambda i, j: (0, j)),
            pl.BlockSpec((D, HEAD_DIM), const),
            pl.BlockSpec((1, HEAD_DIM), const),
            pl.BlockSpec((1, HEAD_W), const),
            pl.BlockSpec((1, HEAD_W), const),
            pl.BlockSpec((1, HEAD_W), const),
        ],
        out_specs=[big, big, big, big, big,
                   pl.BlockSpec((tm, HEAD_DIM), row),
                   pl.BlockSpec((N_HEADS, tm), lambda i, j: (0, i))],
        out_shape=[
            jax.ShapeDtypeStruct((T, HEAD_W), BF16),
            jax.ShapeDtypeStruct((T, HEAD_W), F32),
            jax.ShapeDtypeStruct((T, HEAD_W), F32),
            jax.ShapeDtypeStruct((T, HEAD_W), F32),
            jax.ShapeDtypeStruct((T, HEAD_W), F32),
            jax.ShapeDtypeStruct((T, HEAD_DIM), F32),
            jax.ShapeDtypeStruct((N_HEADS, T), F32),
        ],
        scratch_shapes=[pltpu.VMEM((tm, D), BF16)],
        compiler_params=_params("parallel", "arbitrary"),
        name="in_proj",
    )(x, g, w_main, wf, bf, gq, gk, gm)


def _split3(x):
    x1 = x.astype(BF16)
    r1 = x - x1.astype(F32)
    x2 = r1.astype(BF16)
    x3 = (r1 - x2.astype(F32)).astype(BF16)
    return x1, x2, x3


def _cumsum_body(lf_ref, lft_ref, c_ref, ct_ref, *, chunk):
    S = lf_ref.shape[0]
    r = lax.broadcasted_iota(I32, (chunk, chunk), 0)
    c = lax.broadcasted_iota(I32, (chunk, chunk), 1)
    lower = (c <= r).astype(BF16)
    upper = (r <= c).astype(BF16)
    carry_c = jnp.zeros((1, lf_ref.shape[1]), F32)
    carry_r = jnp.zeros((lft_ref.shape[0], 1), F32)
    for n in range(S // chunk):
        sl = slice(n * chunk, (n + 1) * chunk)
        cs = carry_c
        for part in _split3(lf_ref[sl, :]):
            cs = cs + jnp.dot(lower, part, preferred_element_type=F32)
        c_ref[sl, :] = cs
        carry_c = cs[chunk - 1:chunk, :]
        rs = carry_r
        for part in _split3(lft_ref[:, sl]):
            rs = rs + jnp.dot(part, upper, preferred_element_type=F32)
        ct_ref[:, sl] = rs
        carry_r = rs[:, chunk - 1:chunk]


def _cumsum(lf, lft, *, seq):
    T = lf.shape[0]
    return pl.pallas_call(
        functools.partial(_cumsum_body, chunk=256),
        grid=(T // seq,),
        in_specs=[pl.BlockSpec((seq, HEAD_DIM), lambda b: (b, 0)),
                  pl.BlockSpec((N_HEADS, seq), lambda b: (0, b))],
        out_specs=[pl.BlockSpec((seq, HEAD_DIM), lambda b: (b, 0)),
                   pl.BlockSpec((N_HEADS, seq), lambda b: (0, b))],
        out_shape=[jax.ShapeDtypeStruct((T, HEAD_DIM), F32),
                   jax.ShapeDtypeStruct((N_HEADS, T), F32)],
        compiler_params=_params("parallel"),
        name="logf_cumsum",
    )(lf, lft)


def _fox_prompt_body(q_ref, k_ref, v_ref, c_ref, ct_ref, hg_ref, o_ref, kb, vb, *, tq):
    h = pl.program_id(1)
    qi = pl.program_id(2)

    @pl.when(qi == 0)
    def _():
        kb[...] = k_ref[...].astype(BF16)
        vb[...] = v_ref[...].astype(BF16)

    q = q_ref[...]
    lane = lax.broadcasted_iota(I32, c_ref.shape, 1)
    c_col = jnp.sum(jnp.where(lane == h, c_ref[...], 0.0), axis=1, keepdims=True)

    def block(kj, carry, diagonal):
        m, l, acc = carry
        start = pl.multiple_of(kj * tq, tq)
        s = _nt_dot(q, kb[pl.ds(start, tq), :])
        s = s + c_col - ct_ref[kj]
        if diagonal:
            r = lax.broadcasted_iota(I32, (tq, tq), 0)
            c = lax.broadcasted_iota(I32, (tq, tq), 1)
            s = jnp.where(c <= r, s, NEG_INF)
        m_new = jnp.maximum(m, jnp.max(s, axis=1, keepdims=True))
        p = jnp.exp(s - m_new)
        alpha = jnp.exp(m - m_new)
        l = alpha * l + jnp.sum(p, axis=1, keepdims=True)
        acc = alpha * acc + jnp.dot(p.astype(BF16), vb[pl.ds(start, tq), :],
                                    preferred_element_type=F32)
        return m_new, l, acc

    init = (jnp.full((tq, 1), NEG_INF, F32), jnp.zeros((tq, 1), F32), jnp.zeros((tq, HEAD_DIM), F32))
    carry = lax.fori_loop(0, qi, lambda kj, cr: block(kj, cr, False), init)
    _, l, acc = block(qi, carry, True)
    o_ref[...] = (_rms(acc / l) * hg_ref[...]).astype(o_ref.dtype)


def _fox_prompt(q, k, v, c, ct, hg, *, batch, seq, tq):
    T = q.shape[0]
    nq = seq // tq
    ct4 = ct.reshape(N_HEADS, batch, nq, 1, tq).transpose(1, 0, 2, 3, 4).reshape(batch * N_HEADS, nq, 1, tq)
    return pl.pallas_call(
        functools.partial(_fox_prompt_body, tq=tq),
        grid=(batch, N_HEADS, nq),
        in_specs=[
            pl.BlockSpec((tq, HEAD_DIM), lambda b, h, i: (b * nq + i, h)),
            pl.BlockSpec((seq, HEAD_DIM), lambda b, h, i: (b, h)),
            pl.BlockSpec((seq, HEAD_DIM), lambda b, h, i: (b, h)),
            pl.BlockSpec((tq, HEAD_DIM), lambda b, h, i: (b * nq + i, 0)),
            pl.BlockSpec((None, nq, 1, tq), lambda b, h, i: (b * N_HEADS + h, 0, 0, 0)),
            pl.BlockSpec((None, 1, HEAD_DIM), lambda b, h, i: (h, 0, 0)),
        ],
        out_specs=pl.BlockSpec((tq, HEAD_DIM), lambda b, h, i: (b * nq + i, h)),
        out_shape=jax.ShapeDtypeStruct((T, HEAD_W), BF16),
        scratch_shapes=[pltpu.VMEM((seq, HEAD_DIM), BF16), pltpu.VMEM((seq, HEAD_DIM), BF16)],
        compiler_params=_params("parallel", "parallel", "arbitrary"),
        name="fox_prompt",
    )(q, k, v, c, ct4, hg)


def _fox_sample_body(pt_ref, q_ref, kn_ref, vn_ref, lfr_ref, lfc_ref, hg_ref, *rest, pps, page):
    del pt_ref
    kp, vp, lp = rest[:pps], rest[pps:2 * pps], rest[2 * pps:3 * pps]
    o_ref, m_scr, l_scr, acc_scr, carry_scr = rest[3 * pps:]
    j = pl.program_id(1)
    rows = q_ref.shape[0]
    dsq = rows // N_HEADS
    cols = page * N_HEADS

    @pl.when(j == 0)
    def _():
        m_scr[...] = jnp.full(m_scr.shape, NEG_INF, F32)
        l_scr[...] = jnp.zeros(l_scr.shape, F32)
        acc_scr[...] = jnp.zeros(acc_scr.shape, F32)
        carry_scr[...] = jnp.zeros(carry_scr.shape, F32)

    q = q_ref[...]

    rn = lax.broadcasted_iota(I32, (rows, rows), 0)
    cn = lax.broadcasted_iota(I32, (rows, rows), 1)
    same_head_new = (rn // dsq) == (cn % N_HEADS)
    cq_col = jnp.sum(jnp.where(same_head_new & ((cn // N_HEADS) <= (rn % dsq)), lfr_ref[...], 0.0),
                     axis=1, keepdims=True)

    def update(s, v2):
        m = m_scr[...]
        m_new = jnp.maximum(m, jnp.max(s, axis=1, keepdims=True))
        p = jnp.exp(s - m_new)
        alpha = jnp.exp(m - m_new)
        l_scr[...] = alpha * l_scr[...] + jnp.sum(p, axis=1, keepdims=True)
        acc_scr[...] = alpha * acc_scr[...] + jnp.dot(p.astype(BF16), v2, preferred_element_type=F32)
        m_scr[...] = m_new

    lpp = jnp.concatenate([lp[p][...] for p in range(pps)], axis=0)
    lane = lax.broadcasted_iota(I32, (pps, cols), 1)
    sub = lax.broadcasted_iota(I32, (pps, cols), 0)
    n_steps = (page - 1).bit_length()
    tot = lpp
    for t in range(n_steps):
        tot = tot + pltpu.roll(tot, N_HEADS << t, axis=1)
    suf = jnp.where(lane < cols - N_HEADS, pltpu.roll(lpp, cols - N_HEADS, axis=1), 0.0)
    for t in range(n_steps):
        sh = N_HEADS << t
        suf = suf + jnp.where(lane < cols - sh, pltpu.roll(suf, cols - sh, axis=1), 0.0)
    pre = jnp.where(sub >= 1, pltpu.roll(tot, 1, axis=0), 0.0)
    for t in range((pps - 1).bit_length()):
        sh = 1 << t
        pre = pre + jnp.where(sub >= sh, pltpu.roll(pre, sh, axis=0), 0.0)
    carry = carry_scr[...]
    bias = suf + pre + carry
    carry_scr[...] = carry + jnp.sum(tot, axis=0, keepdims=True)

    rp = lax.broadcasted_iota(I32, (rows, cols), 0)
    cp = lax.broadcasted_iota(I32, (rows, cols), 1)
    same_head = (rp // dsq) == (cp % N_HEADS)
    for p in range(pps):
        k2 = kp[p][...].reshape(cols, HEAD_DIM).astype(BF16)
        v2 = vp[p][...].reshape(cols, HEAD_DIM).astype(BF16)
        s = _nt_dot(q, k2) + cq_col + bias[p:p + 1, :]
        update(jnp.where(same_head, s, NEG_INF), v2)

    @pl.when(j == pl.num_programs(1) - 1)
    def _():
        cn_row = jnp.sum(jnp.where(((rn % N_HEADS) == (cn % N_HEADS)) & ((rn // N_HEADS) <= (cn // N_HEADS)),
                                   lfc_ref[...], 0.0), axis=0, keepdims=True)
        s = _nt_dot(q, kn_ref[...].astype(BF16)) + cq_col - cn_row
        keep = same_head_new & ((cn // N_HEADS) <= (rn % dsq))
        update(jnp.where(keep, s, NEG_INF), vn_ref[...].astype(BF16))
        o = acc_scr[...] / l_scr[...]
        o_ref[...] = (_rms(o) * hg_ref[...]).astype(o_ref.dtype)


def _fox_sample(layer, page_table, q, kn, vn, lfr, lfc, hg_rows, cache_k, cache_v, cache_lf2, *, pps):
    DB, n_pages = page_table.shape
    rows = q.shape[1]
    page = cache_k.shape[2]
    cols = page * N_HEADS
    n_steps = n_pages // pps

    def seq_map(b, j, pt):
        return (b, 0, 0)

    def page_map(p):
        def f(b, j, pt):
            return (layer, pt[b, n_pages - 1 - (j * pps + p)], 0, 0, 0)
        return f

    def lf_map(p):
        def f(b, j, pt):
            return (layer, pt[b, n_pages - 1 - (j * pps + p)], 0, 0)
        return f

    kv_spec = lambda p: pl.BlockSpec((None, None, page, N_HEADS, HEAD_DIM), page_map(p))
    grid_spec = pltpu.PrefetchScalarGridSpec(
        num_scalar_prefetch=1,
        grid=(DB, n_steps),
        in_specs=[
            pl.BlockSpec((None, rows, HEAD_DIM), seq_map),
            pl.BlockSpec((None, rows, HEAD_DIM), seq_map),
            pl.BlockSpec((None, rows, HEAD_DIM), seq_map),
            pl.BlockSpec((None, 1, rows), seq_map),
            pl.BlockSpec((None, rows, 1), seq_map),
            pl.BlockSpec((rows, HEAD_DIM), lambda b, j, pt: (0, 0)),
        ] + [kv_spec(p) for p in range(pps)] + [kv_spec(p) for p in range(pps)]
          + [pl.BlockSpec((None, None, 1, cols), lf_map(p)) for p in range(pps)],
        out_specs=pl.BlockSpec((None, rows, HEAD_DIM), seq_map),
        scratch_shapes=[pltpu.VMEM((rows, 1), F32), pltpu.VMEM((rows, 1), F32),
                        pltpu.VMEM((rows, HEAD_DIM), F32), pltpu.VMEM((1, cols), F32)],
    )
    return pl.pallas_call(
        functools.partial(_fox_sample_body, pps=pps, page=page),
        grid_spec=grid_spec,
        out_shape=jax.ShapeDtypeStruct((DB, rows, HEAD_DIM), BF16),
        compiler_params=_params("parallel", "arbitrary"),
        name="fox_sample",
    )(page_table, q, kn, vn, lfr, lfc, hg_rows,
      *([cache_k] * pps), *([cache_v] * pps), *([cache_lf2] * pps))


def _merge_body(att_ref, u_ref, vm_ref, wc_ref, bc_ref, hg_ref, wo_ref, x_ref, g_ref,
                xo_ref, ho_ref, mlp_scr, *, chunk):
    tm = x_ref.shape[0]
    for h in range(N_HEADS):
        sl = slice(h * HEAD_DIM, (h + 1) * HEAD_DIM)
        for c in range(tm // chunk):
            rows = slice(c * chunk, (c + 1) * chunk)
            s = jnp.dot(wc_ref[h], vm_ref[rows, sl].astype(BF16), preferred_element_type=F32) + bc_ref[h]
            o = u_ref[rows, sl] * s
            mlp_scr[rows, sl] = (_rms(o) * hg_ref[:, sl]).astype(BF16)
    y = jnp.dot(att_ref[...], wo_ref[:HEAD_W, :], preferred_element_type=F32)
    y = y + jnp.dot(mlp_scr[...], wo_ref[HEAD_W:, :], preferred_element_type=F32)
    xn = x_ref[...] + y
    xo_ref[...] = xn
    ho_ref[...] = (_rms(xn) * g_ref[...]).astype(ho_ref.dtype)


def _merge(att, u, vm, wc, bc, hg_mlp, wo, x, g, *, tm, chunk, h_dtype):
    T, D = x.shape
    row = lambda i: (i, 0)
    c2 = lambda i: (0, 0)
    c3 = lambda i: (0, 0, 0)
    return pl.pallas_call(
        functools.partial(_merge_body, chunk=chunk),
        grid=(T // tm,),
        in_specs=[
            pl.BlockSpec((tm, HEAD_W), row),
            pl.BlockSpec((tm, HEAD_W), row),
            pl.BlockSpec((tm, HEAD_W), row),
            pl.BlockSpec((N_HEADS, chunk, chunk), c3),
            pl.BlockSpec((N_HEADS, chunk, HEAD_DIM), c3),
            pl.BlockSpec((1, HEAD_W), c2),
            pl.BlockSpec((2 * HEAD_W, D), c2),
            pl.BlockSpec((tm, D), row),
            pl.BlockSpec((1, D), c2),
        ],
        out_specs=[pl.BlockSpec((tm, D), row), pl.BlockSpec((tm, D), row)],
        out_shape=[jax.ShapeDtypeStruct((T, D), F32), jax.ShapeDtypeStruct((T, D), h_dtype)],
        scratch_shapes=[pltpu.VMEM((tm, HEAD_W), BF16)],
        compiler_params=_params("parallel"),
        name="mlp_merge_out",
    )(att, u, vm, wc, bc, hg_mlp, wo, x, g)


def _ffn_body(te_ref, nt_ref, h_ref, wg_ref, wu_ref, wd_ref, e_ref, o_ref, acc, hb, *, gated):
    del te_ref
    i = pl.program_id(0)
    f = pl.program_id(1)

    @pl.when(i < nt_ref[0])
    def _():
        @pl.when(f == 0)
        def _():
            hb[...] = h_ref[...].astype(BF16)
            acc[...] = jnp.zeros(acc.shape, F32)

        a = jnp.dot(hb[...], wg_ref[...].astype(BF16), preferred_element_type=F32)
        b = jnp.dot(hb[...], wu_ref[...].astype(BF16), preferred_element_type=F32)
        hh = (a * jax.nn.sigmoid(a) * b).astype(BF16)
        acc[...] += jnp.dot(hh, wd_ref[...].astype(BF16), preferred_element_type=F32)

        @pl.when(f == pl.num_programs(1) - 1)
        def _():
            if gated:
                o_ref[...] = e_ref[...] * acc[...]
            else:
                o_ref[...] = e_ref[...] + acc[...]

    @pl.when((i >= nt_ref[0]) & (f == 0))
    def _():
        o_ref[...] = jnp.zeros(o_ref.shape, o_ref.dtype)


def _ffn(tile_expert, n_tiles, h, wg, wu, wd, extra, *, tm, tf, gated):
    R, D = h.shape
    n_f = wg.shape[2] // tf
    last = n_f - 1
    ew = extra.shape[1]

    def live(i, nt):
        return jnp.minimum(i, nt[0] - 1)

    def fidx(i, f, nt):
        return jnp.where(i < nt[0], f, last)

    grid_spec = pltpu.PrefetchScalarGridSpec(
        num_scalar_prefetch=2,
        grid=(R // tm, n_f),
        in_specs=[
            pl.BlockSpec((tm, D), lambda i, f, te, nt: (live(i, nt), 0)),
            pl.BlockSpec((None, D, tf), lambda i, f, te, nt: (te[live(i, nt)], 0, fidx(i, f, nt))),
            pl.BlockSpec((None, D, tf), lambda i, f, te, nt: (te[live(i, nt)], 0, fidx(i, f, nt))),
            pl.BlockSpec((None, tf, D), lambda i, f, te, nt: (te[live(i, nt)], fidx(i, f, nt), 0)),
            pl.BlockSpec((tm, ew), lambda i, f, te, nt: (live(i, nt), 0)),
        ],
        out_specs=pl.BlockSpec((tm, D), lambda i, f, te, nt: (i, 0)),
        scratch_shapes=[pltpu.VMEM((tm, D), F32), pltpu.VMEM((tm, D), BF16)],
    )
    return pl.pallas_call(
        functools.partial(_ffn_body, gated=gated),
        grid_spec=grid_spec,
        out_shape=jax.ShapeDtypeStruct((R, D), F32),
        compiler_params=_params("arbitrary", "arbitrary"),
        name="moe_ffn" if gated else "dense_ffn",
    )(tile_expert, n_tiles, h, wg, wu, wd, extra)


def _router_body(h_ref, w_ref, eid_ref, gate_ref, *, n_exp):
    logits = jnp.dot(h_ref[...].astype(BF16), w_ref[...], preferred_element_type=F32)
    lane = lax.broadcasted_iota(I32, logits.shape, 1)
    big = logits.shape[1]
    l1 = jnp.where(lane < n_exp, logits, NEG_INF)
    m1 = jnp.max(l1, axis=1, keepdims=True)
    i1 = jnp.min(jnp.where(l1 == m1, lane, big), axis=1, keepdims=True)
    l2 = jnp.where(lane == i1, NEG_INF, l1)
    m2 = jnp.max(l2, axis=1, keepdims=True)
    i2 = jnp.min(jnp.where(l2 == m2, lane, big), axis=1, keepdims=True)
    e2 = jnp.exp(m2 - m1)
    p1 = 1.0 / (1.0 + e2)
    p2 = e2 / (1.0 + e2)
    eid_ref[...] = jnp.where(lane == 0, i1, jnp.where(lane == 1, i2, 0))
    gate_ref[...] = jnp.where(lane == 0, p1, jnp.where(lane == 1, p2, 0.0))


def _router(h, w_pad, n_exp, *, tm):
    T, D = h.shape
    return pl.pallas_call(
        functools.partial(_router_body, n_exp=n_exp),
        grid=(T // tm,),
        in_specs=[pl.BlockSpec((tm, D), lambda i: (i, 0)),
                  pl.BlockSpec((D, HEAD_DIM), lambda i: (0, 0))],
        out_specs=[pl.BlockSpec((tm, HEAD_DIM), lambda i: (i, 0)),
                   pl.BlockSpec((tm, HEAD_DIM), lambda i: (i, 0))],
        out_shape=[jax.ShapeDtypeStruct((T, HEAD_DIM), I32), jax.ShapeDtypeStruct((T, HEAD_DIM), F32)],
        compiler_params=_params("parallel"),
        name="router_top2",
    )(h, w_pad)


def _row_copy(src_hbm, row, dst_vmem, r, sem):
    return pltpu.make_async_copy(src_hbm.at[pl.ds(row, 1)], dst_vmem.at[pl.ds(r, 1)], sem)


def _gather_body(src_ref, hp_hbm, hs_hbm, o_ref, sem, *, tm, n_prompt):
    base = pl.program_id(0) * tm

    def issue(r, carry):
        t = src_ref[base + r]

        @pl.when(t < n_prompt)
        def _():
            _row_copy(hp_hbm, t, o_ref, r, sem).start()

        @pl.when(t >= n_prompt)
        def _():
            _row_copy(hs_hbm, t - n_prompt, o_ref, r, sem).start()

        return carry

    lax.fori_loop(0, tm, issue, 0)

    def drain(r, carry):
        _row_copy(hp_hbm, 0, o_ref, r, sem).wait()
        return carry

    lax.fori_loop(0, tm, drain, 0)


def _gather_rows(src, hp, hs, n_rows, *, tm):
    D = hp.shape[1]
    grid_spec = pltpu.PrefetchScalarGridSpec(
        num_scalar_prefetch=1,
        grid=(n_rows // tm,),
        in_specs=[pl.BlockSpec(memory_space=pl.ANY), pl.BlockSpec(memory_space=pl.ANY)],
        out_specs=pl.BlockSpec((tm, D), lambda i, s: (i, 0)),
        scratch_shapes=[pltpu.SemaphoreType.DMA(())],
    )
    return pl.pallas_call(
        functools.partial(_gather_body, tm=tm, n_prompt=hp.shape[0]),
        grid_spec=grid_spec,
        out_shape=jax.ShapeDtypeStruct((n_rows, D), hp.dtype),
        compiler_params=_params("arbitrary"),
        name="gather_rows",
    )(src, hp, hs)


def _combine_body(pos_ref, x_ref, y_hbm, o_ref, buf, sem, *, tm, tok0):
    base = (tok0 + pl.program_id(0) * tm) * TOP_K

    def issue(r, carry):
        for k in range(TOP_K):
            _row_copy(y_hbm, pos_ref[base + r * TOP_K + k], buf.at[k], r, sem).start()
        return carry

    lax.fori_loop(0, tm, issue, 0)

    def drain(r, carry):
        for k in range(TOP_K):
            _row_copy(y_hbm, 0, buf.at[k], r, sem).wait()
        return carry

    lax.fori_loop(0, tm, drain, 0)
    acc = x_ref[...]
    for k in range(TOP_K):
        acc = acc + buf[k]
    o_ref[...] = acc


def _combine(pos, x, y, *, tm, tok0):
    T, D = x.shape
    grid_spec = pltpu.PrefetchScalarGridSpec(
        num_scalar_prefetch=1,
        grid=(T // tm,),
        in_specs=[pl.BlockSpec((tm, D), lambda i, p: (i, 0)), pl.BlockSpec(memory_space=pl.ANY)],
        out_specs=pl.BlockSpec((tm, D), lambda i, p: (i, 0)),
        scratch_shapes=[pltpu.VMEM((TOP_K, tm, D), F32), pltpu.SemaphoreType.DMA(())],
    )
    return pl.pallas_call(
        functools.partial(_combine_body, tm=tm, tok0=tok0),
        grid_spec=grid_spec,
        out_shape=jax.ShapeDtypeStruct((T, D), F32),
        compiler_params=_params("arbitrary"),
        name="combine_rows",
    )(pos, x, y)


def _route_tables(eid, n_exp, tm):
    flat = eid.reshape(-1)
    n_assign = flat.shape[0]
    onehot = (flat[:, None] == jnp.arange(n_exp, dtype=I32)[None, :]).astype(I32)
    counts = jnp.sum(onehot, axis=0)
    rank = jnp.sum((jnp.cumsum(onehot, axis=0) - 1) * onehot, axis=1)
    padded = ((counts + tm - 1) // tm) * tm
    ends = jnp.cumsum(padded)
    pos = ((ends - padded)[flat] + rank).astype(I32)
    n_rows = (n_assign // tm + n_exp) * tm
    src = jnp.zeros((n_rows,), I32).at[pos].set(jnp.arange(n_assign, dtype=I32) // TOP_K)
    gate_slot = jnp.zeros((n_rows,), I32).at[pos].set(jnp.arange(n_assign, dtype=I32) + 1)
    n_tiles = (ends[-1] // tm).astype(I32).reshape(1)
    tile_expert = jnp.minimum(
        jnp.searchsorted(ends, jnp.arange(n_rows // tm, dtype=I32) * tm, side="right"), n_exp - 1).astype(I32)
    return pos, src, gate_slot, tile_expert, n_tiles, n_rows


def kernel(x_prompt, x_sample, cache_k, cache_v, cache_logf, page_table, norm_mix_g, w_in, b_f,
           q_norm_g, k_norm_g, sgu_norm_g, w_spatial, b_spatial, head_norm_g, w_out, norm_ffn_g,
           dense_w_gate, dense_w_up, dense_w_down, router_w, moe_w_gate, moe_w_up, moe_w_down):
    B, S, D = x_prompt.shape
    DB, DS, _ = x_sample.shape
    depth = w_in.shape[0]
    n_phys, page = cache_k.shape[1], cache_k.shape[2]
    chunk = w_spatial.shape[-1]
    n_exp = router_w.shape[-1]
    assert cache_k.shape[3] == N_HEADS and cache_k.shape[4] == HEAD_DIM
    assert w_in.shape[2] == 5 * HEAD_W + N_HEADS and D == 2 * HEAD_W
    Tp, Ts = B * S, DB * DS
    rows = N_HEADS * DS
    scale = HEAD_DIM ** -0.5

    xp = x_prompt.reshape(Tp, D)
    xs = x_sample.reshape(Ts, D)
    cache_lf2 = cache_logf.reshape(depth, n_phys, 1, page * N_HEADS)

    outs = {name: [] for name in ("kp", "vp", "fp", "ks", "vs", "fs", "ms")}
    for l in range(depth):
        w_main = jnp.concatenate([w_in[l][:, :3 * HEAD_W], w_in[l][:, 3 * HEAD_W + N_HEADS:]], axis=1).astype(BF16)
        wf = jnp.pad(w_in[l][:, 3 * HEAD_W:3 * HEAD_W + N_HEADS], ((0, 0), (0, HEAD_DIM - N_HEADS))).astype(BF16)
        bf = jnp.pad(b_f[l], (0, HEAD_DIM - N_HEADS)).reshape(1, HEAD_DIM)
        g_mix = norm_mix_g[l].reshape(1, D)
        gq = (q_norm_g[l] * scale).reshape(1, HEAD_W)
        gk = k_norm_g[l].reshape(1, HEAD_W)
        gm = sgu_norm_g[l].reshape(1, HEAD_W)
        hg_att = head_norm_g[l][:N_HEADS]
        hg_mlp = head_norm_g[l][N_HEADS:].reshape(1, HEAD_W)
        wo = w_out[l].astype(BF16)
        g_ffn = norm_ffn_g[l].reshape(1, D)
        wm = jnp.tril(w_spatial[l])
        wc_p = wm.astype(BF16)
        bc_p = jnp.broadcast_to(b_spatial[l][:, :, None], (N_HEADS, chunk, HEAD_DIM))
        wm_s = wm[:, :DS, :DS]
        wc_s = jnp.einsum("ab,hts->hatbs", jnp.eye(DB, dtype=F32), wm_s).reshape(N_HEADS, Ts, Ts).astype(BF16)
        bc_s = jnp.broadcast_to(jnp.tile(b_spatial[l][:, :DS], (1, DB))[:, :, None], (N_HEADS, Ts, HEAD_DIM))

        qp, kp, vp, up, vmp, lfp, lftp = _project(xp, g_mix, w_main, wf, bf, gq, gk, gm, tm=512)
        qs, ks, vs, us, vms, lfs, _ = _project(xs, g_mix, w_main, wf, bf, gq, gk, gm, tm=Ts)

        cp, ctp = _cumsum(lfp, lftp, seq=S)
        att_p = _fox_prompt(qp, kp, vp, cp, ctp, hg_att.reshape(N_HEADS, 1, HEAD_DIM), batch=B, seq=S, tq=256)

        q_rows = qs.reshape(DB, DS, N_HEADS, HEAD_DIM).transpose(0, 2, 1, 3).reshape(DB, rows, HEAD_DIM)
        kn = ks.reshape(DB, rows, HEAD_DIM)
        vn = vs.reshape(DB, rows, HEAD_DIM)
        lfs8 = lfs[:, :N_HEADS]
        att_rows = _fox_sample(l, page_table, q_rows, kn, vn, lfs8.reshape(DB, 1, rows), lfs8.reshape(DB, rows, 1),
                               jnp.repeat(hg_att, DS, axis=0), cache_k, cache_v, cache_lf2, pps=8)
        att_s = att_rows.reshape(DB, N_HEADS, DS, HEAD_DIM).transpose(0, 2, 1, 3).reshape(Ts, HEAD_W)

        moe_layer = l % 2 == 1
        h_dtype = F32 if moe_layer else BF16
        xp, hp = _merge(att_p, up, vmp, wc_p, bc_p, hg_mlp, wo, xp, g_ffn, tm=256, chunk=chunk, h_dtype=h_dtype)
        xs, hs = _merge(att_s, us, vms, wc_s, bc_s, hg_mlp, wo, xs, g_ffn, tm=Ts, chunk=Ts, h_dtype=h_dtype)

        i = l // 2
        if not moe_layer:
            d_ff = dense_w_gate.shape[-1]
            tf = 512
            pad = (-d_ff) % tf
            wg = jnp.pad(dense_w_gate[i], ((0, 0), (0, pad))).astype(BF16)[None]
            wu = jnp.pad(dense_w_up[i], ((0, 0), (0, pad))).astype(BF16)[None]
            wd = jnp.pad(dense_w_down[i], ((0, pad), (0, 0))).astype(BF16)[None]
            for name in ("p", "s"):
                x, h = (xp, hp) if name == "p" else (xs, hs)
                tm = min(512, x.shape[0])
                nt = x.shape[0] // tm
                y = _ffn(jnp.zeros((nt,), I32), jnp.full((1,), nt, I32), h, wg, wu, wd, x, tm=tm, tf=tf, gated=False)
                if name == "p":
                    xp = y
                else:
                    xs = y
        else:
            tm = 512
            rw = jnp.pad(router_w[i], ((0, 0), (0, HEAD_DIM - n_exp))).astype(BF16)
            eid_p, gate_p = _router(hp, rw, n_exp, tm=512)
            eid_s, gate_s = _router(hs, rw, n_exp, tm=Ts)
            eid = jnp.concatenate([eid_p[:, :TOP_K], eid_s[:, :TOP_K]], axis=0)
            gate = jnp.concatenate([gate_p[:, :TOP_K], gate_s[:, :TOP_K]], axis=0).reshape(-1)
            pos, src, gate_slot, tile_expert, n_tiles, n_rows = _route_tables(eid, n_exp, tm)
            gate_rows = jnp.concatenate([jnp.zeros((1,), F32), gate])[gate_slot].reshape(n_rows, 1)
            h_sorted = _gather_rows(src, hp, hs, n_rows, tm=256)
            y_sorted = _ffn(tile_expert, n_tiles, h_sorted, moe_w_gate[i], moe_w_up[i], moe_w_down[i], gate_rows,
                            tm=tm, tf=256, gated=True)
            xp = _combine(pos, xp, y_sorted, tm=256, tok0=0)
            xs = _combine(pos, xs, y_sorted, tm=Ts, tok0=Tp)

        outs["kp"].append(kp.reshape(B, S, N_HEADS, HEAD_DIM))
        outs["vp"].append(vp.reshape(B, S, N_HEADS, HEAD_DIM))
        outs["fp"].append(lfp[:, :N_HEADS].reshape(B, S, N_HEADS))
        outs["ks"].append(ks.reshape(DB, DS, N_HEADS, HEAD_DIM))
        outs["vs"].append(vs.reshape(DB, DS, N_HEADS, HEAD_DIM))
        outs["fs"].append(lfs8.reshape(DB, DS, N_HEADS))
        outs["ms"].append(vms.reshape(DB, DS, N_HEADS, HEAD_DIM))

    return (xp.reshape(B, S, D), xs.reshape(DB, DS, D),
            jnp.stack(outs["kp"]), jnp.stack(outs["vp"]), jnp.stack(outs["fp"]),
            jnp.stack(outs["ks"]), jnp.stack(outs["vs"]), jnp.stack(outs["fs"]), jnp.stack(outs["ms"]))
```

```python
import functools

import jax
import jax.numpy as jnp
from jax import lax
from jax.experimental import pallas as pl
from jax.experimental.pallas import tpu as pltpu

F32 = jnp.float32
BF16 = jnp.bfloat16
I32 = jnp.int32

HEAD_DIM = 128
N_HEADS = 8
HEAD_W = N_HEADS * HEAD_DIM
N_SECTIONS = 5
EPS = 1e-6
NEG_INF = float("-inf")
TOP_K = 2
ROW_TILE = 256
VMEM_LIMIT = 56 * 1024 * 1024


def _params(*sem):
    return pltpu.CompilerParams(dimension_semantics=sem, vmem_limit_bytes=VMEM_LIMIT)


def _nt_dot(a, b):
    return lax.dot_general(a, b, (((1,), (1,)), ((), ())), preferred_element_type=F32)


def _rms(x):
    return x * lax.rsqrt(jnp.mean(x * x, axis=-1, keepdims=True) + EPS)


def _split3(x):
    x1 = x.astype(BF16).astype(F32)
    r1 = x - x1
    x2 = r1.astype(BF16).astype(F32)
    x3 = (r1 - x2).astype(BF16).astype(F32)
    return x1, x2, x3


def _pick_lane(x, lane_idx):
    lane = lax.broadcasted_iota(I32, x.shape, 1)
    return jnp.sum(jnp.where(lane == lane_idx, x, 0.0), axis=1, keepdims=True)


def _proj_body(x_ref, g_ref, w_ref, wf_ref, bf_ref, gain_ref, z_ref, lf_ref, c_ref, h_scr, carry_scr,
               *, chunk, chunks_per_seq):
    i = pl.program_id(0)
    j = pl.program_id(1)
    tm = x_ref.shape[0]

    @pl.when(j == 0)
    def _():
        hb = (_rms(x_ref[...]) * g_ref[...]).astype(BF16)
        h_scr[...] = hb
        f = jnp.dot(hb, wf_ref[...], preferred_element_type=F32) + bf_ref[...]
        lf = jnp.minimum(f, 0.0) - jnp.log1p(jnp.exp(-jnp.abs(f)))
        lf_ref[...] = lf
        r = lax.broadcasted_iota(I32, (chunk, chunk), 0)
        c = lax.broadcasted_iota(I32, (chunk, chunk), 1)
        lower = (c <= r).astype(BF16)

        @pl.when(i == 0)
        def _():
            carry_scr[...] = jnp.zeros(carry_scr.shape, F32)

        carry = carry_scr[...]
        for n in range(tm // chunk):
            sl = slice(n * chunk, (n + 1) * chunk)
            first_of_seq = (i * (tm // chunk) + n) % chunks_per_seq == 0
            cs = jnp.where(first_of_seq, 0.0, carry)
            for part in _split3(lf[sl, :]):
                cs = cs + jnp.dot(lower, part.astype(BF16), preferred_element_type=F32)
            c_ref[sl, :] = cs
            carry = cs[chunk - 1:chunk, :]
        carry_scr[...] = carry

    z = jnp.dot(h_scr[...], w_ref[...], preferred_element_type=F32)
    normed = (j == 0) | (j == 1) | (j == 4)

    @pl.when(normed)
    def _():
        for h in range(N_HEADS):
            sl = slice(h * HEAD_DIM, (h + 1) * HEAD_DIM)
            z_ref[:, sl] = _rms(z[:, sl]) * gain_ref[:, sl]

    @pl.when(jnp.logical_not(normed))
    def _():
        z_ref[...] = z


def _project(x, g, w_main, wf, bf, gains, *, tm, seq):
    T, D = x.shape
    row = lambda i, j: (i, 0)
    const = lambda i, j: (0, 0)
    return pl.pallas_call(
        functools.partial(_proj_body, chunk=ROW_TILE, chunks_per_seq=seq // ROW_TILE),
        grid=(T // tm, N_SECTIONS),
        in_specs=[
            pl.BlockSpec((tm, D), row),
            pl.BlockSpec((1, D), const),
            pl.BlockSpec((D, HEAD_W), lambda i, j: (0, j)),
            pl.BlockSpec((D, HEAD_DIM), const),
            pl.BlockSpec((1, HEAD_DIM), const),
            pl.BlockSpec((None, 1, HEAD_W), lambda i, j: (j, 0, 0)),
        ],
        out_specs=[pl.BlockSpec((tm, HEAD_W), lambda i, j: (i, j)),
                   pl.BlockSpec((tm, HEAD_DIM), row),
                   pl.BlockSpec((tm, HEAD_DIM), row)],
        out_shape=[
            jax.ShapeDtypeStruct((T, N_SECTIONS * HEAD_W), F32),
            jax.ShapeDtypeStruct((T, HEAD_DIM), F32),
            jax.ShapeDtypeStruct((T, HEAD_DIM), F32),
        ],
        scratch_shapes=[pltpu.VMEM((tm, D), BF16), pltpu.VMEM((1, HEAD_DIM), F32)],
        compiler_params=_params("arbitrary", "arbitrary"),
        name="in_proj",
    )(x, g, w_main, wf, bf, gains)


def _bias_lanes(c_col, key_side):
    lane = lax.broadcasted_iota(I32, (1, HEAD_DIM), 1)
    c1, c2, c3 = _split3(c_col)
    if key_side:
        terms = (1.0, 1.0, 1.0, -c1, -c2, -c3)
    else:
        terms = (c1, c2, c3, 1.0, 1.0, 1.0)
    out = jnp.zeros((c_col.shape[0], HEAD_DIM), F32)
    for idx, t in enumerate(terms):
        out = jnp.where(lane == idx, t, out)
    return out.astype(BF16)


def _fox_prompt_body(q_ref, k_ref, v_ref, cq_ref, ck_ref, hg_ref, o_ref, ka, vb, *, tq, hp):
    hgrp = pl.program_id(1)
    qi = pl.program_id(2)

    @pl.when(qi == 0)
    def _():
        for hh in range(hp):
            sl = slice(hh * HEAD_DIM, (hh + 1) * HEAD_DIM)
            ka[hh, :, :HEAD_DIM] = k_ref[:, sl].astype(BF16)
            ka[hh, :, HEAD_DIM:] = _bias_lanes(_pick_lane(ck_ref[...], hgrp * hp + hh), True)
            vb[hh] = v_ref[:, sl].astype(BF16)

    qa = []
    for hh in range(hp):
        sl = slice(hh * HEAD_DIM, (hh + 1) * HEAD_DIM)
        ext = _bias_lanes(_pick_lane(cq_ref[...], hgrp * hp + hh), False)
        qa.append(jnp.concatenate([q_ref[:, sl].astype(BF16), ext], axis=1))

    def block(kj, carry, diagonal):
        start = pl.multiple_of(kj * tq, tq)
        out = []
        for hh in range(hp):
            m, l, acc = carry[hh]
            s = _nt_dot(qa[hh], ka[hh, pl.ds(start, tq), :])
            if diagonal:
                r = lax.broadcasted_iota(I32, (tq, tq), 0)
                c = lax.broadcasted_iota(I32, (tq, tq), 1)
                s = jnp.where(c <= r, s, NEG_INF)
            m_new = jnp.maximum(m, jnp.max(s, axis=1, keepdims=True))
            p = jnp.exp(s - m_new)
            alpha = jnp.exp(m - m_new)
            l = alpha * l + jnp.sum(p, axis=1, keepdims=True)
            acc = alpha * acc + jnp.dot(p.astype(BF16), vb[hh, pl.ds(start, tq), :],
                                        preferred_element_type=F32)
            out.append((m_new, l, acc))
        return tuple(out)

    one = (jnp.full((tq, 1), NEG_INF, F32), jnp.zeros((tq, 1), F32), jnp.zeros((tq, HEAD_DIM), F32))
    carry = lax.fori_loop(0, qi, lambda kj, cr: block(kj, cr, False), (one,) * hp)
    carry = block(qi, carry, True)
    for hh in range(hp):
        _, l, acc = carry[hh]
        o_ref[:, hh * HEAD_DIM:(hh + 1) * HEAD_DIM] = (_rms(acc / l) * hg_ref[hh]).astype(o_ref.dtype)


def _fox_prompt(z, c, hg, *, batch, seq, tq, hp):
    nq = seq // tq
    ngrp = N_HEADS // hp
    w = hp * HEAD_DIM
    return pl.pallas_call(
        functools.partial(_fox_prompt_body, tq=tq, hp=hp),
        grid=(batch, ngrp, nq),
        in_specs=[
            pl.BlockSpec((tq, w), lambda b, h, i: (b * nq + i, h)),
            pl.BlockSpec((seq, w), lambda b, h, i: (b, ngrp + h)),
            pl.BlockSpec((seq, w), lambda b, h, i: (b, 2 * ngrp + h)),
            pl.BlockSpec((tq, HEAD_DIM), lambda b, h, i: (b * nq + i, 0)),
            pl.BlockSpec((seq, HEAD_DIM), lambda b, h, i: (b, 0)),
            pl.BlockSpec((hp, 1, HEAD_DIM), lambda b, h, i: (h, 0, 0)),
        ],
        out_specs=pl.BlockSpec((tq, w), lambda b, h, i: (b * nq + i, h)),
        out_shape=jax.ShapeDtypeStruct((batch * seq, HEAD_W), BF16),
        scratch_shapes=[pltpu.VMEM((hp, seq, 2 * HEAD_DIM), BF16), pltpu.VMEM((hp, seq, HEAD_DIM), BF16)],
        compiler_params=_params("arbitrary", "arbitrary", "arbitrary"),
        name="fox_prompt",
    )(z, z, z, c, c, hg)


def _fox_sample_body(pt_ref, q_ref, kn_ref, vn_ref, lfr_ref, lfc_ref, hg_ref, *rest, pps, page):
    del pt_ref
    kp, vp, lp = rest[:pps], rest[pps:2 * pps], rest[2 * pps:3 * pps]
    o_ref, m_scr, l_scr, acc_scr, carry_scr, negq_scr, bias_scr, s_scr = rest[3 * pps:]
    j = pl.program_id(1)
    rows = q_ref.shape[0]
    dsq = rows // N_HEADS
    cols = page * N_HEADS

    rn = lax.broadcasted_iota(I32, (rows, rows), 0)
    cn = lax.broadcasted_iota(I32, (rows, rows), 1)
    keep_new = ((rn // dsq) == (cn % N_HEADS)) & ((cn // N_HEADS) <= (rn % dsq))
    cq_col = jnp.sum(jnp.where(keep_new, lfr_ref[...], 0.0), axis=1, keepdims=True)

    @pl.when(j == 0)
    def _():
        m_scr[...] = jnp.full(m_scr.shape, NEG_INF, F32)
        l_scr[...] = jnp.zeros(l_scr.shape, F32)
        acc_scr[...] = jnp.zeros(acc_scr.shape, F32)
        carry_scr[...] = jnp.zeros(carry_scr.shape, F32)
        rp = lax.broadcasted_iota(I32, (rows, cols), 0)
        cp = lax.broadcasted_iota(I32, (rows, cols), 1)
        negq_scr[...] = jnp.where((rp // dsq) == (cp % N_HEADS), cq_col, NEG_INF)

    def update(m_blk, scores, values):
        m = m_scr[...]
        m_new = jnp.maximum(m, m_blk)
        alpha = jnp.exp(m - m_new)
        psum = None
        pv = None
        for s_fn, v_fn in zip(scores, values):
            p = jnp.exp(s_fn() - m_new)
            psum = p if psum is None else psum + p
            d = jnp.dot(p.astype(BF16), v_fn(), preferred_element_type=F32)
            pv = d if pv is None else pv + d
        l_scr[...] = alpha * l_scr[...] + jnp.sum(psum, axis=1, keepdims=True)
        acc_scr[...] = alpha * acc_scr[...] + pv
        m_scr[...] = m_new

    lpp = jnp.concatenate([lp[p][...] for p in range(pps)], axis=0)
    lane = lax.broadcasted_iota(I32, (pps, cols), 1)
    sub = lax.broadcasted_iota(I32, (pps, cols), 0)
    n_steps = (page - 1).bit_length()
    tot = lpp
    for t in range(n_steps):
        tot = tot + pltpu.roll(tot, N_HEADS << t, axis=1)
    suf = jnp.where(lane < cols - N_HEADS, pltpu.roll(lpp, cols - N_HEADS, axis=1), 0.0)
    for t in range(n_steps):
        sh = N_HEADS << t
        suf = suf + jnp.where(lane < cols - sh, pltpu.roll(suf, cols - sh, axis=1), 0.0)
    pre = jnp.where(sub >= 1, pltpu.roll(tot, 1, axis=0), 0.0)
    for t in range((pps - 1).bit_length()):
        sh = 1 << t
        pre = pre + jnp.where(sub >= sh, pltpu.roll(pre, sh, axis=0), 0.0)
    carry = carry_scr[...]
    bias_scr[...] = suf + pre + carry
    carry_scr[...] = carry + jnp.sum(tot, axis=0, keepdims=True)

    q = q_ref[...]
    mx = None
    for p in range(pps):
        k2 = kp[p][...].reshape(cols, HEAD_DIM).astype(BF16)
        s = _nt_dot(q, k2) + negq_scr[...] + bias_scr[pl.ds(p, 1), :]
        s_scr[p] = s
        mx = s if mx is None else jnp.maximum(mx, s)
    update(jnp.max(mx, axis=1, keepdims=True),
           [functools.partial(lambda p: s_scr[p], p) for p in range(pps)],
           [functools.partial(lambda p: vp[p][...].reshape(cols, HEAD_DIM).astype(BF16), p) for p in range(pps)])

    @pl.when(j == pl.num_programs(1) - 1)
    def _():
        cn_row = jnp.sum(jnp.where(((rn % N_HEADS) == (cn % N_HEADS)) & ((rn // N_HEADS) <= (cn // N_HEADS)),
                                   lfc_ref[...], 0.0), axis=0, keepdims=True)
        s = jnp.where(keep_new, _nt_dot(q, kn_ref[...].astype(BF16)) + cq_col - cn_row, NEG_INF)
        update(jnp.max(s, axis=1, keepdims=True), [lambda: s], [lambda: vn_ref[...].astype(BF16)])
        o = acc_scr[...] / l_scr[...]
        o_ref[...] = (_rms(o) * hg_ref[...]).astype(o_ref.dtype)


def _fox_sample(layer, page_table, q, kn, vn, lfr, lfc, hg_rows, cache_k, cache_v, cache_lf2, *, pps):
    DB, n_pages = page_table.shape
    rows = q.shape[1]
    page = cache_k.shape[2]
    cols = page * N_HEADS
    n_steps = n_pages // pps

    def seq_map(b, j, pt):
        return (b, 0, 0)

    def page_map(p):
        def f(b, j, pt):
            return (layer, pt[b, n_pages - 1 - (j * pps + p)], 0, 0, 0)
        return f

    def lf_map(p):
        def f(b, j, pt):
            return (layer, pt[b, n_pages - 1 - (j * pps + p)], 0, 0)
        return f

    kv_spec = lambda p: pl.BlockSpec((None, None, page, N_HEADS, HEAD_DIM), page_map(p))
    grid_spec = pltpu.PrefetchScalarGridSpec(
        num_scalar_prefetch=1,
        grid=(DB, n_steps),
        in_specs=[
            pl.BlockSpec((None, rows, HEAD_DIM), seq_map),
            pl.BlockSpec((None, rows, HEAD_DIM), seq_map),
            pl.BlockSpec((None, rows, HEAD_DIM), seq_map),
            pl.BlockSpec((None, 1, rows), seq_map),
            pl.BlockSpec((None, rows, 1), seq_map),
            pl.BlockSpec((rows, HEAD_DIM), lambda b, j, pt: (0, 0)),
        ] + [kv_spec(p) for p in range(pps)] + [kv_spec(p) for p in range(pps)]
          + [pl.BlockSpec((None, None, 1, cols), lf_map(p)) for p in range(pps)],
        out_specs=pl.BlockSpec((None, rows, HEAD_DIM), seq_map),
        scratch_shapes=[pltpu.VMEM((rows, 1), F32), pltpu.VMEM((rows, 1), F32),
                        pltpu.VMEM((rows, HEAD_DIM), F32), pltpu.VMEM((1, cols), F32),
                        pltpu.VMEM((rows, cols), F32), pltpu.VMEM((pps, cols), F32),
                        pltpu.VMEM((pps, rows, cols), F32)],
    )
    return pl.pallas_call(
        functools.partial(_fox_sample_body, pps=pps, page=page),
        grid_spec=grid_spec,
        out_shape=jax.ShapeDtypeStruct((DB, rows, HEAD_DIM), BF16),
        compiler_params=_params("arbitrary", "arbitrary"),
        name="fox_sample",
    )(page_table, q, kn, vn, lfr, lfc, hg_rows,
      *([cache_k] * pps), *([cache_v] * pps), *([cache_lf2] * pps))


def _merge_body(attp_ref, atts_ref, u_ref, vm_ref, wcp_ref, bcp_ref, wcs_ref, bcs_ref, hg_ref, wo_ref,
                x_ref, g_ref, xo_ref, ho_ref, att_scr, mlp_scr, *, n_prompt_tiles):
    i = pl.program_id(0)
    tm = x_ref.shape[0]

    def gate_heads(wc_ref, bc_ref):
        chunk = wc_ref.shape[1]
        for h in range(N_HEADS):
            sl = slice(h * HEAD_DIM, (h + 1) * HEAD_DIM)
            for c in range(tm // chunk):
                rows = slice(c * chunk, (c + 1) * chunk)
                s = jnp.dot(wc_ref[h], vm_ref[rows, sl].astype(BF16), preferred_element_type=F32) + bc_ref[h]
                o = u_ref[rows, sl] * s
                mlp_scr[rows, sl] = (_rms(o) * hg_ref[:, sl]).astype(BF16)

    @pl.when(i < n_prompt_tiles)
    def _():
        att_scr[...] = attp_ref[...]
        gate_heads(wcp_ref, bcp_ref)

    @pl.when(i >= n_prompt_tiles)
    def _():
        att_scr[...] = atts_ref[...]
        gate_heads(wcs_ref, bcs_ref)

    y = jnp.dot(att_scr[...], wo_ref[:HEAD_W, :], preferred_element_type=F32)
    y = y + jnp.dot(mlp_scr[...], wo_ref[HEAD_W:, :], preferred_element_type=F32)
    xn = x_ref[...] + y
    xo_ref[...] = xn
    ho_ref[...] = (_rms(xn) * g_ref[...]).astype(ho_ref.dtype)


def _merge(att_p, att_s, z, wc_p, bc_p, wc_s, bc_s, hg_mlp, wo, x, g, *, h_dtype):
    T, D = x.shape
    tm = ROW_TILE
    n_prompt_tiles = att_p.shape[0] // tm
    assert att_s.shape[0] == tm and wc_s.shape[1] == tm
    chunk = wc_p.shape[1]
    row = lambda i: (i, 0)
    c2 = lambda i: (0, 0)
    c3 = lambda i: (0, 0, 0)
    return pl.pallas_call(
        functools.partial(_merge_body, n_prompt_tiles=n_prompt_tiles),
        grid=(T // tm,),
        in_specs=[
            pl.BlockSpec((tm, HEAD_W), lambda i: (jnp.minimum(i, n_prompt_tiles - 1), 0)),
            pl.BlockSpec((tm, HEAD_W), c2),
            pl.BlockSpec((tm, HEAD_W), lambda i: (i, 3)),
            pl.BlockSpec((tm, HEAD_W), lambda i: (i, 4)),
            pl.BlockSpec((N_HEADS, chunk, chunk), c3),
            pl.BlockSpec((N_HEADS, chunk, HEAD_DIM), c3),
            pl.BlockSpec((N_HEADS, tm, tm), c3),
            pl.BlockSpec((N_HEADS, tm, HEAD_DIM), c3),
            pl.BlockSpec((1, HEAD_W), c2),
            pl.BlockSpec((2 * HEAD_W, D), c2),
            pl.BlockSpec((tm, D), row),
            pl.BlockSpec((1, D), c2),
        ],
        out_specs=[pl.BlockSpec((tm, D), row), pl.BlockSpec((tm, D), row)],
        out_shape=[jax.ShapeDtypeStruct((T, D), F32), jax.ShapeDtypeStruct((T, D), h_dtype)],
        scratch_shapes=[pltpu.VMEM((tm, HEAD_W), BF16), pltpu.VMEM((tm, HEAD_W), BF16)],
        compiler_params=_params("arbitrary"),
        name="mlp_merge_out",
    )(att_p, att_s, z, z, wc_p, bc_p, wc_s, bc_s, hg_mlp, wo, x, g)


def _dense_body(h_ref, wg_ref, wu_ref, wd_ref, x_ref, o_ref):
    @pl.when(pl.program_id(1) == 0)
    def _():
        o_ref[...] = x_ref[...]

    h = h_ref[...]
    a = jnp.dot(h, wg_ref[...], preferred_element_type=F32)
    b = jnp.dot(h, wu_ref[...], preferred_element_type=F32)
    hh = (a * jax.nn.sigmoid(a) * b).astype(BF16)
    o_ref[...] += jnp.dot(hh, wd_ref[...], preferred_element_type=F32)


def _dense_ffn(h, wg, wu, wd, x, *, tm, tf):
    T, D = x.shape
    return pl.pallas_call(
        _dense_body,
        grid=(T // tm, wg.shape[1] // tf),
        in_specs=[
            pl.BlockSpec((tm, D), lambda i, f: (i, 0)),
            pl.BlockSpec((D, tf), lambda i, f: (0, f)),
            pl.BlockSpec((D, tf), lambda i, f: (0, f)),
            pl.BlockSpec((tf, D), lambda i, f: (f, 0)),
            pl.BlockSpec((tm, D), lambda i, f: (i, 0)),
        ],
        out_specs=pl.BlockSpec((tm, D), lambda i, f: (i, 0)),
        out_shape=jax.ShapeDtypeStruct((T, D), F32),
        compiler_params=_params("arbitrary", "arbitrary"),
        name="dense_ffn",
    )(h, wg, wu, wd, x)


def _router_body(h_ref, w_ref, eid_ref, gate_ref, *, n_exp):
    logits = jnp.dot(h_ref[...].astype(BF16), w_ref[...], preferred_element_type=F32)
    lane = lax.broadcasted_iota(I32, logits.shape, 1)
    big = logits.shape[1]
    l1 = jnp.where(lane < n_exp, logits, NEG_INF)
    m1 = jnp.max(l1, axis=1, keepdims=True)
    i1 = jnp.min(jnp.where(l1 == m1, lane, big), axis=1, keepdims=True)
    l2 = jnp.where(lane == i1, NEG_INF, l1)
    m2 = jnp.max(l2, axis=1, keepdims=True)
    i2 = jnp.min(jnp.where(l2 == m2, lane, big), axis=1, keepdims=True)
    e2 = jnp.exp(m2 - m1)
    p1 = 1.0 / (1.0 + e2)
    p2 = e2 / (1.0 + e2)
    eid_ref[...] = jnp.where(lane == 0, i1, jnp.where(lane == 1, i2, 0))
    gate_ref[...] = jnp.where(lane == 0, p1, jnp.where(lane == 1, p2, 0.0))


def _router(h, w_pad, n_exp, *, tm):
    T, D = h.shape
    return pl.pallas_call(
        functools.partial(_router_body, n_exp=n_exp),
        grid=(T // tm,),
        in_specs=[pl.BlockSpec((tm, D), lambda i: (i, 0)),
                  pl.BlockSpec((D, HEAD_DIM), lambda i: (0, 0))],
        out_specs=[pl.BlockSpec((tm, HEAD_DIM), lambda i: (i, 0)),
                   pl.BlockSpec((tm, HEAD_DIM), lambda i: (i, 0))],
        out_shape=[jax.ShapeDtypeStruct((T, HEAD_DIM), I32), jax.ShapeDtypeStruct((T, HEAD_DIM), F32)],
        compiler_params=_params("arbitrary"),
        name="router_top2",
    )(h, w_pad)


def _row_copy(src_hbm, row, dst_vmem, r, sem):
    return pltpu.make_async_copy(src_hbm.at[pl.ds(row, 1)], dst_vmem.at[pl.ds(r, 1)], sem)


def _moe_body(ie_ref, ifirst_ref, inb_ref, cnt_ref, src_ref,
              h_hbm, wg_ref, wu_ref, wd_ref, y_hbm,
              xbuf, acc, wgb, wub, wdb, stage, sem_in, sem_out, *, sub, unroll):
    del ie_ref
    w = pl.program_id(0)
    f = pl.program_id(1)
    n_f = pl.num_programs(1)
    n_items, n_used = cnt_ref[0], cnt_ref[1]
    n_total = y_hbm.shape[0] // sub

    def block_rows(r):
        return pl.ds(pl.multiple_of(r * sub, sub), sub)

    def out_copy(r_local, r_global):
        return pltpu.make_async_copy(acc.at[block_rows(r_local)], y_hbm.at[block_rows(r_global)], sem_out)

    @pl.when((w == 0) & (f == 0))
    def _():
        acc[block_rows(0)] = jnp.zeros((sub, acc.shape[1]), F32)

        def fill(r, carry):
            out_copy(0, r).start()
            return carry

        def drain(r, carry):
            out_copy(0, r).wait()
            return carry

        lax.fori_loop(n_used, n_total, fill, 0)
        lax.fori_loop(n_used, n_total, drain, 0)

    @pl.when(w < n_items)
    def _():
        first = ifirst_ref[w]
        nb = inb_ref[w]

        @pl.when(f == 0)
        def _():
            def issue(blk, slot):
                base = (first + blk) * sub

                def body(rr, carry):
                    for u in range(unroll):
                        r = rr * unroll + u
                        _row_copy(h_hbm, src_ref[base + r], stage.at[slot], r, sem_in.at[slot]).start(priority=u % 2)
                    return carry

                lax.fori_loop(0, sub // unroll, body, 0)

            issue(0, 0)

            def per_block(blk, carry):
                slot = blk % 2

                @pl.when(blk + 1 < nb)
                def _():
                    issue(blk + 1, 1 - slot)

                pltpu.make_async_copy(h_hbm.at[pl.ds(0, sub)], stage.at[slot], sem_in.at[slot]).wait()
                xbuf[block_rows(blk)] = stage[slot].astype(BF16)
                acc[block_rows(blk)] = jnp.zeros((sub, acc.shape[1]), F32)
                return carry

            lax.fori_loop(0, nb, per_block, 0)

        wgb[...] = wg_ref[...].astype(BF16)
        wub[...] = wu_ref[...].astype(BF16)
        wdb[...] = wd_ref[...].astype(BF16)

        def compute(r, carry):
            rows = block_rows(r)
            x = xbuf[rows]
            a = jnp.dot(x, wgb[...], preferred_element_type=F32)
            b = jnp.dot(x, wub[...], preferred_element_type=F32)
            hh = (a * jax.nn.sigmoid(a) * b).astype(BF16)
            acc[rows] += jnp.dot(hh, wdb[...], preferred_element_type=F32)
            return carry

        lax.fori_loop(0, nb, compute, 0)

        @pl.when(f == n_f - 1)
        def _():
            def start(r, carry):
                out_copy(r, first + r).start()
                return carry

            def wait(r, carry):
                out_copy(r, first + r).wait()
                return carry

            lax.fori_loop(0, nb, start, 0)
            lax.fori_loop(0, nb, wait, 0)


def _moe_ffn(tables, h, wg, wu, wd, *, tf, max_blocks):
    item_expert, item_first, item_nb, counts, src, n_rows = tables
    T, D = h.shape
    sub = ROW_TILE
    n_f = wg.shape[2] // tf
    last = n_f - 1

    def live(w, cnt):
        return jnp.minimum(w, cnt[0] - 1)

    def fidx(w, f, cnt):
        return jnp.where(w < cnt[0], f, last)

    grid_spec = pltpu.PrefetchScalarGridSpec(
        num_scalar_prefetch=5,
        grid=(item_expert.shape[0], n_f),
        in_specs=[
            pl.BlockSpec(memory_space=pl.ANY),
            pl.BlockSpec((None, D, tf), lambda w, f, ie, i1, i2, cnt, s: (ie[live(w, cnt)], 0, fidx(w, f, cnt))),
            pl.BlockSpec((None, D, tf), lambda w, f, ie, i1, i2, cnt, s: (ie[live(w, cnt)], 0, fidx(w, f, cnt))),
            pl.BlockSpec((None, tf, D), lambda w, f, ie, i1, i2, cnt, s: (ie[live(w, cnt)], fidx(w, f, cnt), 0)),
        ],
        out_specs=pl.BlockSpec(memory_space=pl.ANY),
        scratch_shapes=[
            pltpu.VMEM((max_blocks * sub, D), BF16),
            pltpu.VMEM((max_blocks * sub, D), F32),
            pltpu.VMEM((D, tf), BF16), pltpu.VMEM((D, tf), BF16), pltpu.VMEM((tf, D), BF16),
            pltpu.VMEM((2, sub, D), F32),
            pltpu.SemaphoreType.DMA((2,)), pltpu.SemaphoreType.DMA(()),
        ],
    )
    return pl.pallas_call(
        functools.partial(_moe_body, sub=sub, unroll=8),
        grid_spec=grid_spec,
        out_shape=jax.ShapeDtypeStruct((n_rows, D), F32),
        compiler_params=_params("arbitrary", "arbitrary"),
        name="moe_ffn",
    )(item_expert, item_first, item_nb, counts, src, h, wg, wu, wd)


def _combine_body(pos_ref, x_ref, gate_ref, y_hbm, o_ref, buf, sem, *, tm, unroll):
    base = pl.program_id(0) * tm * TOP_K

    def issue(rr, carry):
        for u in range(unroll):
            r = rr * unroll + u
            for k in range(TOP_K):
                _row_copy(y_hbm, pos_ref[base + r * TOP_K + k], buf.at[k], r, sem.at[k]).start(priority=k)
        return carry

    lax.fori_loop(0, tm // unroll, issue, 0)
    out = x_ref[...]
    for k in range(TOP_K):
        pltpu.make_async_copy(y_hbm.at[pl.ds(0, tm)], buf.at[k], sem.at[k]).wait()
        out = out + gate_ref[:, k:k + 1] * buf[k]
    o_ref[...] = out


def _combine(pos, x, gate, y, *, tm):
    T, D = x.shape
    grid_spec = pltpu.PrefetchScalarGridSpec(
        num_scalar_prefetch=1,
        grid=(T // tm,),
        in_specs=[pl.BlockSpec((tm, D), lambda i, p: (i, 0)),
                  pl.BlockSpec((tm, HEAD_DIM), lambda i, p: (i, 0)),
                  pl.BlockSpec(memory_space=pl.ANY)],
        out_specs=pl.BlockSpec((tm, D), lambda i, p: (i, 0)),
        scratch_shapes=[pltpu.VMEM((TOP_K, tm, D), F32), pltpu.SemaphoreType.DMA((TOP_K,))],
    )
    return pl.pallas_call(
        functools.partial(_combine_body, tm=tm, unroll=4),
        grid_spec=grid_spec,
        out_shape=jax.ShapeDtypeStruct((T, D), F32),
        compiler_params=_params("arbitrary"),
        name="combine_rows",
    )(pos, x, gate, y)


def _route_tables(eid, n_exp, sub, max_blocks):
    flat = eid.reshape(-1)
    n_assign = flat.shape[0]
    experts = jnp.arange(n_exp, dtype=I32)
    onehot = (flat[:, None] == experts[None, :]).astype(I32)
    counts = jnp.sum(onehot, axis=0)
    rank = jnp.sum((jnp.cumsum(onehot, axis=0) - 1) * onehot, axis=1)
    nsub = (counts + sub - 1) // sub
    sub_end = jnp.cumsum(nsub)
    sub_start = sub_end - nsub
    pos = (sub_start[flat] * sub + rank).astype(I32)
    n_rows = (n_assign // sub + n_exp) * sub
    src = jnp.zeros((n_rows,), I32).at[pos].set(jnp.arange(n_assign, dtype=I32) // TOP_K)
    items = (nsub + max_blocks - 1) // max_blocks
    item_end = jnp.cumsum(items)
    item_start = item_end - items
    n_items_max = n_exp + (n_rows // sub) // max_blocks
    w = jnp.arange(n_items_max, dtype=I32)
    item_expert = jnp.minimum(jnp.sum((item_end[None, :] <= w[:, None]).astype(I32), axis=1), n_exp - 1)
    k = w - item_start[item_expert]
    item_first = sub_start[item_expert] + k * max_blocks
    item_nb = jnp.clip(nsub[item_expert] - k * max_blocks, 0, max_blocks)
    counts2 = jnp.stack([item_end[-1], sub_end[-1]]).astype(I32)
    return pos, (item_expert.astype(I32), item_first.astype(I32), item_nb.astype(I32), counts2, src, n_rows)


def _row_tile_multiple(n, cap):
    best = ROW_TILE
    for m in range(ROW_TILE, cap + 1, ROW_TILE):
        if n % m == 0:
            best = m
    return best


def kernel(x_prompt, x_sample, cache_k, cache_v, cache_logf, page_table, norm_mix_g, w_in, b_f,
           q_norm_g, k_norm_g, sgu_norm_g, w_spatial, b_spatial, head_norm_g, w_out, norm_ffn_g,
           dense_w_gate, dense_w_up, dense_w_down, router_w, moe_w_gate, moe_w_up, moe_w_down):
    B, S, D = x_prompt.shape
    DB, DS, _ = x_sample.shape
    depth = w_in.shape[0]
    n_phys, page = cache_k.shape[1], cache_k.shape[2]
    chunk = w_spatial.shape[-1]
    n_exp = router_w.shape[-1]
    assert cache_k.shape[3] == N_HEADS and cache_k.shape[4] == HEAD_DIM
    assert w_in.shape[2] == N_SECTIONS * HEAD_W + N_HEADS and D == 2 * HEAD_W
    Tp, Ts = B * S, DB * DS
    T = Tp + Ts
    assert Ts == ROW_TILE and S % ROW_TILE == 0 and ROW_TILE % chunk == 0
    rows = N_HEADS * DS
    scale = HEAD_DIM ** -0.5
    f_lo = 3 * HEAD_W

    x = jnp.concatenate([x_prompt.reshape(Tp, D), x_sample.reshape(Ts, D)], axis=0)
    cache_lf2 = cache_logf.reshape(depth, n_phys, 1, page * N_HEADS)
    ones = jnp.ones((HEAD_W,), F32)

    outs = {name: [] for name in ("kp", "vp", "fp", "ks", "vs", "fs", "ms")}
    for l in range(depth):
        w_main = jnp.concatenate([w_in[l][:, :f_lo], w_in[l][:, f_lo + N_HEADS:]], axis=1).astype(BF16)
        wf = jnp.pad(w_in[l][:, f_lo:f_lo + N_HEADS], ((0, 0), (0, HEAD_DIM - N_HEADS))).astype(BF16)
        bf = jnp.pad(b_f[l], (0, HEAD_DIM - N_HEADS)).reshape(1, HEAD_DIM)
        g_mix = norm_mix_g[l].reshape(1, D)
        gains = jnp.stack([(q_norm_g[l] * scale).reshape(-1), k_norm_g[l].reshape(-1), ones, ones,
                           sgu_norm_g[l].reshape(-1)]).reshape(N_SECTIONS, 1, HEAD_W)
        hg_att = head_norm_g[l][:N_HEADS]
        hg_mlp = head_norm_g[l][N_HEADS:].reshape(1, HEAD_W)
        wo = w_out[l].astype(BF16)
        g_ffn = norm_ffn_g[l].reshape(1, D)
        wm = jnp.tril(w_spatial[l])
        wc_p = wm.astype(BF16)
        bc_p = jnp.broadcast_to(b_spatial[l][:, :, None], (N_HEADS, chunk, HEAD_DIM))
        wc_s = jnp.einsum("ab,hts->hatbs", jnp.eye(DB, dtype=F32), wm[:, :DS, :DS]).reshape(N_HEADS, Ts, Ts).astype(BF16)
        bc_s = jnp.broadcast_to(jnp.tile(b_spatial[l][:, :DS], (1, DB))[:, :, None], (N_HEADS, Ts, HEAD_DIM))

        z, lf, c = _project(x, g_mix, w_main, wf, bf, gains, tm=_row_tile_multiple(T, 3 * ROW_TILE), seq=S)

        att_p = _fox_prompt(z, c, hg_att.reshape(N_HEADS, 1, HEAD_DIM), batch=B, seq=S, tq=ROW_TILE, hp=2)

        zs = z[Tp:]
        q_rows = zs[:, :HEAD_W].astype(BF16).reshape(DB, DS, N_HEADS, HEAD_DIM).transpose(0, 2, 1, 3).reshape(DB, rows, HEAD_DIM)
        kn = zs[:, HEAD_W:2 * HEAD_W].reshape(DB, rows, HEAD_DIM)
        vn = zs[:, 2 * HEAD_W:3 * HEAD_W].reshape(DB, rows, HEAD_DIM)
        lfs8 = lf[Tp:, :N_HEADS]
        att_rows = _fox_sample(l, page_table, q_rows, kn, vn, lfs8.reshape(DB, 1, rows), lfs8.reshape(DB, rows, 1),
                               jnp.repeat(hg_att, DS, axis=0), cache_k, cache_v, cache_lf2, pps=8)
        att_s = att_rows.reshape(DB, N_HEADS, DS, HEAD_DIM).transpose(0, 2, 1, 3).reshape(Ts, HEAD_W)

        moe_layer = l % 2 == 1
        x, h = _merge(att_p, att_s, z, wc_p, bc_p, wc_s, bc_s, hg_mlp, wo, x, g_ffn,
                      h_dtype=F32 if moe_layer else BF16)

        i = l // 2
        if not moe_layer:
            tf = 512
            pad = (-dense_w_gate.shape[-1]) % tf
            wg = jnp.pad(dense_w_gate[i], ((0, 0), (0, pad))).astype(BF16)
            wu = jnp.pad(dense_w_up[i], ((0, 0), (0, pad))).astype(BF16)
            wd = jnp.pad(dense_w_down[i], ((0, pad), (0, 0))).astype(BF16)
            x = _dense_ffn(h, wg, wu, wd, x, tm=_row_tile_multiple(T, 3 * ROW_TILE), tf=tf)
        else:
            rw = jnp.pad(router_w[i], ((0, 0), (0, HEAD_DIM - n_exp))).astype(BF16)
            eid, gate = _router(h, rw, n_exp, tm=ROW_TILE)
            pos, tables = _route_tables(eid[:, :TOP_K], n_exp, ROW_TILE, 10)
            y_sorted = _moe_ffn(tables, h, moe_w_gate[i], moe_w_up[i], moe_w_down[i], tf=256, max_blocks=10)
            x = _combine(pos, x, gate, y_sorted, tm=ROW_TILE)

        outs["kp"].append(z[:Tp, HEAD_W:2 * HEAD_W].reshape(B, S, N_HEADS, HEAD_DIM))
        outs["vp"].append(z[:Tp, 2 * HEAD_W:3 * HEAD_W].reshape(B, S, N_HEADS, HEAD_DIM))
        outs["fp"].append(lf[:Tp, :N_HEADS].reshape(B, S, N_HEADS))
        outs["ks"].append(kn.reshape(DB, DS, N_HEADS, HEAD_DIM))
        outs["vs"].append(vn.reshape(DB, DS, N_HEADS, HEAD_DIM))
        outs["fs"].append(lfs8.reshape(DB, DS, N_HEADS))
        outs["ms"].append(zs[:, 4 * HEAD_W:].reshape(DB, DS, N_HEADS, HEAD_DIM))

    return (x[:Tp].reshape(B, S, D), x[Tp:].reshape(DB, DS, D),
            jnp.stack(outs["kp"]), jnp.stack(outs["vp"]), jnp.stack(outs["fp"]),
            jnp.stack(outs["ks"]), jnp.stack(outs["vs"]), jnp.stack(outs["fs"]), jnp.stack(outs["ms"]))
```

```python
import functools

import jax
import jax.numpy as jnp
from jax import lax
from jax.experimental import pallas as pl
from jax.experimental.pallas import tpu as pltpu

F32 = jnp.float32
BF16 = jnp.bfloat16
I32 = jnp.int32

HEAD_DIM = 128
N_HEADS = 8
HEAD_W = N_HEADS * HEAD_DIM
N_SECTIONS = 5
EPS = 1e-6
NEG_INF = float("-inf")
TOP_K = 2
ROW_TILE = 256
VMEM_LIMIT = 56 * 1024 * 1024


def _params(*sem):
    return pltpu.CompilerParams(dimension_semantics=sem, vmem_limit_bytes=VMEM_LIMIT)


def _nt_dot(a, b):
    return lax.dot_general(a, b, (((1,), (1,)), ((), ())), preferred_element_type=F32)


def _rms(x):
    return x * lax.rsqrt(jnp.mean(x * x, axis=-1, keepdims=True) + EPS)


def _split3(x):
    x1 = x.astype(BF16).astype(F32)
    r1 = x - x1
    x2 = r1.astype(BF16).astype(F32)
    x3 = (r1 - x2).astype(BF16).astype(F32)
    return x1, x2, x3


def _pick_lane(x, lane_idx):
    lane = lax.broadcasted_iota(I32, x.shape, 1)
    return jnp.sum(jnp.where(lane == lane_idx, x, 0.0), axis=1, keepdims=True)


def _proj_body(x_ref, g_ref, w_ref, wf_ref, bf_ref, gain_ref, z_ref, lf_ref, c_ref, h_scr, carry_scr,
               *, chunk, chunks_per_seq):
    i = pl.program_id(0)
    j = pl.program_id(1)
    tm = x_ref.shape[0]

    @pl.when(j == 0)
    def _():
        hb = (_rms(x_ref[...]) * g_ref[...]).astype(BF16)
        h_scr[...] = hb
        f = jnp.dot(hb, wf_ref[...], preferred_element_type=F32) + bf_ref[...]
        lf = jnp.minimum(f, 0.0) - jnp.log1p(jnp.exp(-jnp.abs(f)))
        lf_ref[...] = lf
        r = lax.broadcasted_iota(I32, (chunk, chunk), 0)
        c = lax.broadcasted_iota(I32, (chunk, chunk), 1)
        lower = (c <= r).astype(BF16)

        @pl.when(i == 0)
        def _():
            carry_scr[...] = jnp.zeros(carry_scr.shape, F32)

        carry = carry_scr[...]
        for n in range(tm // chunk):
            sl = slice(n * chunk, (n + 1) * chunk)
            first_of_seq = (i * (tm // chunk) + n) % chunks_per_seq == 0
            cs = jnp.where(first_of_seq, 0.0, carry)
            for part in _split3(lf[sl, :]):
                cs = cs + jnp.dot(lower, part.astype(BF16), preferred_element_type=F32)
            c_ref[sl, :] = cs
            carry = cs[chunk - 1:chunk, :]
        carry_scr[...] = carry

    z = jnp.dot(h_scr[...], w_ref[...], preferred_element_type=F32)
    normed = (j == 0) | (j == 1) | (j == 4)

    @pl.when(normed)
    def _():
        for h in range(N_HEADS):
            sl = slice(h * HEAD_DIM, (h + 1) * HEAD_DIM)
            z_ref[:, sl] = _rms(z[:, sl]) * gain_ref[:, sl]

    @pl.when(jnp.logical_not(normed))
    def _():
        z_ref[...] = z


def _project(x, g, w_main, wf, bf, gains, *, tm, seq):
    T, D = x.shape
    row = lambda i, j: (i, 0)
    const = lambda i, j: (0, 0)
    return pl.pallas_call(
        functools.partial(_proj_body, chunk=ROW_TILE, chunks_per_seq=seq // ROW_TILE),
        grid=(T // tm, N_SECTIONS),
        in_specs=[
            pl.BlockSpec((tm, D), row),
            pl.BlockSpec((1, D), const),
            pl.BlockSpec((D, HEAD_W), lambda i, j: (0, j)),
            pl.BlockSpec((D, HEAD_DIM), const),
            pl.BlockSpec((1, HEAD_DIM), const),
            pl.BlockSpec((None, 1, HEAD_W), lambda i, j: (j, 0, 0)),
        ],
        out_specs=[pl.BlockSpec((tm, HEAD_W), lambda i, j: (i, j)),
                   pl.BlockSpec((tm, HEAD_DIM), row),
                   pl.BlockSpec((tm, HEAD_DIM), row)],
        out_shape=[
            jax.ShapeDtypeStruct((T, N_SECTIONS * HEAD_W), F32),
            jax.ShapeDtypeStruct((T, HEAD_DIM), F32),
            jax.ShapeDtypeStruct((T, HEAD_DIM), F32),
        ],
        scratch_shapes=[pltpu.VMEM((tm, D), BF16), pltpu.VMEM((1, HEAD_DIM), F32)],
        compiler_params=_params("arbitrary", "arbitrary"),
        name="in_proj",
    )(x, g, w_main, wf, bf, gains)


def _bias_lanes(c_col, key_side):
    lane = lax.broadcasted_iota(I32, (1, HEAD_DIM), 1)
    c1, c2, c3 = _split3(c_col)
    if key_side:
        terms = (1.0, 1.0, 1.0, -c1, -c2, -c3)
    else:
        terms = (c1, c2, c3, 1.0, 1.0, 1.0)
    out = jnp.zeros((c_col.shape[0], HEAD_DIM), F32)
    for idx, t in enumerate(terms):
        out = jnp.where(lane == idx, t, out)
    return out.astype(BF16)


def _fox_prompt_body(q_ref, k_ref, v_ref, cq_ref, ck_ref, hg_ref, o_ref, ka, vt, m_scr, acc_scr, *, tq, hp):
    hgrp = pl.program_id(1)
    qi = pl.program_id(2)
    n_kv = vt.shape[1]
    ext = vt.shape[2] - HEAD_DIM

    @pl.when(qi == 0)
    def _():
        row = lax.broadcasted_iota(I32, (ext, tq), 0)
        ones_ext = jnp.where(row == 0, 1.0, 0.0).astype(BF16)
        for hh in range(hp):
            sl = slice(hh * HEAD_DIM, (hh + 1) * HEAD_DIM)
            ka[hh, :, :HEAD_DIM] = k_ref[:, sl].astype(BF16)
            ka[hh, :, HEAD_DIM:] = _bias_lanes(_pick_lane(ck_ref[...], hgrp * hp + hh), True)
            for n in range(n_kv):
                vt[hh, n, :HEAD_DIM, :] = v_ref[n * tq:(n + 1) * tq, sl].T.astype(BF16)
                vt[hh, n, HEAD_DIM:, :] = ones_ext

    qa = []
    for hh in range(hp):
        sl = slice(hh * HEAD_DIM, (hh + 1) * HEAD_DIM)
        bias = _bias_lanes(_pick_lane(cq_ref[...], hgrp * hp + hh), False)
        qa.append(jnp.concatenate([q_ref[:, sl].astype(BF16), bias], axis=1))
        m_scr[hh] = jnp.full((1, tq), NEG_INF, F32)
        acc_scr[hh] = jnp.zeros(acc_scr.shape[1:], F32)

    def block(kj, diagonal):
        start = pl.multiple_of(kj * tq, tq)
        st = [_nt_dot(ka[hh, pl.ds(start, tq), :], qa[hh]) for hh in range(hp)]
        vts = [vt[hh, kj] for hh in range(hp)]
        ms = [m_scr[hh] for hh in range(hp)]
        accs = [acc_scr[hh] for hh in range(hp)]
        if diagonal:
            key = lax.broadcasted_iota(I32, (tq, tq), 0)
            qry = lax.broadcasted_iota(I32, (tq, tq), 1)
            st = [jnp.where(key <= qry, s, NEG_INF) for s in st]
        m_new = [jnp.maximum(m, jnp.max(s, axis=0, keepdims=True)) for m, s in zip(ms, st)]
        pt = [jnp.exp(s - mn).astype(BF16) for s, mn in zip(st, m_new)]
        pv = [jnp.dot(v, p, preferred_element_type=F32) for v, p in zip(vts, pt)]
        new_acc = [jnp.exp(m - mn) * a + d for m, mn, a, d in zip(ms, m_new, accs, pv)]
        for hh in range(hp):
            acc_scr[hh] = new_acc[hh]
            m_scr[hh] = m_new[hh]

    def body(kj, carry):
        block(kj, False)
        return carry

    lax.fori_loop(0, qi, body, 0)
    block(qi, True)
    for hh in range(hp):
        acc = acc_scr[hh]
        ot = acc[:HEAD_DIM, :] / acc[HEAD_DIM:HEAD_DIM + 1, :]
        ot = ot * lax.rsqrt(jnp.mean(ot * ot, axis=0, keepdims=True) + EPS)
        o_ref[:, hh * HEAD_DIM:(hh + 1) * HEAD_DIM] = (ot.T * hg_ref[hh]).astype(o_ref.dtype)


def _fox_prompt(z, c, hg, *, batch, seq, tq, hp):
    nq = seq // tq
    ngrp = N_HEADS // hp
    w = hp * HEAD_DIM
    ext = 16
    return pl.pallas_call(
        functools.partial(_fox_prompt_body, tq=tq, hp=hp),
        grid=(batch, ngrp, nq),
        in_specs=[
            pl.BlockSpec((tq, w), lambda b, h, i: (b * nq + i, h)),
            pl.BlockSpec((seq, w), lambda b, h, i: (b, ngrp + h)),
            pl.BlockSpec((seq, w), lambda b, h, i: (b, 2 * ngrp + h)),
            pl.BlockSpec((tq, HEAD_DIM), lambda b, h, i: (b * nq + i, 0)),
            pl.BlockSpec((seq, HEAD_DIM), lambda b, h, i: (b, 0)),
            pl.BlockSpec((hp, 1, HEAD_DIM), lambda b, h, i: (h, 0, 0)),
        ],
        out_specs=pl.BlockSpec((tq, w), lambda b, h, i: (b * nq + i, h)),
        out_shape=jax.ShapeDtypeStruct((batch * seq, HEAD_W), BF16),
        scratch_shapes=[pltpu.VMEM((hp, seq, 2 * HEAD_DIM), BF16),
                        pltpu.VMEM((hp, nq, HEAD_DIM + ext, tq), BF16),
                        pltpu.VMEM((hp, 1, tq), F32), pltpu.VMEM((hp, HEAD_DIM + ext, tq), F32)],
        compiler_params=_params("arbitrary", "arbitrary", "arbitrary"),
        name="fox_prompt",
    )(z, z, z, c, c, hg)


def _fox_sample_body(pt_ref, q_ref, kn_ref, vn_ref, lfr_ref, lfc_ref, hg_ref, *rest, pps, page):
    del pt_ref
    kp, vp, lp = rest[:pps], rest[pps:2 * pps], rest[2 * pps:3 * pps]
    o_ref, m_scr, l_scr, acc_scr, carry_scr, negq_scr = rest[3 * pps:]
    j = pl.program_id(1)
    rows = q_ref.shape[0]
    dsq = rows // N_HEADS
    cols = page * N_HEADS

    rn = lax.broadcasted_iota(I32, (rows, rows), 0)
    cn = lax.broadcasted_iota(I32, (rows, rows), 1)
    keep_new = ((rn // dsq) == (cn % N_HEADS)) & ((cn // N_HEADS) <= (rn % dsq))
    cq_col = jnp.sum(jnp.where(keep_new, lfr_ref[...], 0.0), axis=1, keepdims=True)

    @pl.when(j == 0)
    def _():
        m_scr[...] = jnp.full(m_scr.shape, NEG_INF, F32)
        l_scr[...] = jnp.zeros(l_scr.shape, F32)
        acc_scr[...] = jnp.zeros(acc_scr.shape, F32)
        carry_scr[...] = jnp.zeros(carry_scr.shape, F32)
        rp = lax.broadcasted_iota(I32, (rows, cols), 0)
        cp = lax.broadcasted_iota(I32, (rows, cols), 1)
        negq_scr[...] = jnp.where((rp // dsq) == (cp % N_HEADS), cq_col, NEG_INF)

    def update(scores, values):
        m, l, acc = m_scr[...], l_scr[...], acc_scr[...]
        mx = functools.reduce(jnp.maximum, scores)
        m_new = jnp.maximum(m, jnp.max(mx, axis=1, keepdims=True))
        alpha = jnp.exp(m - m_new)
        ps = [jnp.exp(s - m_new) for s in scores]
        pv = [jnp.dot(p.astype(BF16), v, preferred_element_type=F32) for p, v in zip(ps, values)]
        l_scr[...] = alpha * l + jnp.sum(functools.reduce(jnp.add, ps), axis=1, keepdims=True)
        acc_scr[...] = alpha * acc + functools.reduce(jnp.add, pv)
        m_scr[...] = m_new

    q = q_ref[...]
    raw = [_nt_dot(q, kp[p][...].reshape(cols, HEAD_DIM).astype(BF16)) for p in range(pps)]

    lpp = jnp.concatenate([lp[p][...] for p in range(pps)], axis=0)
    lane = lax.broadcasted_iota(I32, (pps, cols), 1)
    sub = lax.broadcasted_iota(I32, (pps, cols), 0)
    n_steps = (page - 1).bit_length()
    tot = lpp
    for t in range(n_steps):
        tot = tot + pltpu.roll(tot, N_HEADS << t, axis=1)
    suf = jnp.where(lane < cols - N_HEADS, pltpu.roll(lpp, cols - N_HEADS, axis=1), 0.0)
    for t in range(n_steps):
        sh = N_HEADS << t
        suf = suf + jnp.where(lane < cols - sh, pltpu.roll(suf, cols - sh, axis=1), 0.0)
    pre = jnp.where(sub >= 1, pltpu.roll(tot, 1, axis=0), 0.0)
    for t in range((pps - 1).bit_length()):
        sh = 1 << t
        pre = pre + jnp.where(sub >= sh, pltpu.roll(pre, sh, axis=0), 0.0)
    carry = carry_scr[...]
    bias = suf + pre + carry

    negq = negq_scr[...]
    v2 = [vp[p][...].reshape(cols, HEAD_DIM).astype(BF16) for p in range(pps)]
    update([raw[p] + negq + bias[p:p + 1, :] for p in range(pps)], v2)
    carry_scr[...] = carry + jnp.sum(tot, axis=0, keepdims=True)

    @pl.when(j == pl.num_programs(1) - 1)
    def _():
        cn_row = jnp.sum(jnp.where(((rn % N_HEADS) == (cn % N_HEADS)) & ((rn // N_HEADS) <= (cn // N_HEADS)),
                                   lfc_ref[...], 0.0), axis=0, keepdims=True)
        s = jnp.where(keep_new, _nt_dot(q, kn_ref[...].astype(BF16)) + cq_col - cn_row, NEG_INF)
        update([s], [vn_ref[...].astype(BF16)])
        o = acc_scr[...] / l_scr[...]
        o_ref[...] = (_rms(o) * hg_ref[...]).astype(o_ref.dtype)


def _fox_sample(layer, page_table, q, kn, vn, lfr, lfc, hg_rows, cache_k, cache_v, cache_lf2, *, pps):
    DB, n_pages = page_table.shape
    rows = q.shape[1]
    page = cache_k.shape[2]
    cols = page * N_HEADS
    n_steps = n_pages // pps

    def seq_map(b, j, pt):
        return (b, 0, 0)

    def page_map(p):
        def f(b, j, pt):
            return (layer, pt[b, n_pages - 1 - (j * pps + p)], 0, 0, 0)
        return f

    def lf_map(p):
        def f(b, j, pt):
            return (layer, pt[b, n_pages - 1 - (j * pps + p)], 0, 0)
        return f

    kv_spec = lambda p: pl.BlockSpec((None, None, page, N_HEADS, HEAD_DIM), page_map(p))
    grid_spec = pltpu.PrefetchScalarGridSpec(
        num_scalar_prefetch=1,
        grid=(DB, n_steps),
        in_specs=[
            pl.BlockSpec((None, rows, HEAD_DIM), seq_map),
            pl.BlockSpec((None, rows, HEAD_DIM), seq_map),
            pl.BlockSpec((None, rows, HEAD_DIM), seq_map),
            pl.BlockSpec((None, 1, rows), seq_map),
            pl.BlockSpec((None, rows, 1), seq_map),
            pl.BlockSpec((rows, HEAD_DIM), lambda b, j, pt: (0, 0)),
        ] + [kv_spec(p) for p in range(pps)] + [kv_spec(p) for p in range(pps)]
          + [pl.BlockSpec((None, None, 1, cols), lf_map(p)) for p in range(pps)],
        out_specs=pl.BlockSpec((None, rows, HEAD_DIM), seq_map),
        scratch_shapes=[pltpu.VMEM((rows, 1), F32), pltpu.VMEM((rows, 1), F32),
                        pltpu.VMEM((rows, HEAD_DIM), F32), pltpu.VMEM((1, cols), F32),
                        pltpu.VMEM((rows, cols), F32)],
    )
    return pl.pallas_call(
        functools.partial(_fox_sample_body, pps=pps, page=page),
        grid_spec=grid_spec,
        out_shape=jax.ShapeDtypeStruct((DB, rows, HEAD_DIM), BF16),
        compiler_params=_params("arbitrary", "arbitrary"),
        name="fox_sample",
    )(page_table, q, kn, vn, lfr, lfc, hg_rows,
      *([cache_k] * pps), *([cache_v] * pps), *([cache_lf2] * pps))


def _merge_body(attp_ref, atts_ref, u_ref, vm_ref, wcp_ref, bcp_ref, wcs_ref, bcs_ref, hg_ref, wo_ref,
                x_ref, g_ref, xo_ref, ho_ref, att_scr, mlp_scr, *, n_prompt_tiles):
    i = pl.program_id(0)
    tm = x_ref.shape[0]

    def gate_heads(wc_ref, bc_ref):
        chunk = wc_ref.shape[1]
        for h in range(N_HEADS):
            sl = slice(h * HEAD_DIM, (h + 1) * HEAD_DIM)
            for c in range(tm // chunk):
                rows = slice(c * chunk, (c + 1) * chunk)
                s = jnp.dot(wc_ref[h], vm_ref[rows, sl].astype(BF16), preferred_element_type=F32) + bc_ref[h]
                o = u_ref[rows, sl] * s
                mlp_scr[rows, sl] = (_rms(o) * hg_ref[:, sl]).astype(BF16)

    @pl.when(i < n_prompt_tiles)
    def _():
        att_scr[...] = attp_ref[...]
        gate_heads(wcp_ref, bcp_ref)

    @pl.when(i >= n_prompt_tiles)
    def _():
        att_scr[...] = atts_ref[...]
        gate_heads(wcs_ref, bcs_ref)

    y = jnp.dot(att_scr[...], wo_ref[:HEAD_W, :], preferred_element_type=F32)
    y = y + jnp.dot(mlp_scr[...], wo_ref[HEAD_W:, :], preferred_element_type=F32)
    xn = x_ref[...] + y
    xo_ref[...] = xn
    ho_ref[...] = (_rms(xn) * g_ref[...]).astype(ho_ref.dtype)


def _merge(att_p, att_s, z, wc_p, bc_p, wc_s, bc_s, hg_mlp, wo, x, g, *, h_dtype):
    T, D = x.shape
    tm = ROW_TILE
    n_prompt_tiles = att_p.shape[0] // tm
    assert att_s.shape[0] == tm and wc_s.shape[1] == tm
    chunk = wc_p.shape[1]
    row = lambda i: (i, 0)
    c2 = lambda i: (0, 0)
    c3 = lambda i: (0, 0, 0)
    return pl.pallas_call(
        functools.partial(_merge_body, n_prompt_tiles=n_prompt_tiles),
        grid=(T // tm,),
        in_specs=[
            pl.BlockSpec((tm, HEAD_W), lambda i: (jnp.minimum(i, n_prompt_tiles - 1), 0)),
            pl.BlockSpec((tm, HEAD_W), c2),
            pl.BlockSpec((tm, HEAD_W), lambda i: (i, 3)),
            pl.BlockSpec((tm, HEAD_W), lambda i: (i, 4)),
            pl.BlockSpec((N_HEADS, chunk, chunk), c3),
            pl.BlockSpec((N_HEADS, chunk, HEAD_DIM), c3),
            pl.BlockSpec((N_HEADS, tm, tm), c3),
            pl.BlockSpec((N_HEADS, tm, HEAD_DIM), c3),
            pl.BlockSpec((1, HEAD_W), c2),
            pl.BlockSpec((2 * HEAD_W, D), c2),
            pl.BlockSpec((tm, D), row),
            pl.BlockSpec((1, D), c2),
        ],
        out_specs=[pl.BlockSpec((tm, D), row), pl.BlockSpec((tm, D), row)],
        out_shape=[jax.ShapeDtypeStruct((T, D), F32), jax.ShapeDtypeStruct((T, D), h_dtype)],
        scratch_shapes=[pltpu.VMEM((tm, HEAD_W), BF16), pltpu.VMEM((tm, HEAD_W), BF16)],
        compiler_params=_params("arbitrary"),
        name="mlp_merge_out",
    )(att_p, att_s, z, z, wc_p, bc_p, wc_s, bc_s, hg_mlp, wo, x, g)


def _dense_body(h_ref, wg_ref, wu_ref, wd_ref, *rest, has_tail):
    if has_tail:
        wgt_ref, wut_ref, wdt_ref, x_ref, o_ref = rest
    else:
        x_ref, o_ref = rest
    f = pl.program_id(1)

    @pl.when(f == 0)
    def _():
        o_ref[...] = x_ref[...]

    h = h_ref[...]

    def swiglu(wg, wu, wd):
        a = jnp.dot(h, wg[...].astype(BF16), preferred_element_type=F32)
        b = jnp.dot(h, wu[...].astype(BF16), preferred_element_type=F32)
        hh = (a * jax.nn.sigmoid(a) * b).astype(BF16)
        return jnp.dot(hh, wd[...].astype(BF16), preferred_element_type=F32)

    o_ref[...] += swiglu(wg_ref, wu_ref, wd_ref)

    if has_tail:
        @pl.when(f == pl.num_programs(1) - 1)
        def _():
            o_ref[...] += swiglu(wgt_ref, wut_ref, wdt_ref)


def _dense_ffn(h, wg, wu, wd, x, *, tm, tf):
    T, D = x.shape
    d_ff = wg.shape[1]
    n_main = d_ff // tf
    tail = d_ff - n_main * tf
    assert tail % HEAD_DIM == 0 and (n_main * tf) % max(tail, 1) == 0
    in_specs = [
        pl.BlockSpec((tm, D), lambda i, f: (i, 0)),
        pl.BlockSpec((D, tf), lambda i, f: (0, f)),
        pl.BlockSpec((D, tf), lambda i, f: (0, f)),
        pl.BlockSpec((tf, D), lambda i, f: (f, 0)),
    ]
    args = [h, wg, wu, wd]
    if tail:
        t_idx = n_main * tf // tail
        in_specs += [pl.BlockSpec((D, tail), lambda i, f: (0, t_idx)),
                     pl.BlockSpec((D, tail), lambda i, f: (0, t_idx)),
                     pl.BlockSpec((tail, D), lambda i, f: (t_idx, 0))]
        args += [wg, wu, wd]
    in_specs.append(pl.BlockSpec((tm, D), lambda i, f: (i, 0)))
    args.append(x)
    return pl.pallas_call(
        functools.partial(_dense_body, has_tail=bool(tail)),
        grid=(T // tm, n_main),
        in_specs=in_specs,
        out_specs=pl.BlockSpec((tm, D), lambda i, f: (i, 0)),
        out_shape=jax.ShapeDtypeStruct((T, D), F32),
        compiler_params=_params("arbitrary", "arbitrary"),
        name="dense_ffn",
    )(*args)


def _router_body(h_ref, w_ref, eid_ref, gate_ref, cnt_ref, *, n_exp):
    tm = h_ref.shape[0]
    logits = jnp.dot(h_ref[...].astype(BF16), w_ref[...], preferred_element_type=F32)
    lane = lax.broadcasted_iota(I32, logits.shape, 1)
    big = logits.shape[1]
    l1 = jnp.where(lane < n_exp, logits, NEG_INF)
    m1 = jnp.max(l1, axis=1, keepdims=True)
    i1 = jnp.min(jnp.where(l1 == m1, lane, big), axis=1, keepdims=True)
    l2 = jnp.where(lane == i1, NEG_INF, l1)
    m2 = jnp.max(l2, axis=1, keepdims=True)
    i2 = jnp.min(jnp.where(l2 == m2, lane, big), axis=1, keepdims=True)
    e2 = jnp.exp(m2 - m1)
    p1 = 1.0 / (1.0 + e2)
    p2 = e2 / (1.0 + e2)
    gate_ref[...] = jnp.where(lane == 0, p1, jnp.where(lane == 1, p2, 0.0))

    @pl.when(pl.program_id(0) == 0)
    def _():
        cnt_ref[...] = jnp.zeros(cnt_ref.shape, F32)

    hit1 = (lane == i1).astype(F32)
    hit2 = (lane == i2).astype(F32)
    r = lax.broadcasted_iota(I32, (tm, tm), 0)
    c = lax.broadcasted_iota(I32, (tm, tm), 1)
    before = jnp.dot((c < r).astype(BF16), (hit1 + hit2).astype(BF16), preferred_element_type=F32)
    before = before + cnt_ref[...]
    rank1 = jnp.sum(hit1 * before, axis=1, keepdims=True).astype(I32)
    rank2 = jnp.sum(hit2 * before, axis=1, keepdims=True).astype(I32)
    cnt_ref[...] += jnp.sum(hit1 + hit2, axis=0, keepdims=True)
    eid_ref[...] = jnp.where(lane == 0, i1, jnp.where(lane == 1, i2,
                             jnp.where(lane == TOP_K, rank1, jnp.where(lane == TOP_K + 1, rank2, 0))))


def _router(h, w_pad, n_exp, *, tm):
    T, D = h.shape
    return pl.pallas_call(
        functools.partial(_router_body, n_exp=n_exp),
        grid=(T // tm,),
        in_specs=[pl.BlockSpec((tm, D), lambda i: (i, 0)),
                  pl.BlockSpec((D, HEAD_DIM), lambda i: (0, 0))],
        out_specs=[pl.BlockSpec((tm, HEAD_DIM), lambda i: (i, 0)),
                   pl.BlockSpec((tm, HEAD_DIM), lambda i: (i, 0)),
                   pl.BlockSpec((1, HEAD_DIM), lambda i: (0, 0))],
        out_shape=[jax.ShapeDtypeStruct((T, HEAD_DIM), I32), jax.ShapeDtypeStruct((T, HEAD_DIM), F32),
                   jax.ShapeDtypeStruct((1, HEAD_DIM), F32)],
        compiler_params=_params("arbitrary"),
        name="router_top2",
    )(h, w_pad)


def _row_copy(src_hbm, row, dst_vmem, r, sem):
    return pltpu.make_async_copy(src_hbm.at[pl.ds(row, 1)], dst_vmem.at[pl.ds(r, 1)], sem)


def _moe_body(ie_ref, ifirst_ref, inb_ref, cnt_ref, src_ref,
              h_hbm, wg_ref, wu_ref, wd_ref, y_hbm,
              xbuf, acc, stage, sem_in, sem_out, *, sub, unroll):
    del ie_ref
    w = pl.program_id(0)
    f = pl.program_id(1)
    n_f = pl.num_programs(1)
    n_items, n_used = cnt_ref[0], cnt_ref[1]
    n_total = y_hbm.shape[0] // sub

    def block_rows(r):
        return pl.ds(pl.multiple_of(r * sub, sub), sub)

    def out_copy(r_local, r_global):
        return pltpu.make_async_copy(acc.at[block_rows(r_local)], y_hbm.at[block_rows(r_global)], sem_out)

    @pl.when((w == 0) & (f == 0))
    def _():
        acc[block_rows(0)] = jnp.zeros((sub, acc.shape[1]), F32)

        def fill(r, carry):
            out_copy(0, r).start()
            return carry

        def drain(r, carry):
            out_copy(0, r).wait()
            return carry

        lax.fori_loop(n_used, n_total, fill, 0)
        lax.fori_loop(n_used, n_total, drain, 0)

    @pl.when(w < n_items)
    def _():
        first = ifirst_ref[w]
        nb = inb_ref[w]

        @pl.when(f == 0)
        def _():
            def issue(blk, slot):
                base = (first + blk) * sub

                def body(rr, carry):
                    for u in range(unroll):
                        r = rr * unroll + u
                        _row_copy(h_hbm, src_ref[base + r], stage.at[slot], r, sem_in.at[slot]).start(priority=u % 2)
                    return carry

                lax.fori_loop(0, sub // unroll, body, 0)

            issue(0, 0)

            def per_block(blk, carry):
                slot = blk % 2

                @pl.when(blk + 1 < nb)
                def _():
                    issue(blk + 1, 1 - slot)

                pltpu.make_async_copy(h_hbm.at[pl.ds(0, sub)], stage.at[slot], sem_in.at[slot]).wait()
                xbuf[block_rows(blk)] = stage[slot].astype(BF16)
                acc[block_rows(blk)] = jnp.zeros((sub, acc.shape[1]), F32)
                return carry

            lax.fori_loop(0, nb, per_block, 0)

        def swiglu(blocks):
            xs = [xbuf[block_rows(r)] for r in blocks]
            wgv, wuv, wdv = (ref[...].astype(BF16) for ref in (wg_ref, wu_ref, wd_ref))
            ab = [(jnp.dot(x, wgv, preferred_element_type=F32),
                   jnp.dot(x, wuv, preferred_element_type=F32)) for x in xs]
            hs = [(a * jax.nn.sigmoid(a) * b).astype(BF16) for a, b in ab]
            ds = [jnp.dot(hh, wdv, preferred_element_type=F32) for hh in hs]
            for r, d in zip(blocks, ds):
                acc[block_rows(r)] += d

        def pair(rp, carry):
            swiglu([2 * rp, 2 * rp + 1])
            return carry

        lax.fori_loop(0, nb // 2, pair, 0)

        @pl.when(nb % 2 == 1)
        def _():
            swiglu([nb - 1])

        @pl.when(f == n_f - 1)
        def _():
            def start(r, carry):
                out_copy(r, first + r).start()
                return carry

            def wait(r, carry):
                out_copy(r, first + r).wait()
                return carry

            lax.fori_loop(0, nb, start, 0)
            lax.fori_loop(0, nb, wait, 0)


def _moe_ffn(tables, h, wg, wu, wd, *, tf, max_blocks):
    item_expert, item_first, item_nb, counts, src, n_rows = tables
    T, D = h.shape
    sub = ROW_TILE
    n_f = wg.shape[2] // tf
    last = n_f - 1

    def live(w, cnt):
        return jnp.minimum(w, cnt[0] - 1)

    def fidx(w, f, cnt):
        return jnp.where(w < cnt[0], f, last)

    grid_spec = pltpu.PrefetchScalarGridSpec(
        num_scalar_prefetch=5,
        grid=(item_expert.shape[0], n_f),
        in_specs=[
            pl.BlockSpec(memory_space=pl.ANY),
            pl.BlockSpec((None, D, tf), lambda w, f, ie, i1, i2, cnt, s: (ie[live(w, cnt)], 0, fidx(w, f, cnt))),
            pl.BlockSpec((None, D, tf), lambda w, f, ie, i1, i2, cnt, s: (ie[live(w, cnt)], 0, fidx(w, f, cnt))),
            pl.BlockSpec((None, tf, D), lambda w, f, ie, i1, i2, cnt, s: (ie[live(w, cnt)], fidx(w, f, cnt), 0)),
        ],
        out_specs=pl.BlockSpec(memory_space=pl.ANY),
        scratch_shapes=[
            pltpu.VMEM((max_blocks * sub, D), BF16),
            pltpu.VMEM((max_blocks * sub, D), F32),
            pltpu.VMEM((2, sub, D), F32),
            pltpu.SemaphoreType.DMA((2,)), pltpu.SemaphoreType.DMA(()),
        ],
    )
    return pl.pallas_call(
        functools.partial(_moe_body, sub=sub, unroll=8),
        grid_spec=grid_spec,
        out_shape=jax.ShapeDtypeStruct((n_rows, D), F32),
        compiler_params=_params("arbitrary", "arbitrary"),
        name="moe_ffn",
    )(item_expert, item_first, item_nb, counts, src, h, wg, wu, wd)


def _combine_body(pos_ref, x_ref, gate_ref, y_hbm, *rest, tm, unroll, split_tile):
    o_refs, (buf, sem) = rest[:-2], rest[-2:]
    base = pl.program_id(0) * tm * TOP_K

    def issue(rr, carry):
        for u in range(unroll):
            r = rr * unroll + u
            for k in range(TOP_K):
                _row_copy(y_hbm, pos_ref[base + r * TOP_K + k], buf.at[k], r, sem.at[k]).start(priority=k)
        return carry

    lax.fori_loop(0, tm // unroll, issue, 0)
    out = x_ref[...]
    for k in range(TOP_K):
        pltpu.make_async_copy(y_hbm.at[pl.ds(0, tm)], buf.at[k], sem.at[k]).wait()
        out = out + gate_ref[:, k:k + 1] * buf[k]
    if split_tile is None:
        o_refs[0][...] = out
    else:
        @pl.when(pl.program_id(0) < split_tile)
        def _():
            o_refs[0][...] = out

        @pl.when(pl.program_id(0) >= split_tile)
        def _():
            o_refs[1][...] = out


def _combine(pos, x, gate, y, *, tm, split_rows=None):
    T, D = x.shape
    if split_rows is None:
        split_tile = None
        out_specs = [pl.BlockSpec((tm, D), lambda i, p: (i, 0))]
        out_shape = [jax.ShapeDtypeStruct((T, D), F32)]
    else:
        split_tile = split_rows // tm
        out_specs = [pl.BlockSpec((tm, D), lambda i, p: (jnp.minimum(i, split_tile - 1), 0)),
                     pl.BlockSpec((tm, D), lambda i, p: (jnp.maximum(i - split_tile, 0), 0))]
        out_shape = [jax.ShapeDtypeStruct((split_rows, D), F32), jax.ShapeDtypeStruct((T - split_rows, D), F32)]
    grid_spec = pltpu.PrefetchScalarGridSpec(
        num_scalar_prefetch=1,
        grid=(T // tm,),
        in_specs=[pl.BlockSpec((tm, D), lambda i, p: (i, 0)),
                  pl.BlockSpec((tm, HEAD_DIM), lambda i, p: (i, 0)),
                  pl.BlockSpec(memory_space=pl.ANY)],
        out_specs=out_specs,
        scratch_shapes=[pltpu.VMEM((TOP_K, tm, D), F32), pltpu.SemaphoreType.DMA((TOP_K,))],
    )
    return pl.pallas_call(
        functools.partial(_combine_body, tm=tm, unroll=4, split_tile=split_tile),
        grid_spec=grid_spec,
        out_shape=out_shape,
        compiler_params=_params("arbitrary"),
        name="combine_rows",
    )(pos, x, gate, y)


def _route_tables(eid, rank, counts, sub, max_blocks):
    flat = eid.reshape(-1)
    rank = rank.reshape(-1)
    n_assign = flat.shape[0]
    n_exp = counts.shape[0]
    nsub = (counts + sub - 1) // sub
    sub_end = jnp.cumsum(nsub)
    sub_start = sub_end - nsub
    pos = (sub_start[flat] * sub + rank).astype(I32)
    n_rows = (n_assign // sub + n_exp) * sub
    src = jnp.zeros((n_rows,), I32).at[pos].set(jnp.arange(n_assign, dtype=I32) // TOP_K)
    items = (nsub + max_blocks - 1) // max_blocks
    item_end = jnp.cumsum(items)
    item_start = item_end - items
    n_items_max = n_exp + (n_rows // sub) // max_blocks
    w = jnp.arange(n_items_max, dtype=I32)
    item_expert = jnp.minimum(jnp.sum((item_end[None, :] <= w[:, None]).astype(I32), axis=1), n_exp - 1)
    k = w - item_start[item_expert]
    item_first = sub_start[item_expert] + k * max_blocks
    item_nb = jnp.clip(nsub[item_expert] - k * max_blocks, 0, max_blocks)
    counts2 = jnp.stack([item_end[-1], sub_end[-1]]).astype(I32)
    return pos, (item_expert.astype(I32), item_first.astype(I32), item_nb.astype(I32), counts2, src, n_rows)


def _row_tile_multiple(n, cap):
    best = ROW_TILE
    for m in range(ROW_TILE, cap + 1, ROW_TILE):
        if n % m == 0:
            best = m
    return best


def kernel(x_prompt, x_sample, cache_k, cache_v, cache_logf, page_table, norm_mix_g, w_in, b_f,
           q_norm_g, k_norm_g, sgu_norm_g, w_spatial, b_spatial, head_norm_g, w_out, norm_ffn_g,
           dense_w_gate, dense_w_up, dense_w_down, router_w, moe_w_gate, moe_w_up, moe_w_down):
    B, S, D = x_prompt.shape
    DB, DS, _ = x_sample.shape
    depth = w_in.shape[0]
    n_phys, page = cache_k.shape[1], cache_k.shape[2]
    chunk = w_spatial.shape[-1]
    n_exp = router_w.shape[-1]
    assert cache_k.shape[3] == N_HEADS and cache_k.shape[4] == HEAD_DIM
    assert w_in.shape[2] == N_SECTIONS * HEAD_W + N_HEADS and D == 2 * HEAD_W
    Tp, Ts = B * S, DB * DS
    T = Tp + Ts
    assert Ts == ROW_TILE and S % ROW_TILE == 0 and ROW_TILE % chunk == 0
    rows = N_HEADS * DS
    scale = HEAD_DIM ** -0.5
    f_lo = 3 * HEAD_W

    x = jnp.concatenate([x_prompt.reshape(Tp, D), x_sample.reshape(Ts, D)], axis=0)
    cache_lf2 = cache_logf.reshape(depth, n_phys, 1, page * N_HEADS)
    ones = jnp.ones((HEAD_W,), F32)

    outs = {name: [] for name in ("kp", "vp", "fp", "ks", "vs", "fs", "ms")}
    for l in range(depth):
        w_main = jnp.concatenate([w_in[l][:, :f_lo], w_in[l][:, f_lo + N_HEADS:]], axis=1).astype(BF16)
        wf = jnp.pad(w_in[l][:, f_lo:f_lo + N_HEADS], ((0, 0), (0, HEAD_DIM - N_HEADS))).astype(BF16)
        bf = jnp.pad(b_f[l], (0, HEAD_DIM - N_HEADS)).reshape(1, HEAD_DIM)
        g_mix = norm_mix_g[l].reshape(1, D)
        gains = jnp.stack([(q_norm_g[l] * scale).reshape(-1), k_norm_g[l].reshape(-1), ones, ones,
                           sgu_norm_g[l].reshape(-1)]).reshape(N_SECTIONS, 1, HEAD_W)
        hg_att = head_norm_g[l][:N_HEADS]
        hg_mlp = head_norm_g[l][N_HEADS:].reshape(1, HEAD_W)
        wo = w_out[l].astype(BF16)
        g_ffn = norm_ffn_g[l].reshape(1, D)
        wm = jnp.tril(w_spatial[l])
        wc_p = wm.astype(BF16)
        bc_p = jnp.broadcast_to(b_spatial[l][:, :, None], (N_HEADS, chunk, HEAD_DIM))
        wc_s = jnp.einsum("ab,hts->hatbs", jnp.eye(DB, dtype=F32), wm[:, :DS, :DS]).reshape(N_HEADS, Ts, Ts).astype(BF16)
        bc_s = jnp.broadcast_to(jnp.tile(b_spatial[l][:, :DS], (1, DB))[:, :, None], (N_HEADS, Ts, HEAD_DIM))

        z, lf, c = _project(x, g_mix, w_main, wf, bf, gains, tm=_row_tile_multiple(T, 3 * ROW_TILE), seq=S)

        att_p = _fox_prompt(z, c, hg_att.reshape(N_HEADS, 1, HEAD_DIM), batch=B, seq=S, tq=ROW_TILE, hp=4)

        zs = z[Tp:]
        q_rows = zs[:, :HEAD_W].astype(BF16).reshape(DB, DS, N_HEADS, HEAD_DIM).transpose(0, 2, 1, 3).reshape(DB, rows, HEAD_DIM)
        kn = zs[:, HEAD_W:2 * HEAD_W].reshape(DB, rows, HEAD_DIM)
        vn = zs[:, 2 * HEAD_W:3 * HEAD_W].reshape(DB, rows, HEAD_DIM)
        lfs8 = lf[Tp:, :N_HEADS]
        att_rows = _fox_sample(l, page_table, q_rows, kn, vn, lfs8.reshape(DB, 1, rows), lfs8.reshape(DB, rows, 1),
                               jnp.repeat(hg_att, DS, axis=0), cache_k, cache_v, cache_lf2, pps=8)
        att_s = att_rows.reshape(DB, N_HEADS, DS, HEAD_DIM).transpose(0, 2, 1, 3).reshape(Ts, HEAD_W)

        moe_layer = l % 2 == 1
        x, h = _merge(att_p, att_s, z, wc_p, bc_p, wc_s, bc_s, hg_mlp, wo, x, g_ffn,
                      h_dtype=F32 if moe_layer else BF16)

        i = l // 2
        if not moe_layer:
            x = _dense_ffn(h, dense_w_gate[i], dense_w_up[i], dense_w_down[i], x,
                           tm=_row_tile_multiple(T, 3 * ROW_TILE), tf=256)
        else:
            rw = jnp.pad(router_w[i], ((0, 0), (0, HEAD_DIM - n_exp))).astype(BF16)
            eid, gate, counts = _router(h, rw, n_exp, tm=ROW_TILE)
            pos, tables = _route_tables(eid[:, :TOP_K], eid[:, TOP_K:2 * TOP_K], counts[0, :n_exp].astype(I32),
                                        ROW_TILE, 10)
            y_sorted = _moe_ffn(tables, h, moe_w_gate[i], moe_w_up[i], moe_w_down[i], tf=256, max_blocks=10)
            if l == depth - 1:
                x_split = _combine(pos, x, gate, y_sorted, tm=ROW_TILE, split_rows=Tp)
            else:
                x = _combine(pos, x, gate, y_sorted, tm=ROW_TILE)[0]

        outs["kp"].append(z[:Tp, HEAD_W:2 * HEAD_W].reshape(B, S, N_HEADS, HEAD_DIM))
        outs["vp"].append(z[:Tp, 2 * HEAD_W:3 * HEAD_W].reshape(B, S, N_HEADS, HEAD_DIM))
        outs["fp"].append(lf[:Tp, :N_HEADS].reshape(B, S, N_HEADS))
        outs["ks"].append(kn.reshape(DB, DS, N_HEADS, HEAD_DIM))
        outs["vs"].append(vn.reshape(DB, DS, N_HEADS, HEAD_DIM))
        outs["fs"].append(lfs8.reshape(DB, DS, N_HEADS))
        outs["ms"].append(zs[:, 4 * HEAD_W:].reshape(DB, DS, N_HEADS, HEAD_DIM))

    if depth % 2 == 0:
        y_p, y_s = x_split
    else:
        y_p, y_s = x[:Tp], x[Tp:]
    return (y_p.reshape(B, S, D), y_s.reshape(DB, DS, D),
            jnp.stack(outs["kp"]), jnp.stack(outs["vp"]), jnp.stack(outs["fp"]),
            jnp.stack(outs["ks"]), jnp.stack(outs["vs"]), jnp.stack(outs["fs"]), jnp.stack(outs["ms"]))
```

```python
import functools

import jax
import jax.numpy as jnp
from jax import lax
from jax.experimental import pallas as pl
from jax.experimental.pallas import tpu as pltpu

F32 = jnp.float32
BF16 = jnp.bfloat16
I32 = jnp.int32

HEAD_DIM = 128
N_HEADS = 8
HEAD_W = N_HEADS * HEAD_DIM
N_SECTIONS = 5
EPS = 1e-6
NEG_INF = float("-inf")
TOP_K = 2
ROW_TILE = 256
VMEM_LIMIT = 56 * 1024 * 1024


def _params(*sem):
    return pltpu.CompilerParams(dimension_semantics=sem, vmem_limit_bytes=VMEM_LIMIT)


def _nt_dot(a, b):
    return lax.dot_general(a, b, (((1,), (1,)), ((), ())), preferred_element_type=F32)


def _rms(x):
    return x * lax.rsqrt(jnp.mean(x * x, axis=-1, keepdims=True) + EPS)


def _split3(x):
    x1 = x.astype(BF16).astype(F32)
    r1 = x - x1
    x2 = r1.astype(BF16).astype(F32)
    x3 = (r1 - x2).astype(BF16).astype(F32)
    return x1, x2, x3


def _pick_lane(x, lane_idx):
    lane = lax.broadcasted_iota(I32, x.shape, 1)
    return jnp.sum(jnp.where(lane == lane_idx, x, 0.0), axis=1, keepdims=True)


def _proj_body(x_ref, g_ref, wa_ref, wb_ref, wf_ref, bf_ref, gain_ref, z_ref, lf_ref, c_ref, h_scr, carry_scr,
               *, chunk, chunks_per_seq, n_direct):
    i = pl.program_id(0)
    j = pl.program_id(1)
    n_chunks = x_ref.shape[0] // chunk

    def section(jj):
        first = jj == 0
        normed = jj in (0, 1, 4)
        w = (wa_ref if jj < n_direct else wb_ref)[...].astype(BF16)
        if first:
            r = lax.broadcasted_iota(I32, (chunk, chunk), 0)
            c = lax.broadcasted_iota(I32, (chunk, chunk), 1)
            lower = (c <= r).astype(BF16)

            @pl.when(i == 0)
            def _():
                carry_scr[...] = jnp.zeros(carry_scr.shape, F32)

            carry = carry_scr[...]
        for n in range(n_chunks):
            rows = slice(n * chunk, (n + 1) * chunk)
            if first:
                hb = (_rms(x_ref[rows, :]) * g_ref[...]).astype(BF16)
                h_scr[rows, :] = hb
                f = jnp.dot(hb, wf_ref[...], preferred_element_type=F32) + bf_ref[...]
                lf = jnp.minimum(f, 0.0) - jnp.log1p(jnp.exp(-jnp.abs(f)))
                lf_ref[rows, :] = lf
                first_of_seq = (i * n_chunks + n) % chunks_per_seq == 0
                cs = jnp.where(first_of_seq, 0.0, carry)
                for part in _split3(lf):
                    cs = cs + jnp.dot(lower, part.astype(BF16), preferred_element_type=F32)
                c_ref[rows, :] = cs
                carry = cs[chunk - 1:chunk, :]
            else:
                hb = h_scr[rows, :]
            z = jnp.dot(hb, w, preferred_element_type=F32)
            if normed:
                for h in range(N_HEADS):
                    sl = slice(h * HEAD_DIM, (h + 1) * HEAD_DIM)
                    z_ref[rows, sl] = _rms(z[:, sl]) * gain_ref[:, sl]
            else:
                z_ref[rows, :] = z
        if first:
            carry_scr[...] = carry

    for jj in range(N_SECTIONS):
        pl.when(j == jj)(functools.partial(section, jj))


def _project(x, g, w_in, layer, w_tail, wf, bf, gains, *, tm, seq):
    T, D = x.shape
    n_direct = N_SECTIONS - w_tail.shape[1] // HEAD_W
    row = lambda i, j: (i, 0)
    const = lambda i, j: (0, 0)
    return pl.pallas_call(
        functools.partial(_proj_body, chunk=ROW_TILE, chunks_per_seq=seq // ROW_TILE, n_direct=n_direct),
        grid=(T // tm, N_SECTIONS),
        in_specs=[
            pl.BlockSpec((tm, D), row),
            pl.BlockSpec((1, D), const),
            pl.BlockSpec((None, D, HEAD_W), lambda i, j: (layer, 0, jnp.minimum(j, n_direct - 1))),
            pl.BlockSpec((D, HEAD_W), lambda i, j: (0, jnp.maximum(j - n_direct, 0))),
            pl.BlockSpec((D, HEAD_DIM), const),
            pl.BlockSpec((1, HEAD_DIM), const),
            pl.BlockSpec((None, 1, HEAD_W), lambda i, j: (j, 0, 0)),
        ],
        out_specs=[pl.BlockSpec((tm, HEAD_W), lambda i, j: (i, j)),
                   pl.BlockSpec((tm, HEAD_DIM), row),
                   pl.BlockSpec((tm, HEAD_DIM), row)],
        out_shape=[
            jax.ShapeDtypeStruct((T, N_SECTIONS * HEAD_W), F32),
            jax.ShapeDtypeStruct((T, HEAD_DIM), F32),
            jax.ShapeDtypeStruct((T, HEAD_DIM), F32),
        ],
        scratch_shapes=[pltpu.VMEM((tm, D), BF16), pltpu.VMEM((1, HEAD_DIM), F32)],
        compiler_params=_params("arbitrary", "arbitrary"),
        name="in_proj",
    )(x, g, w_in, w_tail, wf, bf, gains)


def _bias_lanes(c_col, key_side):
    lane = lax.broadcasted_iota(I32, (1, HEAD_DIM), 1)
    c1, c2, c3 = _split3(c_col)
    if key_side:
        terms = (1.0, 1.0, 1.0, -c1, -c2, -c3)
    else:
        terms = (c1, c2, c3, 1.0, 1.0, 1.0)
    out = jnp.zeros((c_col.shape[0], HEAD_DIM), F32)
    for idx, t in enumerate(terms):
        out = jnp.where(lane == idx, t, out)
    return out.astype(BF16)


def _fox_prompt_body(q_ref, k_ref, v_ref, cq_ref, ck_ref, hg_ref, o_ref, ka, vt, m_scr, acc_scr, *, tq, hp):
    hgrp = pl.program_id(1)
    qi = pl.program_id(2)
    n_kv = vt.shape[1]
    ext = vt.shape[2] - HEAD_DIM

    @pl.when(qi == 0)
    def _():
        row = lax.broadcasted_iota(I32, (ext, tq), 0)
        ones_ext = jnp.where(row == 0, 1.0, 0.0).astype(BF16)
        for hh in range(hp):
            sl = slice(hh * HEAD_DIM, (hh + 1) * HEAD_DIM)
            ka[hh, :, :HEAD_DIM] = k_ref[:, sl].astype(BF16)
            ka[hh, :, HEAD_DIM:] = _bias_lanes(_pick_lane(ck_ref[...], hgrp * hp + hh), True)
            for n in range(n_kv):
                vt[hh, n, :HEAD_DIM, :] = v_ref[n * tq:(n + 1) * tq, sl].T.astype(BF16)
                vt[hh, n, HEAD_DIM:, :] = ones_ext

    qa = []
    for hh in range(hp):
        sl = slice(hh * HEAD_DIM, (hh + 1) * HEAD_DIM)
        bias = _bias_lanes(_pick_lane(cq_ref[...], hgrp * hp + hh), False)
        qa.append(jnp.concatenate([q_ref[:, sl].astype(BF16), bias], axis=1))
        m_scr[hh] = jnp.full((1, tq), NEG_INF, F32)
        acc_scr[hh] = jnp.zeros(acc_scr.shape[1:], F32)

    def block(kj, diagonal):
        start = pl.multiple_of(kj * tq, tq)
        st = [_nt_dot(ka[hh, pl.ds(start, tq), :], qa[hh]) for hh in range(hp)]
        vts = [vt[hh, kj] for hh in range(hp)]
        ms = [m_scr[hh] for hh in range(hp)]
        accs = [acc_scr[hh] for hh in range(hp)]
        if diagonal:
            key = lax.broadcasted_iota(I32, (tq, tq), 0)
            qry = lax.broadcasted_iota(I32, (tq, tq), 1)
            st = [jnp.where(key <= qry, s, NEG_INF) for s in st]
        m_new = [jnp.maximum(m, jnp.max(s, axis=0, keepdims=True)) for m, s in zip(ms, st)]
        pt = [jnp.exp(s - mn).astype(BF16) for s, mn in zip(st, m_new)]
        pv = [jnp.dot(v, p, preferred_element_type=F32) for v, p in zip(vts, pt)]
        new_acc = [jnp.exp(m - mn) * a + d for m, mn, a, d in zip(ms, m_new, accs, pv)]
        for hh in range(hp):
            acc_scr[hh] = new_acc[hh]
            m_scr[hh] = m_new[hh]

    def body(kj, carry):
        block(kj, False)
        return carry

    lax.fori_loop(0, qi, body, 0)
    block(qi, True)
    for hh in range(hp):
        acc = acc_scr[hh]
        ot = acc[:HEAD_DIM, :] / acc[HEAD_DIM:HEAD_DIM + 1, :]
        ot = ot * lax.rsqrt(jnp.mean(ot * ot, axis=0, keepdims=True) + EPS)
        o_ref[:, hh * HEAD_DIM:(hh + 1) * HEAD_DIM] = (ot.T * hg_ref[hh]).astype(o_ref.dtype)


def _fox_prompt(z, c, hg, *, batch, seq, tq, hp):
    nq = seq // tq
    ngrp = N_HEADS // hp
    w = hp * HEAD_DIM
    ext = 16
    return pl.pallas_call(
        functools.partial(_fox_prompt_body, tq=tq, hp=hp),
        grid=(batch, ngrp, nq),
        in_specs=[
            pl.BlockSpec((tq, w), lambda b, h, i: (b * nq + i, h)),
            pl.BlockSpec((seq, w), lambda b, h, i: (b, ngrp + h)),
            pl.BlockSpec((seq, w), lambda b, h, i: (b, 2 * ngrp + h)),
            pl.BlockSpec((tq, HEAD_DIM), lambda b, h, i: (b * nq + i, 0)),
            pl.BlockSpec((seq, HEAD_DIM), lambda b, h, i: (b, 0)),
            pl.BlockSpec((hp, 1, HEAD_DIM), lambda b, h, i: (h, 0, 0)),
        ],
        out_specs=pl.BlockSpec((tq, w), lambda b, h, i: (b * nq + i, h)),
        out_shape=jax.ShapeDtypeStruct((batch * seq, HEAD_W), BF16),
        scratch_shapes=[pltpu.VMEM((hp, seq, 2 * HEAD_DIM), BF16),
                        pltpu.VMEM((hp, nq, HEAD_DIM + ext, tq), BF16),
                        pltpu.VMEM((hp, 1, tq), F32), pltpu.VMEM((hp, HEAD_DIM + ext, tq), F32)],
        compiler_params=_params("arbitrary", "arbitrary", "arbitrary"),
        name="fox_prompt",
    )(z, z, z, c, c, hg)


def _fox_sample_body(pt_ref, q_ref, kn_ref, vn_ref, lfr_ref, lfc_ref, hg_ref, *rest, pps, page):
    del pt_ref
    kp, vp, lp = rest[:pps], rest[pps:2 * pps], rest[2 * pps:3 * pps]
    o_ref, m_scr, l_scr, acc_scr, carry_scr, negq_scr = rest[3 * pps:]
    j = pl.program_id(1)
    rows = q_ref.shape[0]
    dsq = rows // N_HEADS
    cols = page * N_HEADS

    rn = lax.broadcasted_iota(I32, (rows, rows), 0)
    cn = lax.broadcasted_iota(I32, (rows, rows), 1)
    keep_new = ((rn // dsq) == (cn % N_HEADS)) & ((cn // N_HEADS) <= (rn % dsq))
    cq_col = jnp.sum(jnp.where(keep_new, lfr_ref[...], 0.0), axis=1, keepdims=True)

    @pl.when(j == 0)
    def _():
        m_scr[...] = jnp.full(m_scr.shape, NEG_INF, F32)
        l_scr[...] = jnp.zeros(l_scr.shape, F32)
        acc_scr[...] = jnp.zeros(acc_scr.shape, F32)
        carry_scr[...] = jnp.zeros(carry_scr.shape, F32)
        rp = lax.broadcasted_iota(I32, (rows, cols), 0)
        cp = lax.broadcasted_iota(I32, (rows, cols), 1)
        negq_scr[...] = jnp.where((rp // dsq) == (cp % N_HEADS), cq_col, NEG_INF)

    def update(scores, values):
        m, l, acc = m_scr[...], l_scr[...], acc_scr[...]
        mx = functools.reduce(jnp.maximum, scores)
        m_new = jnp.maximum(m, jnp.max(mx, axis=1, keepdims=True))
        alpha = jnp.exp(m - m_new)
        ps = [jnp.exp(s - m_new) for s in scores]
        pv = [jnp.dot(p.astype(BF16), v, preferred_element_type=F32) for p, v in zip(ps, values)]
        l_scr[...] = alpha * l + jnp.sum(functools.reduce(jnp.add, ps), axis=1, keepdims=True)
        acc_scr[...] = alpha * acc + functools.reduce(jnp.add, pv)
        m_scr[...] = m_new

    q = q_ref[...]
    raw = [_nt_dot(q, kp[p][...].reshape(cols, HEAD_DIM).astype(BF16)) for p in range(pps)]

    lpp = jnp.concatenate([lp[p][...] for p in range(pps)], axis=0)
    lane = lax.broadcasted_iota(I32, (pps, cols), 1)
    sub = lax.broadcasted_iota(I32, (pps, cols), 0)
    n_steps = (page - 1).bit_length()
    tot = lpp
    for t in range(n_steps):
        tot = tot + pltpu.roll(tot, N_HEADS << t, axis=1)
    suf = jnp.where(lane < cols - N_HEADS, pltpu.roll(lpp, cols - N_HEADS, axis=1), 0.0)
    for t in range(n_steps):
        sh = N_HEADS << t
        suf = suf + jnp.where(lane < cols - sh, pltpu.roll(suf, cols - sh, axis=1), 0.0)
    pre = jnp.where(sub >= 1, pltpu.roll(tot, 1, axis=0), 0.0)
    for t in range((pps - 1).bit_length()):
        sh = 1 << t
        pre = pre + jnp.where(sub >= sh, pltpu.roll(pre, sh, axis=0), 0.0)
    carry = carry_scr[...]
    bias = suf + pre + carry

    negq = negq_scr[...]
    v2 = [vp[p][...].reshape(cols, HEAD_DIM).astype(BF16) for p in range(pps)]
    update([raw[p] + negq + bias[p:p + 1, :] for p in range(pps)], v2)
    carry_scr[...] = carry + jnp.sum(tot, axis=0, keepdims=True)

    @pl.when(j == pl.num_programs(1) - 1)
    def _():
        cn_row = jnp.sum(jnp.where(((rn % N_HEADS) == (cn % N_HEADS)) & ((rn // N_HEADS) <= (cn // N_HEADS)),
                                   lfc_ref[...], 0.0), axis=0, keepdims=True)
        s = jnp.where(keep_new, _nt_dot(q, kn_ref[...].astype(BF16)) + cq_col - cn_row, NEG_INF)
        update([s], [vn_ref[...].astype(BF16)])
        o = acc_scr[...] / l_scr[...]
        o_ref[...] = (_rms(o) * hg_ref[...]).astype(o_ref.dtype)


def _fox_sample(layer, page_table, q, kn, vn, lfr, lfc, hg_rows, cache_k, cache_v, cache_lf2, *, pps):
    DB, n_pages = page_table.shape
    rows = q.shape[1]
    page = cache_k.shape[2]
    cols = page * N_HEADS
    n_steps = n_pages // pps

    def seq_map(b, j, pt):
        return (b, 0, 0)

    def page_map(p):
        def f(b, j, pt):
            return (layer, pt[b, n_pages - 1 - (j * pps + p)], 0, 0, 0)
        return f

    def lf_map(p):
        def f(b, j, pt):
            return (layer, pt[b, n_pages - 1 - (j * pps + p)], 0, 0)
        return f

    kv_spec = lambda p: pl.BlockSpec((None, None, page, N_HEADS, HEAD_DIM), page_map(p))
    grid_spec = pltpu.PrefetchScalarGridSpec(
        num_scalar_prefetch=1,
        grid=(DB, n_steps),
        in_specs=[
            pl.BlockSpec((None, rows, HEAD_DIM), seq_map),
            pl.BlockSpec((None, rows, HEAD_DIM), seq_map),
            pl.BlockSpec((None, rows, HEAD_DIM), seq_map),
            pl.BlockSpec((None, 1, rows), seq_map),
            pl.BlockSpec((None, rows, 1), seq_map),
            pl.BlockSpec((rows, HEAD_DIM), lambda b, j, pt: (0, 0)),
        ] + [kv_spec(p) for p in range(pps)] + [kv_spec(p) for p in range(pps)]
          + [pl.BlockSpec((None, None, 1, cols), lf_map(p)) for p in range(pps)],
        out_specs=pl.BlockSpec((None, rows, HEAD_DIM), seq_map),
        scratch_shapes=[pltpu.VMEM((rows, 1), F32), pltpu.VMEM((rows, 1), F32),
                        pltpu.VMEM((rows, HEAD_DIM), F32), pltpu.VMEM((1, cols), F32),
                        pltpu.VMEM((rows, cols), F32)],
    )
    return pl.pallas_call(
        functools.partial(_fox_sample_body, pps=pps, page=page),
        grid_spec=grid_spec,
        out_shape=jax.ShapeDtypeStruct((DB, rows, HEAD_DIM), BF16),
        compiler_params=_params("arbitrary", "arbitrary"),
        name="fox_sample",
    )(page_table, q, kn, vn, lfr, lfc, hg_rows,
      *([cache_k] * pps), *([cache_v] * pps), *([cache_lf2] * pps))


def _merge_body(attp_ref, atts_ref, u_ref, vm_ref, wcp_ref, bcp_ref, wcs_ref, bcs_ref, hg_ref, wo_ref,
                x_ref, g_ref, xo_ref, ho_ref, att_scr, mlp_scr, *, n_prompt_tiles):
    i = pl.program_id(0)
    tm = x_ref.shape[0]

    def gate_heads(wc_ref, bc_ref):
        chunk = wc_ref.shape[1]
        for h in range(N_HEADS):
            sl = slice(h * HEAD_DIM, (h + 1) * HEAD_DIM)
            for c in range(tm // chunk):
                rows = slice(c * chunk, (c + 1) * chunk)
                s = jnp.dot(wc_ref[h], vm_ref[rows, sl].astype(BF16), preferred_element_type=F32) + bc_ref[h]
                o = u_ref[rows, sl] * s
                mlp_scr[rows, sl] = (_rms(o) * hg_ref[:, sl]).astype(BF16)

    @pl.when(i < n_prompt_tiles)
    def _():
        att_scr[...] = attp_ref[...]
        gate_heads(wcp_ref, bcp_ref)

    @pl.when(i >= n_prompt_tiles)
    def _():
        att_scr[...] = atts_ref[...]
        gate_heads(wcs_ref, bcs_ref)

    y = jnp.dot(att_scr[...], wo_ref[:HEAD_W, :], preferred_element_type=F32)
    y = y + jnp.dot(mlp_scr[...], wo_ref[HEAD_W:, :], preferred_element_type=F32)
    xn = x_ref[...] + y
    xo_ref[...] = xn
    ho_ref[...] = (_rms(xn) * g_ref[...]).astype(ho_ref.dtype)


def _merge(att_p, att_s, z, wc_p, bc_p, wc_s, bc_s, hg_mlp, wo, x, g, *, h_dtype):
    T, D = x.shape
    tm = ROW_TILE
    n_prompt_tiles = att_p.shape[0] // tm
    assert att_s.shape[0] == tm and wc_s.shape[1] == tm
    chunk = wc_p.shape[1]
    row = lambda i: (i, 0)
    c2 = lambda i: (0, 0)
    c3 = lambda i: (0, 0, 0)
    return pl.pallas_call(
        functools.partial(_merge_body, n_prompt_tiles=n_prompt_tiles),
        grid=(T // tm,),
        in_specs=[
            pl.BlockSpec((tm, HEAD_W), lambda i: (jnp.minimum(i, n_prompt_tiles - 1), 0)),
            pl.BlockSpec((tm, HEAD_W), c2),
            pl.BlockSpec((tm, HEAD_W), lambda i: (i, 3)),
            pl.BlockSpec((tm, HEAD_W), lambda i: (i, 4)),
            pl.BlockSpec((N_HEADS, chunk, chunk), c3),
            pl.BlockSpec((N_HEADS, chunk, HEAD_DIM), c3),
            pl.BlockSpec((N_HEADS, tm, tm), c3),
            pl.BlockSpec((N_HEADS, tm, HEAD_DIM), c3),
            pl.BlockSpec((1, HEAD_W), c2),
            pl.BlockSpec((2 * HEAD_W, D), c2),
            pl.BlockSpec((tm, D), row),
            pl.BlockSpec((1, D), c2),
        ],
        out_specs=[pl.BlockSpec((tm, D), row), pl.BlockSpec((tm, D), row)],
        out_shape=[jax.ShapeDtypeStruct((T, D), F32), jax.ShapeDtypeStruct((T, D), h_dtype)],
        scratch_shapes=[pltpu.VMEM((tm, HEAD_W), BF16), pltpu.VMEM((tm, HEAD_W), BF16)],
        compiler_params=_params("arbitrary"),
        name="mlp_merge_out",
    )(att_p, att_s, z, z, wc_p, bc_p, wc_s, bc_s, hg_mlp, wo, x, g)


def _dense_body(h_ref, wg_ref, wu_ref, wd_ref, *rest, has_tail):
    if has_tail:
        wgt_ref, wut_ref, wdt_ref, x_ref, o_ref = rest
    else:
        x_ref, o_ref = rest
    f = pl.program_id(1)

    @pl.when(f == 0)
    def _():
        o_ref[...] = x_ref[...]

    n_chunks = h_ref.shape[0] // ROW_TILE

    def swiglu(wg, wu, wd):
        wgv, wuv, wdv = (ref[...].astype(BF16) for ref in (wg, wu, wd))
        hs = [h_ref[n * ROW_TILE:(n + 1) * ROW_TILE, :] for n in range(n_chunks)]
        ab = [(jnp.dot(h, wgv, preferred_element_type=F32), jnp.dot(h, wuv, preferred_element_type=F32)) for h in hs]
        gs = [(a * jax.nn.sigmoid(a) * b).astype(BF16) for a, b in ab]
        ds = [jnp.dot(g, wdv, preferred_element_type=F32) for g in gs]
        for n, d in enumerate(ds):
            o_ref[n * ROW_TILE:(n + 1) * ROW_TILE, :] += d

    swiglu(wg_ref, wu_ref, wd_ref)

    if has_tail:
        @pl.when(f == pl.num_programs(1) - 1)
        def _():
            swiglu(wgt_ref, wut_ref, wdt_ref)


def _dense_ffn(h, wg, wu, wd, x, *, tm, tf):
    T, D = x.shape
    d_ff = wg.shape[1]
    n_main = d_ff // tf
    tail = d_ff - n_main * tf
    assert tail % HEAD_DIM == 0 and (n_main * tf) % max(tail, 1) == 0
    in_specs = [
        pl.BlockSpec((tm, D), lambda i, f: (i, 0)),
        pl.BlockSpec((D, tf), lambda i, f: (0, f)),
        pl.BlockSpec((D, tf), lambda i, f: (0, f)),
        pl.BlockSpec((tf, D), lambda i, f: (f, 0)),
    ]
    args = [h, wg, wu, wd]
    if tail:
        t_idx = n_main * tf // tail
        in_specs += [pl.BlockSpec((D, tail), lambda i, f: (0, t_idx)),
                     pl.BlockSpec((D, tail), lambda i, f: (0, t_idx)),
                     pl.BlockSpec((tail, D), lambda i, f: (t_idx, 0))]
        args += [wg, wu, wd]
    in_specs.append(pl.BlockSpec((tm, D), lambda i, f: (i, 0)))
    args.append(x)
    return pl.pallas_call(
        functools.partial(_dense_body, has_tail=bool(tail)),
        grid=(T // tm, n_main),
        in_specs=in_specs,
        out_specs=pl.BlockSpec((tm, D), lambda i, f: (i, 0)),
        out_shape=jax.ShapeDtypeStruct((T, D), F32),
        compiler_params=_params("arbitrary", "arbitrary"),
        name="dense_ffn",
    )(*args)


def _router_body(h_ref, w_ref, eid_ref, gate_ref, cnt_ref, *, n_exp):
    tm = h_ref.shape[0]
    logits = jnp.dot(h_ref[...].astype(BF16), w_ref[...], preferred_element_type=F32)
    lane = lax.broadcasted_iota(I32, logits.shape, 1)
    big = logits.shape[1]
    l1 = jnp.where(lane < n_exp, logits, NEG_INF)
    m1 = jnp.max(l1, axis=1, keepdims=True)
    i1 = jnp.min(jnp.where(l1 == m1, lane, big), axis=1, keepdims=True)
    l2 = jnp.where(lane == i1, NEG_INF, l1)
    m2 = jnp.max(l2, axis=1, keepdims=True)
    i2 = jnp.min(jnp.where(l2 == m2, lane, big), axis=1, keepdims=True)
    e2 = jnp.exp(m2 - m1)
    p1 = 1.0 / (1.0 + e2)
    p2 = e2 / (1.0 + e2)
    gate_ref[...] = jnp.where(lane == 0, p1, jnp.where(lane == 1, p2, 0.0))

    @pl.when(pl.program_id(0) == 0)
    def _():
        cnt_ref[...] = jnp.zeros(cnt_ref.shape, F32)

    hit1 = (lane == i1).astype(F32)
    hit2 = (lane == i2).astype(F32)
    r = lax.broadcasted_iota(I32, (tm, tm), 0)
    c = lax.broadcasted_iota(I32, (tm, tm), 1)
    before = jnp.dot((c < r).astype(BF16), (hit1 + hit2).astype(BF16), preferred_element_type=F32)
    before = before + cnt_ref[...]
    rank1 = jnp.sum(hit1 * before, axis=1, keepdims=True).astype(I32)
    rank2 = jnp.sum(hit2 * before, axis=1, keepdims=True).astype(I32)
    cnt_ref[...] += jnp.sum(hit1 + hit2, axis=0, keepdims=True)
    eid_ref[...] = jnp.where(lane == 0, i1, jnp.where(lane == 1, i2,
                             jnp.where(lane == TOP_K, rank1, jnp.where(lane == TOP_K + 1, rank2, 0))))


def _router(h, w_pad, n_exp, *, tm):
    T, D = h.shape
    return pl.pallas_call(
        functools.partial(_router_body, n_exp=n_exp),
        grid=(T // tm,),
        in_specs=[pl.BlockSpec((tm, D), lambda i: (i, 0)),
                  pl.BlockSpec((D, HEAD_DIM), lambda i: (0, 0))],
        out_specs=[pl.BlockSpec((tm, HEAD_DIM), lambda i: (i, 0)),
                   pl.BlockSpec((tm, HEAD_DIM), lambda i: (i, 0)),
                   pl.BlockSpec((1, HEAD_DIM), lambda i: (0, 0))],
        out_shape=[jax.ShapeDtypeStruct((T, HEAD_DIM), I32), jax.ShapeDtypeStruct((T, HEAD_DIM), F32),
                   jax.ShapeDtypeStruct((1, HEAD_DIM), F32)],
        compiler_params=_params("arbitrary"),
        name="router_top2",
    )(h, w_pad)


def _row_copy(src_hbm, row, dst_vmem, r, sem):
    return pltpu.make_async_copy(src_hbm.at[pl.ds(row, 1)], dst_vmem.at[pl.ds(r, 1)], sem)


def _moe_body(ie_ref, ifirst_ref, inb_ref, cnt_ref, src_ref,
              h_hbm, wg_ref, wu_ref, wd_ref, y_hbm,
              xbuf, acc, stage, sem_in, sem_out, *, sub, unroll):
    del ie_ref
    w = pl.program_id(0)
    f = pl.program_id(1)
    n_f = pl.num_programs(1)
    n_items, n_used = cnt_ref[0], cnt_ref[1]
    n_total = y_hbm.shape[0] // sub

    def block_rows(r):
        return pl.ds(pl.multiple_of(r * sub, sub), sub)

    def out_copy(r_local, r_global):
        return pltpu.make_async_copy(acc.at[block_rows(r_local)], y_hbm.at[block_rows(r_global)], sem_out)

    def drain_out(n):
        def wait(r, carry):
            out_copy(r, r).wait()
            return carry

        lax.fori_loop(0, n, wait, 0)

    @pl.when((w == 0) & (f == 0))
    def _():
        acc[block_rows(0)] = jnp.zeros((sub, acc.shape[1]), F32)

        def fill(r, carry):
            out_copy(0, r).start()
            return carry

        def drain(r, carry):
            out_copy(0, r).wait()
            return carry

        lax.fori_loop(n_used, n_total, fill, 0)
        lax.fori_loop(n_used, n_total, drain, 0)

    @pl.when(w < n_items)
    def _():
        first = ifirst_ref[w]
        nb = inb_ref[w]

        @pl.when(f == 0)
        def _():
            def issue(blk, slot):
                base = (first + blk) * sub

                def body(rr, carry):
                    for u in range(unroll):
                        r = rr * unroll + u
                        _row_copy(h_hbm, src_ref[base + r], stage.at[slot], r, sem_in.at[slot]).start(priority=u % 2)
                    return carry

                lax.fori_loop(0, sub // unroll, body, 0)

            issue(0, 0)

            @pl.when(w > 0)
            def _():
                drain_out(inb_ref[w - 1])

            def per_block(blk, carry):
                slot = blk % 2

                @pl.when(blk + 1 < nb)
                def _():
                    issue(blk + 1, 1 - slot)

                pltpu.make_async_copy(h_hbm.at[pl.ds(0, sub)], stage.at[slot], sem_in.at[slot]).wait()
                xbuf[block_rows(blk)] = stage[slot].astype(BF16)
                acc[block_rows(blk)] = jnp.zeros((sub, acc.shape[1]), F32)
                return carry

            lax.fori_loop(0, nb, per_block, 0)

        def swiglu(blocks):
            xs = [xbuf[block_rows(r)] for r in blocks]
            wgv, wuv, wdv = (ref[...].astype(BF16) for ref in (wg_ref, wu_ref, wd_ref))
            ab = [(jnp.dot(x, wgv, preferred_element_type=F32),
                   jnp.dot(x, wuv, preferred_element_type=F32)) for x in xs]
            hs = [(a * jax.nn.sigmoid(a) * b).astype(BF16) for a, b in ab]
            ds = [jnp.dot(hh, wdv, preferred_element_type=F32) for hh in hs]
            for r, d in zip(blocks, ds):
                acc[block_rows(r)] += d

        def pair(rp, carry):
            swiglu([2 * rp, 2 * rp + 1])
            return carry

        lax.fori_loop(0, nb // 2, pair, 0)

        @pl.when(nb % 2 == 1)
        def _():
            swiglu([nb - 1])

        @pl.when(f == n_f - 1)
        def _():
            def start(r, carry):
                out_copy(r, first + r).start()
                return carry

            lax.fori_loop(0, nb, start, 0)

            @pl.when(w == n_items - 1)
            def _():
                drain_out(nb)


def _moe_ffn(tables, h, wg, wu, wd, *, tf, max_blocks):
    item_expert, item_first, item_nb, counts, src, n_rows = tables
    T, D = h.shape
    sub = ROW_TILE
    n_f = wg.shape[2] // tf
    last = n_f - 1

    def live(w, cnt):
        return jnp.minimum(w, cnt[0] - 1)

    def fidx(w, f, cnt):
        return jnp.where(w < cnt[0], f, last)

    grid_spec = pltpu.PrefetchScalarGridSpec(
        num_scalar_prefetch=5,
        grid=(item_expert.shape[0], n_f),
        in_specs=[
            pl.BlockSpec(memory_space=pl.ANY),
            pl.BlockSpec((None, D, tf), lambda w, f, ie, i1, i2, cnt, s: (ie[live(w, cnt)], 0, fidx(w, f, cnt))),
            pl.BlockSpec((None, D, tf), lambda w, f, ie, i1, i2, cnt, s: (ie[live(w, cnt)], 0, fidx(w, f, cnt))),
            pl.BlockSpec((None, tf, D), lambda w, f, ie, i1, i2, cnt, s: (ie[live(w, cnt)], fidx(w, f, cnt), 0)),
        ],
        out_specs=pl.BlockSpec(memory_space=pl.ANY),
        scratch_shapes=[
            pltpu.VMEM((max_blocks * sub, D), BF16),
            pltpu.VMEM((max_blocks * sub, D), F32),
            pltpu.VMEM((2, sub, D), F32),
            pltpu.SemaphoreType.DMA((2,)), pltpu.SemaphoreType.DMA(()),
        ],
    )
    return pl.pallas_call(
        functools.partial(_moe_body, sub=sub, unroll=8),
        grid_spec=grid_spec,
        out_shape=jax.ShapeDtypeStruct((n_rows, D), F32),
        compiler_params=_params("arbitrary", "arbitrary"),
        name="moe_ffn",
    )(item_expert, item_first, item_nb, counts, src, h, wg, wu, wd)


def _combine_body(pos_ref, x_ref, gate_ref, y_hbm, *rest, tm, unroll, split_tile):
    o_refs, (buf, sem) = rest[:-2], rest[-2:]
    i = pl.program_id(0)
    slot = i % 2

    def gather(tile, to_slot):
        base = tile * tm * TOP_K

        def issue(rr, carry):
            for u in range(unroll):
                r = rr * unroll + u
                for k in range(TOP_K):
                    _row_copy(y_hbm, pos_ref[base + r * TOP_K + k], buf.at[to_slot, k], r,
                              sem.at[to_slot, k]).start(priority=k)
            return carry

        lax.fori_loop(0, tm // unroll, issue, 0)

    @pl.when(i == 0)
    def _():
        gather(0, 0)

    @pl.when(i + 1 < pl.num_programs(0))
    def _():
        gather(i + 1, 1 - slot)

    out = x_ref[...]
    for k in range(TOP_K):
        pltpu.make_async_copy(y_hbm.at[pl.ds(0, tm)], buf.at[slot, k], sem.at[slot, k]).wait()
        out = out + gate_ref[:, k:k + 1] * buf[slot, k]
    if split_tile is None:
        o_refs[0][...] = out
    else:
        @pl.when(pl.program_id(0) < split_tile)
        def _():
            o_refs[0][...] = out

        @pl.when(pl.program_id(0) >= split_tile)
        def _():
            o_refs[1][...] = out


def _combine(pos, x, gate, y, *, tm, split_rows=None):
    T, D = x.shape
    if split_rows is None:
        split_tile = None
        out_specs = [pl.BlockSpec((tm, D), lambda i, p: (i, 0))]
        out_shape = [jax.ShapeDtypeStruct((T, D), F32)]
    else:
        split_tile = split_rows // tm
        out_specs = [pl.BlockSpec((tm, D), lambda i, p: (jnp.minimum(i, split_tile - 1), 0)),
                     pl.BlockSpec((tm, D), lambda i, p: (jnp.maximum(i - split_tile, 0), 0))]
        out_shape = [jax.ShapeDtypeStruct((split_rows, D), F32), jax.ShapeDtypeStruct((T - split_rows, D), F32)]
    grid_spec = pltpu.PrefetchScalarGridSpec(
        num_scalar_prefetch=1,
        grid=(T // tm,),
        in_specs=[pl.BlockSpec((tm, D), lambda i, p: (i, 0)),
                  pl.BlockSpec((tm, HEAD_DIM), lambda i, p: (i, 0)),
                  pl.BlockSpec(memory_space=pl.ANY)],
        out_specs=out_specs,
        scratch_shapes=[pltpu.VMEM((2, TOP_K, tm, D), F32), pltpu.SemaphoreType.DMA((2, TOP_K))],
    )
    return pl.pallas_call(
        functools.partial(_combine_body, tm=tm, unroll=4, split_tile=split_tile),
        grid_spec=grid_spec,
        out_shape=out_shape,
        compiler_params=_params("arbitrary"),
        name="combine_rows",
    )(pos, x, gate, y)


def _route_tables(eid, rank, counts, sub, max_blocks):
    flat = eid.reshape(-1)
    rank = rank.reshape(-1)
    n_assign = flat.shape[0]
    n_exp = counts.shape[0]
    nsub = (counts + sub - 1) // sub
    sub_end = jnp.cumsum(nsub)
    sub_start = sub_end - nsub
    pos = (sub_start[flat] * sub + rank).astype(I32)
    n_rows = (n_assign // sub + n_exp) * sub
    src = jnp.zeros((n_rows,), I32).at[pos].set(jnp.arange(n_assign, dtype=I32) // TOP_K)
    items = (nsub + max_blocks - 1) // max_blocks
    item_end = jnp.cumsum(items)
    item_start = item_end - items
    n_items_max = n_exp + (n_rows // sub) // max_blocks
    w = jnp.arange(n_items_max, dtype=I32)
    item_expert = jnp.minimum(jnp.sum((item_end[None, :] <= w[:, None]).astype(I32), axis=1), n_exp - 1)
    k = w - item_start[item_expert]
    item_first = sub_start[item_expert] + k * max_blocks
    item_nb = jnp.clip(nsub[item_expert] - k * max_blocks, 0, max_blocks)
    counts2 = jnp.stack([item_end[-1], sub_end[-1]]).astype(I32)
    return pos, (item_expert.astype(I32), item_first.astype(I32), item_nb.astype(I32), counts2, src, n_rows)


def _row_tile_multiple(n, cap):
    best = ROW_TILE
    for m in range(ROW_TILE, cap + 1, ROW_TILE):
        if n % m == 0:
            best = m
    return best


def kernel(x_prompt, x_sample, cache_k, cache_v, cache_logf, page_table, norm_mix_g, w_in, b_f,
           q_norm_g, k_norm_g, sgu_norm_g, w_spatial, b_spatial, head_norm_g, w_out, norm_ffn_g,
           dense_w_gate, dense_w_up, dense_w_down, router_w, moe_w_gate, moe_w_up, moe_w_down):
    B, S, D = x_prompt.shape
    DB, DS, _ = x_sample.shape
    depth = w_in.shape[0]
    n_phys, page = cache_k.shape[1], cache_k.shape[2]
    chunk = w_spatial.shape[-1]
    n_exp = router_w.shape[-1]
    assert cache_k.shape[3] == N_HEADS and cache_k.shape[4] == HEAD_DIM
    assert w_in.shape[2] == N_SECTIONS * HEAD_W + N_HEADS and D == 2 * HEAD_W
    Tp, Ts = B * S, DB * DS
    T = Tp + Ts
    assert Ts == ROW_TILE and S % ROW_TILE == 0 and ROW_TILE % chunk == 0
    rows = N_HEADS * DS
    scale = HEAD_DIM ** -0.5
    f_lo = 3 * HEAD_W

    x = jnp.concatenate([x_prompt.reshape(Tp, D), x_sample.reshape(Ts, D)], axis=0)
    cache_lf2 = cache_logf.reshape(depth, n_phys, 1, page * N_HEADS)
    ones = jnp.ones((HEAD_W,), F32)

    outs = {name: [] for name in ("kp", "vp", "fp", "ks", "vs", "fs", "ms")}
    for l in range(depth):
        w_tail = w_in[l][:, f_lo + N_HEADS:].astype(BF16)
        wf = jnp.pad(w_in[l][:, f_lo:f_lo + N_HEADS], ((0, 0), (0, HEAD_DIM - N_HEADS))).astype(BF16)
        bf = jnp.pad(b_f[l], (0, HEAD_DIM - N_HEADS)).reshape(1, HEAD_DIM)
        g_mix = norm_mix_g[l].reshape(1, D)
        gains = jnp.stack([(q_norm_g[l] * scale).reshape(-1), k_norm_g[l].reshape(-1), ones, ones,
                           sgu_norm_g[l].reshape(-1)]).reshape(N_SECTIONS, 1, HEAD_W)
        hg_att = head_norm_g[l][:N_HEADS]
        hg_mlp = head_norm_g[l][N_HEADS:].reshape(1, HEAD_W)
        wo = w_out[l].astype(BF16)
        g_ffn = norm_ffn_g[l].reshape(1, D)
        wm = jnp.tril(w_spatial[l])
        wc_p = wm.astype(BF16)
        bc_p = jnp.broadcast_to(b_spatial[l][:, :, None], (N_HEADS, chunk, HEAD_DIM))
        tok = jnp.arange(Ts, dtype=I32)
        pick = (tok[:, None] % DS == jnp.arange(DS, dtype=I32)[None, :]).astype(F32)
        same_seq = (tok[:, None] // DS == tok[None, :] // DS).astype(F32)
        wc_s = (jnp.einsum("ri,hij,cj->hrc", pick, wm[:, :DS, :DS], pick, precision=lax.Precision.HIGHEST)
                * same_seq).astype(BF16)
        bc_s = jnp.broadcast_to(jnp.tile(b_spatial[l][:, :DS], (1, DB))[:, :, None], (N_HEADS, Ts, HEAD_DIM))

        z, lf, c = _project(x, g_mix, w_in, l, w_tail, wf, bf, gains, tm=_row_tile_multiple(T, 3 * ROW_TILE), seq=S)

        att_p = _fox_prompt(z, c, hg_att.reshape(N_HEADS, 1, HEAD_DIM), batch=B, seq=S, tq=ROW_TILE, hp=4)

        zs = z[Tp:]
        q_rows = zs[:, :HEAD_W].astype(BF16).reshape(DB, DS, N_HEADS, HEAD_DIM).transpose(0, 2, 1, 3).reshape(DB, rows, HEAD_DIM)
        kn = zs[:, HEAD_W:2 * HEAD_W].reshape(DB, rows, HEAD_DIM)
        vn = zs[:, 2 * HEAD_W:3 * HEAD_W].reshape(DB, rows, HEAD_DIM)
        lfs8 = lf[Tp:, :N_HEADS]
        att_rows = _fox_sample(l, page_table, q_rows, kn, vn, lfs8.reshape(DB, 1, rows), lfs8.reshape(DB, rows, 1),
                               jnp.repeat(hg_att, DS, axis=0), cache_k, cache_v, cache_lf2, pps=8)
        att_s = att_rows.reshape(DB, N_HEADS, DS, HEAD_DIM).transpose(0, 2, 1, 3).reshape(Ts, HEAD_W)

        moe_layer = l % 2 == 1
        x, h = _merge(att_p, att_s, z, wc_p, bc_p, wc_s, bc_s, hg_mlp, wo, x, g_ffn,
                      h_dtype=F32 if moe_layer else BF16)

        i = l // 2
        if not moe_layer:
            x = _dense_ffn(h, dense_w_gate[i], dense_w_up[i], dense_w_down[i], x,
                           tm=_row_tile_multiple(T, 3 * ROW_TILE), tf=256)
        else:
            rw = jnp.pad(router_w[i], ((0, 0), (0, HEAD_DIM - n_exp))).astype(BF16)
            eid, gate, counts = _router(h, rw, n_exp, tm=ROW_TILE)
            pos, tables = _route_tables(eid[:, :TOP_K], eid[:, TOP_K:2 * TOP_K], counts[0, :n_exp].astype(I32),
                                        ROW_TILE, 10)
            y_sorted = _moe_ffn(tables, h, moe_w_gate[i], moe_w_up[i], moe_w_down[i], tf=256, max_blocks=10)
            if l == depth - 1:
                x_split = _combine(pos, x, gate, y_sorted, tm=ROW_TILE, split_rows=Tp)
            else:
                x = _combine(pos, x, gate, y_sorted, tm=ROW_TILE)[0]

        outs["kp"].append(z[:Tp, HEAD_W:2 * HEAD_W].reshape(B, S, N_HEADS, HEAD_DIM))
        outs["vp"].append(z[:Tp, 2 * HEAD_W:3 * HEAD_W].reshape(B, S, N_HEADS, HEAD_DIM))
        outs["fp"].append(lf[:Tp, :N_HEADS].reshape(B, S, N_HEADS))
        outs["ks"].append(kn.reshape(DB, DS, N_HEADS, HEAD_DIM))
        outs["vs"].append(vn.reshape(DB, DS, N_HEADS, HEAD_DIM))
        outs["fs"].append(lfs8.reshape(DB, DS, N_HEADS))
        outs["ms"].append(zs[:, 4 * HEAD_W:].reshape(DB, DS, N_HEADS, HEAD_DIM))

    if depth % 2 == 0:
        y_p, y_s = x_split
    else:
        y_p, y_s = x[:Tp], x[Tp:]
    return (y_p.reshape(B, S, D), y_s.reshape(DB, DS, D),
            jnp.stack(outs["kp"]), jnp.stack(outs["vp"]), jnp.stack(outs["fp"]),
            jnp.stack(outs["ks"]), jnp.stack(outs["vs"]), jnp.stack(outs["fs"]), jnp.stack(outs["ms"]))
```

```python
import functools

import jax
import jax.numpy as jnp
from jax import lax
from jax.experimental import pallas as pl
from jax.experimental.pallas import tpu as pltpu

F32 = jnp.float32
BF16 = jnp.bfloat16
I32 = jnp.int32

HEAD_DIM = 128
N_HEADS = 8
HEAD_W = N_HEADS * HEAD_DIM
N_SECTIONS = 5
EPS = 1e-6
NEG_INF = float("-inf")
TOP_K = 2
ROW_TILE = 256
MOE_SUB = ROW_TILE // 2
MOE_ITEM_BLOCKS = 20
VMEM_LIMIT = 56 * 1024 * 1024


def _params(*sem):
    return pltpu.CompilerParams(dimension_semantics=sem, vmem_limit_bytes=VMEM_LIMIT)


def _nt_dot(a, b):
    return lax.dot_general(a, b, (((1,), (1,)), ((), ())), preferred_element_type=F32)


def _rms(x):
    return x * lax.rsqrt(jnp.mean(x * x, axis=-1, keepdims=True) + EPS)


def _split3(x):
    x1 = x.astype(BF16).astype(F32)
    r1 = x - x1
    x2 = r1.astype(BF16).astype(F32)
    x3 = (r1 - x2).astype(BF16).astype(F32)
    return x1, x2, x3


def _pick_lane(x, lane_idx):
    lane = lax.broadcasted_iota(I32, x.shape, 1)
    return jnp.sum(jnp.where(lane == lane_idx, x, 0.0), axis=1, keepdims=True)


def _proj_body(x_ref, g_ref, w_ref, wf_ref, bf_ref, gain_ref, z_ref, lf_ref, c_ref, h_scr, carry_scr,
               *, chunk, chunks_per_seq):
    i = pl.program_id(0)
    j = pl.program_id(1)
    n_chunks = x_ref.shape[0] // chunk

    def section(jj):
        first = jj == 0
        normed = jj in (0, 1, 4)
        w = w_ref[...]
        if first:
            r = lax.broadcasted_iota(I32, (chunk, chunk), 0)
            c = lax.broadcasted_iota(I32, (chunk, chunk), 1)
            lower = (c <= r).astype(BF16)

            @pl.when(i == 0)
            def _():
                carry_scr[...] = jnp.zeros(carry_scr.shape, F32)

            carry = carry_scr[...]
        for n in range(n_chunks):
            rows = slice(n * chunk, (n + 1) * chunk)
            if first:
                hb = (_rms(x_ref[rows, :]) * g_ref[...]).astype(BF16)
                h_scr[rows, :] = hb
                f = jnp.dot(hb, wf_ref[...], preferred_element_type=F32) + bf_ref[...]
                lf = jnp.minimum(f, 0.0) - jnp.log1p(jnp.exp(-jnp.abs(f)))
                lf_ref[rows, :] = lf
                first_of_seq = (i * n_chunks + n) % chunks_per_seq == 0
                cs = jnp.where(first_of_seq, 0.0, carry)
                for part in _split3(lf):
                    cs = cs + jnp.dot(lower, part.astype(BF16), preferred_element_type=F32)
                c_ref[rows, :] = cs
                carry = cs[chunk - 1:chunk, :]
            else:
                hb = h_scr[rows, :]
            z = jnp.dot(hb, w, preferred_element_type=F32)
            if normed:
                for h in range(N_HEADS):
                    sl = slice(h * HEAD_DIM, (h + 1) * HEAD_DIM)
                    z_ref[rows, sl] = _rms(z[:, sl]) * gain_ref[:, sl]
            else:
                z_ref[rows, :] = z
        if first:
            carry_scr[...] = carry

    for jj in range(N_SECTIONS):
        pl.when(j == jj)(functools.partial(section, jj))


def _project(x, g, w_main, wf, bf, gains, *, tm, seq):
    T, D = x.shape
    row = lambda i, j: (i, 0)
    const = lambda i, j: (0, 0)
    return pl.pallas_call(
        functools.partial(_proj_body, chunk=ROW_TILE, chunks_per_seq=seq // ROW_TILE),
        grid=(T // tm, N_SECTIONS),
        in_specs=[
            pl.BlockSpec((tm, D), row),
            pl.BlockSpec((1, D), const),
            pl.BlockSpec((D, HEAD_W), lambda i, j: (0, j)),
            pl.BlockSpec((D, HEAD_DIM), const),
            pl.BlockSpec((1, HEAD_DIM), const),
            pl.BlockSpec((None, 1, HEAD_W), lambda i, j: (j, 0, 0)),
        ],
        out_specs=[pl.BlockSpec((tm, HEAD_W), lambda i, j: (i, j)),
                   pl.BlockSpec((tm, HEAD_DIM), row),
                   pl.BlockSpec((tm, HEAD_DIM), row)],
        out_shape=[
            jax.ShapeDtypeStruct((T, N_SECTIONS * HEAD_W), F32),
            jax.ShapeDtypeStruct((T, HEAD_DIM), F32),
            jax.ShapeDtypeStruct((T, HEAD_DIM), F32),
        ],
        scratch_shapes=[pltpu.VMEM((tm, D), BF16), pltpu.VMEM((1, HEAD_DIM), F32)],
        compiler_params=_params("arbitrary", "arbitrary"),
        name="in_proj",
    )(x, g, w_main, wf, bf, gains)


def _bias_lanes(c_col, key_side):
    lane = lax.broadcasted_iota(I32, (1, HEAD_DIM), 1)
    c1, c2, c3 = _split3(c_col)
    if key_side:
        terms = (1.0, 1.0, 1.0, -c1, -c2, -c3)
    else:
        terms = (c1, c2, c3, 1.0, 1.0, 1.0)
    out = jnp.zeros((c_col.shape[0], HEAD_DIM), F32)
    for idx, t in enumerate(terms):
        out = jnp.where(lane == idx, t, out)
    return out.astype(BF16)


def _fox_prompt_body(q_ref, k_ref, v_ref, cq_ref, ck_ref, hg_ref, o_ref, ka, vt, m_scr, acc_scr, *, tq, hp):
    hgrp = pl.program_id(1)
    qi = pl.program_id(2)
    n_kv = vt.shape[1]
    ext = vt.shape[2] - HEAD_DIM

    @pl.when(qi == 0)
    def _():
        row = lax.broadcasted_iota(I32, (ext, tq), 0)
        ones_ext = jnp.where(row == 0, 1.0, 0.0).astype(BF16)
        for hh in range(hp):
            sl = slice(hh * HEAD_DIM, (hh + 1) * HEAD_DIM)
            ka[hh, :, :HEAD_DIM] = k_ref[:, sl].astype(BF16)
            ka[hh, :, HEAD_DIM:] = _bias_lanes(_pick_lane(ck_ref[...], hgrp * hp + hh), True)
            for n in range(n_kv):
                vt[hh, n, :HEAD_DIM, :] = v_ref[n * tq:(n + 1) * tq, sl].T.astype(BF16)
                vt[hh, n, HEAD_DIM:, :] = ones_ext

    qa = []
    for hh in range(hp):
        sl = slice(hh * HEAD_DIM, (hh + 1) * HEAD_DIM)
        bias = _bias_lanes(_pick_lane(cq_ref[...], hgrp * hp + hh), False)
        qa.append(jnp.concatenate([q_ref[:, sl].astype(BF16), bias], axis=1))
        m_scr[hh] = jnp.full((1, tq), NEG_INF, F32)
        acc_scr[hh] = jnp.zeros(acc_scr.shape[1:], F32)

    def block(kj, diagonal):
        start = pl.multiple_of(kj * tq, tq)
        st = [_nt_dot(ka[hh, pl.ds(start, tq), :], qa[hh]) for hh in range(hp)]
        vts = [vt[hh, kj] for hh in range(hp)]
        ms = [m_scr[hh] for hh in range(hp)]
        accs = [acc_scr[hh] for hh in range(hp)]
        if diagonal:
            key = lax.broadcasted_iota(I32, (tq, tq), 0)
            qry = lax.broadcasted_iota(I32, (tq, tq), 1)
            st = [jnp.where(key <= qry, s, NEG_INF) for s in st]
        m_new = [jnp.maximum(m, jnp.max(s, axis=0, keepdims=True)) for m, s in zip(ms, st)]
        pt = [jnp.exp(s - mn).astype(BF16) for s, mn in zip(st, m_new)]
        pv = [jnp.dot(v, p, preferred_element_type=F32) for v, p in zip(vts, pt)]
        new_acc = [jnp.exp(m - mn) * a + d for m, mn, a, d in zip(ms, m_new, accs, pv)]
        for hh in range(hp):
            acc_scr[hh] = new_acc[hh]
            m_scr[hh] = m_new[hh]

    def body(kj, carry):
        block(kj, False)
        return carry

    lax.fori_loop(0, qi, body, 0)
    block(qi, True)
    for hh in range(hp):
        acc = acc_scr[hh]
        ot = acc[:HEAD_DIM, :] / acc[HEAD_DIM:HEAD_DIM + 1, :]
        ot = ot * lax.rsqrt(jnp.mean(ot * ot, axis=0, keepdims=True) + EPS)
        o_ref[:, hh * HEAD_DIM:(hh + 1) * HEAD_DIM] = (ot.T * hg_ref[hh]).astype(o_ref.dtype)


def _fox_prompt(z, c, hg, *, batch, seq, tq, hp):
    nq = seq // tq
    ngrp = N_HEADS // hp
    w = hp * HEAD_DIM
    ext = 16
    return pl.pallas_call(
        functools.partial(_fox_prompt_body, tq=tq, hp=hp),
        grid=(batch, ngrp, nq),
        in_specs=[
            pl.BlockSpec((tq, w), lambda b, h, i: (b * nq + i, h)),
            pl.BlockSpec((seq, w), lambda b, h, i: (b, ngrp + h)),
            pl.BlockSpec((seq, w), lambda b, h, i: (b, 2 * ngrp + h)),
            pl.BlockSpec((tq, HEAD_DIM), lambda b, h, i: (b * nq + i, 0)),
            pl.BlockSpec((seq, HEAD_DIM), lambda b, h, i: (b, 0)),
            pl.BlockSpec((hp, 1, HEAD_DIM), lambda b, h, i: (h, 0, 0)),
        ],
        out_specs=pl.BlockSpec((tq, w), lambda b, h, i: (b * nq + i, h)),
        out_shape=jax.ShapeDtypeStruct((batch * seq, HEAD_W), BF16),
        scratch_shapes=[pltpu.VMEM((hp, seq, 2 * HEAD_DIM), BF16),
                        pltpu.VMEM((hp, nq, HEAD_DIM + ext, tq), BF16),
                        pltpu.VMEM((hp, 1, tq), F32), pltpu.VMEM((hp, HEAD_DIM + ext, tq), F32)],
        compiler_params=_params("arbitrary", "arbitrary", "arbitrary"),
        name="fox_prompt",
    )(z, z, z, c, c, hg)


def _fox_sample_body(pt_ref, q_ref, kn_ref, vn_ref, lfr_ref, lfc_ref, hg_ref, ck_hbm, cv_hbm, *rest,
                     layer, pps, page, ring):
    lp = rest[:pps]
    o_ref, m_scr, l_scr, acc_scr, carry_scr, negq_scr, kbuf, vbuf, sem = rest[pps:]
    j = pl.program_id(1)
    n_j = pl.num_programs(1)
    rows = q_ref.shape[0]
    dsq = rows // N_HEADS
    cols = page * N_HEADS
    n_pages = n_j * pps

    g = pl.program_id(0) * n_j + j
    n_g = pl.num_programs(0) * n_j

    def fetch(gg):
        slot = gg % ring
        seq = gg // n_j
        first = n_pages - 1 - (gg % n_j) * pps
        for p in range(pps):
            pg = pt_ref[seq, first - p]
            pltpu.make_async_copy(ck_hbm.at[layer, pg], kbuf.at[slot, p], sem.at[slot]).start()
            pltpu.make_async_copy(cv_hbm.at[layer, pg], vbuf.at[slot, p], sem.at[slot]).start()

    @pl.when(g == 0)
    def _():
        for ahead in range(ring - 1):
            @pl.when(ahead < n_g)
            def _():
                fetch(ahead)

    @pl.when(g + ring - 1 < n_g)
    def _():
        fetch(g + ring - 1)

    slot = g % ring
    pltpu.make_async_copy(ck_hbm.at[layer, pl.ds(0, pps)], kbuf.at[slot], sem.at[slot]).wait()
    pltpu.make_async_copy(cv_hbm.at[layer, pl.ds(0, pps)], vbuf.at[slot], sem.at[slot]).wait()

    rn = lax.broadcasted_iota(I32, (rows, rows), 0)
    cn = lax.broadcasted_iota(I32, (rows, rows), 1)
    keep_new = ((rn // dsq) == (cn % N_HEADS)) & ((cn // N_HEADS) <= (rn % dsq))
    cq_col = jnp.sum(jnp.where(keep_new, lfr_ref[...], 0.0), axis=1, keepdims=True)

    @pl.when(j == 0)
    def _():
        m_scr[...] = jnp.full(m_scr.shape, NEG_INF, F32)
        l_scr[...] = jnp.zeros(l_scr.shape, F32)
        acc_scr[...] = jnp.zeros(acc_scr.shape, F32)
        carry_scr[...] = jnp.zeros(carry_scr.shape, F32)
        rp = lax.broadcasted_iota(I32, (rows, cols), 0)
        cp = lax.broadcasted_iota(I32, (rows, cols), 1)
        negq_scr[...] = jnp.where((rp // dsq) == (cp % N_HEADS), cq_col, NEG_INF)

    def update(scores, values):
        m, l, acc = m_scr[...], l_scr[...], acc_scr[...]
        mx = functools.reduce(jnp.maximum, scores)
        m_new = jnp.maximum(m, jnp.max(mx, axis=1, keepdims=True))
        alpha = jnp.exp(m - m_new)
        ps = [jnp.exp(s - m_new) for s in scores]
        pv = [jnp.dot(p.astype(BF16), v, preferred_element_type=F32) for p, v in zip(ps, values)]
        l_scr[...] = alpha * l + jnp.sum(functools.reduce(jnp.add, ps), axis=1, keepdims=True)
        acc_scr[...] = alpha * acc + functools.reduce(jnp.add, pv)
        m_scr[...] = m_new

    q = q_ref[...]
    raw = [_nt_dot(q, kbuf[slot, p].reshape(cols, HEAD_DIM).astype(BF16)) for p in range(pps)]

    lpp = jnp.concatenate([lp[p][...] for p in range(pps)], axis=0)
    lane = lax.broadcasted_iota(I32, (pps, cols), 1)
    sub = lax.broadcasted_iota(I32, (pps, cols), 0)
    n_steps = (page - 1).bit_length()
    tot = lpp
    for t in range(n_steps):
        tot = tot + pltpu.roll(tot, N_HEADS << t, axis=1)
    suf = jnp.where(lane < cols - N_HEADS, pltpu.roll(lpp, cols - N_HEADS, axis=1), 0.0)
    for t in range(n_steps):
        sh = N_HEADS << t
        suf = suf + jnp.where(lane < cols - sh, pltpu.roll(suf, cols - sh, axis=1), 0.0)
    pre = jnp.where(sub >= 1, pltpu.roll(tot, 1, axis=0), 0.0)
    for t in range((pps - 1).bit_length()):
        sh = 1 << t
        pre = pre + jnp.where(sub >= sh, pltpu.roll(pre, sh, axis=0), 0.0)
    carry = carry_scr[...]
    bias = suf + pre + carry

    negq = negq_scr[...]
    v2 = [vbuf[slot, p].reshape(cols, HEAD_DIM).astype(BF16) for p in range(pps)]
    update([raw[p] + negq + bias[p:p + 1, :] for p in range(pps)], v2)
    carry_scr[...] = carry + jnp.sum(tot, axis=0, keepdims=True)

    @pl.when(j == pl.num_programs(1) - 1)
    def _():
        cn_row = jnp.sum(jnp.where(((rn % N_HEADS) == (cn % N_HEADS)) & ((rn // N_HEADS) <= (cn // N_HEADS)),
                                   lfc_ref[...], 0.0), axis=0, keepdims=True)
        s = jnp.where(keep_new, _nt_dot(q, kn_ref[...].astype(BF16)) + cq_col - cn_row, NEG_INF)
        update([s], [vn_ref[...].astype(BF16)])
        o = acc_scr[...] / l_scr[...]
        o_ref[...] = (_rms(o) * hg_ref[...]).astype(o_ref.dtype)


def _fox_sample(layer, page_table, q, kn, vn, lfr, lfc, hg_rows, cache_k, cache_v, cache_lf2, *, pps):
    DB, n_pages = page_table.shape
    rows = q.shape[1]
    page = cache_k.shape[2]
    cols = page * N_HEADS
    n_steps = n_pages // pps

    def seq_map(b, j, pt):
        return (b, 0, 0)

    def lf_map(p):
        def f(b, j, pt):
            return (layer, pt[b, n_pages - 1 - (j * pps + p)], 0, 0)
        return f

    ring = 3
    grid_spec = pltpu.PrefetchScalarGridSpec(
        num_scalar_prefetch=1,
        grid=(DB, n_steps),
        in_specs=[
            pl.BlockSpec((None, rows, HEAD_DIM), seq_map),
            pl.BlockSpec((None, rows, HEAD_DIM), seq_map),
            pl.BlockSpec((None, rows, HEAD_DIM), seq_map),
            pl.BlockSpec((None, 1, rows), seq_map),
            pl.BlockSpec((None, rows, 1), seq_map),
            pl.BlockSpec((rows, HEAD_DIM), lambda b, j, pt: (0, 0)),
            pl.BlockSpec(memory_space=pl.ANY),
            pl.BlockSpec(memory_space=pl.ANY),
        ] + [pl.BlockSpec((None, None, 1, cols), lf_map(p)) for p in range(pps)],
        out_specs=pl.BlockSpec((None, rows, HEAD_DIM), seq_map),
        scratch_shapes=[pltpu.VMEM((rows, 1), F32), pltpu.VMEM((rows, 1), F32),
                        pltpu.VMEM((rows, HEAD_DIM), F32), pltpu.VMEM((1, cols), F32),
                        pltpu.VMEM((rows, cols), F32),
                        pltpu.VMEM((ring, pps, page, N_HEADS, HEAD_DIM), F32),
                        pltpu.VMEM((ring, pps, page, N_HEADS, HEAD_DIM), F32),
                        pltpu.SemaphoreType.DMA((ring,))],
    )
    return pl.pallas_call(
        functools.partial(_fox_sample_body, layer=layer, pps=pps, page=page, ring=ring),
        grid_spec=grid_spec,
        out_shape=jax.ShapeDtypeStruct((DB, rows, HEAD_DIM), BF16),
        compiler_params=_params("arbitrary", "arbitrary"),
        name="fox_sample",
    )(page_table, q, kn, vn, lfr, lfc, hg_rows, cache_k, cache_v, *([cache_lf2] * pps))


def _merge_body(attp_ref, atts_ref, u_ref, vm_ref, wcp_ref, bcp_ref, wcs_ref, bcs_ref, hg_ref, wo_ref,
                x_ref, g_ref, xo_ref, ho_ref, att_scr, mlp_scr, *, n_prompt_tiles):
    i = pl.program_id(0)
    tm = x_ref.shape[0]

    def gate_heads(wc_ref, bc_ref):
        chunk = wc_ref.shape[1]
        for h in range(N_HEADS):
            sl = slice(h * HEAD_DIM, (h + 1) * HEAD_DIM)
            for c in range(tm // chunk):
                rows = slice(c * chunk, (c + 1) * chunk)
                s = jnp.dot(wc_ref[h], vm_ref[rows, sl].astype(BF16), preferred_element_type=F32) + bc_ref[h]
                o = u_ref[rows, sl] * s
                mlp_scr[rows, sl] = (_rms(o) * hg_ref[:, sl]).astype(BF16)

    @pl.when(i < n_prompt_tiles)
    def _():
        att_scr[...] = attp_ref[...]
        gate_heads(wcp_ref, bcp_ref)

    @pl.when(i >= n_prompt_tiles)
    def _():
        att_scr[...] = atts_ref[...]
        gate_heads(wcs_ref, bcs_ref)

    y = jnp.dot(att_scr[...], wo_ref[:HEAD_W, :], preferred_element_type=F32)
    y = y + jnp.dot(mlp_scr[...], wo_ref[HEAD_W:, :], preferred_element_type=F32)
    xn = x_ref[...] + y
    xo_ref[...] = xn
    ho_ref[...] = (_rms(xn) * g_ref[...]).astype(ho_ref.dtype)


def _merge(att_p, att_s, z, wc_p, bc_p, wc_s, bc_s, hg_mlp, wo, x, g, *, h_dtype):
    T, D = x.shape
    tm = ROW_TILE
    n_prompt_tiles = att_p.shape[0] // tm
    assert att_s.shape[0] == tm and wc_s.shape[1] == tm
    chunk = wc_p.shape[1]
    row = lambda i: (i, 0)
    c2 = lambda i: (0, 0)
    c3 = lambda i: (0, 0, 0)
    return pl.pallas_call(
        functools.partial(_merge_body, n_prompt_tiles=n_prompt_tiles),
        grid=(T // tm,),
        in_specs=[
            pl.BlockSpec((tm, HEAD_W), lambda i: (jnp.minimum(i, n_prompt_tiles - 1), 0)),
            pl.BlockSpec((tm, HEAD_W), c2),
            pl.BlockSpec((tm, HEAD_W), lambda i: (i, 3)),
            pl.BlockSpec((tm, HEAD_W), lambda i: (i, 4)),
            pl.BlockSpec((N_HEADS, chunk, chunk), c3),
            pl.BlockSpec((N_HEADS, chunk, HEAD_DIM), c3),
            pl.BlockSpec((N_HEADS, tm, tm), c3),
            pl.BlockSpec((N_HEADS, tm, HEAD_DIM), c3),
            pl.BlockSpec((1, HEAD_W), c2),
            pl.BlockSpec((2 * HEAD_W, D), c2),
            pl.BlockSpec((tm, D), row),
            pl.BlockSpec((1, D), c2),
        ],
        out_specs=[pl.BlockSpec((tm, D), row), pl.BlockSpec((tm, D), row)],
        out_shape=[jax.ShapeDtypeStruct((T, D), F32), jax.ShapeDtypeStruct((T, D), h_dtype)],
        scratch_shapes=[pltpu.VMEM((tm, HEAD_W), BF16), pltpu.VMEM((tm, HEAD_W), BF16)],
        compiler_params=_params("arbitrary"),
        name="mlp_merge_out",
    )(att_p, att_s, z, z, wc_p, bc_p, wc_s, bc_s, hg_mlp, wo, x, g)


def _dense_body(h_ref, wg_ref, wu_ref, wd_ref, *rest, has_tail):
    if has_tail:
        wgt_ref, wut_ref, wdt_ref, x_ref, o_ref = rest
    else:
        x_ref, o_ref = rest
    f = pl.program_id(1)

    @pl.when(f == 0)
    def _():
        o_ref[...] = x_ref[...]

    n_chunks = h_ref.shape[0] // ROW_TILE

    def swiglu(wg, wu, wd):
        wgv, wuv, wdv = (ref[...].astype(BF16) for ref in (wg, wu, wd))
        hs = [h_ref[n * ROW_TILE:(n + 1) * ROW_TILE, :] for n in range(n_chunks)]
        ab = [(jnp.dot(h, wgv, preferred_element_type=F32), jnp.dot(h, wuv, preferred_element_type=F32)) for h in hs]
        gs = [(a * jax.nn.sigmoid(a) * b).astype(BF16) for a, b in ab]
        ds = [jnp.dot(g, wdv, preferred_element_type=F32) for g in gs]
        for n, d in enumerate(ds):
            o_ref[n * ROW_TILE:(n + 1) * ROW_TILE, :] += d

    swiglu(wg_ref, wu_ref, wd_ref)

    if has_tail:
        @pl.when(f == pl.num_programs(1) - 1)
        def _():
            swiglu(wgt_ref, wut_ref, wdt_ref)


def _dense_ffn(h, wg, wu, wd, x, *, tm, tf):
    T, D = x.shape
    d_ff = wg.shape[1]
    n_main = d_ff // tf
    tail = d_ff - n_main * tf
    assert tail % HEAD_DIM == 0 and (n_main * tf) % max(tail, 1) == 0
    in_specs = [
        pl.BlockSpec((tm, D), lambda i, f: (i, 0)),
        pl.BlockSpec((D, tf), lambda i, f: (0, f)),
        pl.BlockSpec((D, tf), lambda i, f: (0, f)),
        pl.BlockSpec((tf, D), lambda i, f: (f, 0)),
    ]
    args = [h, wg, wu, wd]
    if tail:
        t_idx = n_main * tf // tail
        in_specs += [pl.BlockSpec((D, tail), lambda i, f: (0, t_idx)),
                     pl.BlockSpec((D, tail), lambda i, f: (0, t_idx)),
                     pl.BlockSpec((tail, D), lambda i, f: (t_idx, 0))]
        args += [wg, wu, wd]
    in_specs.append(pl.BlockSpec((tm, D), lambda i, f: (i, 0)))
    args.append(x)
    return pl.pallas_call(
        functools.partial(_dense_body, has_tail=bool(tail)),
        grid=(T // tm, n_main),
        in_specs=in_specs,
        out_specs=pl.BlockSpec((tm, D), lambda i, f: (i, 0)),
        out_shape=jax.ShapeDtypeStruct((T, D), F32),
        compiler_params=_params("arbitrary", "arbitrary"),
        name="dense_ffn",
    )(*args)


def _router_body(h_ref, w_ref, eid_ref, gate_ref, cnt_ref, *, n_exp):
    tm = h_ref.shape[0]
    logits = jnp.dot(h_ref[...].astype(BF16), w_ref[...], preferred_element_type=F32)
    lane = lax.broadcasted_iota(I32, logits.shape, 1)
    big = logits.shape[1]
    l1 = jnp.where(lane < n_exp, logits, NEG_INF)
    m1 = jnp.max(l1, axis=1, keepdims=True)
    i1 = jnp.min(jnp.where(l1 == m1, lane, big), axis=1, keepdims=True)
    l2 = jnp.where(lane == i1, NEG_INF, l1)
    m2 = jnp.max(l2, axis=1, keepdims=True)
    i2 = jnp.min(jnp.where(l2 == m2, lane, big), axis=1, keepdims=True)
    e2 = jnp.exp(m2 - m1)
    p1 = 1.0 / (1.0 + e2)
    p2 = e2 / (1.0 + e2)
    gate_ref[...] = jnp.where(lane == 0, p1, jnp.where(lane == 1, p2, 0.0))

    @pl.when(pl.program_id(0) == 0)
    def _():
        cnt_ref[...] = jnp.zeros(cnt_ref.shape, F32)

    hit1 = (lane == i1).astype(F32)
    hit2 = (lane == i2).astype(F32)
    r = lax.broadcasted_iota(I32, (tm, tm), 0)
    c = lax.broadcasted_iota(I32, (tm, tm), 1)
    before = jnp.dot((c < r).astype(BF16), (hit1 + hit2).astype(BF16), preferred_element_type=F32)
    before = before + cnt_ref[...]
    rank1 = jnp.sum(hit1 * before, axis=1, keepdims=True).astype(I32)
    rank2 = jnp.sum(hit2 * before, axis=1, keepdims=True).astype(I32)
    cnt_ref[...] += jnp.sum(hit1 + hit2, axis=0, keepdims=True)
    eid_ref[...] = jnp.where(lane == 0, i1, jnp.where(lane == 1, i2,
                             jnp.where(lane == TOP_K, rank1, jnp.where(lane == TOP_K + 1, rank2, 0))))


def _router(h, w_pad, n_exp, *, tm):
    T, D = h.shape
    return pl.pallas_call(
        functools.partial(_router_body, n_exp=n_exp),
        grid=(T // tm,),
        in_specs=[pl.BlockSpec((tm, D), lambda i: (i, 0)),
                  pl.BlockSpec((D, HEAD_DIM), lambda i: (0, 0))],
        out_specs=[pl.BlockSpec((tm, HEAD_DIM), lambda i: (i, 0)),
                   pl.BlockSpec((tm, HEAD_DIM), lambda i: (i, 0)),
                   pl.BlockSpec((1, HEAD_DIM), lambda i: (0, 0))],
        out_shape=[jax.ShapeDtypeStruct((T, HEAD_DIM), I32), jax.ShapeDtypeStruct((T, HEAD_DIM), F32),
                   jax.ShapeDtypeStruct((1, HEAD_DIM), F32)],
        compiler_params=_params("arbitrary"),
        name="router_top2",
    )(h, w_pad)


def _row_copy(src_hbm, row, dst_vmem, r, sem):
    return pltpu.make_async_copy(src_hbm.at[pl.ds(row, 1)], dst_vmem.at[pl.ds(r, 1)], sem)


def _moe_body(ie_ref, ifirst_ref, inb_ref, cnt_ref, src_ref,
              h_hbm, wg_ref, wu_ref, wd_ref, y_hbm,
              xbuf, acc, stage, sem_in, sem_out, *, sub, unroll):
    del ie_ref
    w = pl.program_id(0)
    f = pl.program_id(1)
    n_f = pl.num_programs(1)
    n_items, n_used = cnt_ref[0], cnt_ref[1]
    n_total = y_hbm.shape[0] // sub

    def block_rows(r):
        return pl.ds(pl.multiple_of(r * sub, sub), sub)

    def out_copy(r_local, r_global):
        return pltpu.make_async_copy(acc.at[block_rows(r_local)], y_hbm.at[block_rows(r_global)], sem_out)

    def drain_out(n):
        def wait(r, carry):
            out_copy(r, r).wait()
            return carry

        lax.fori_loop(0, n, wait, 0)

    @pl.when((w == 0) & (f == 0))
    def _():
        acc[block_rows(0)] = jnp.zeros((sub, acc.shape[1]), F32)

        def fill(r, carry):
            out_copy(0, r).start()
            return carry

        def drain(r, carry):
            out_copy(0, r).wait()
            return carry

        lax.fori_loop(n_used, n_total, fill, 0)
        lax.fori_loop(n_used, n_total, drain, 0)

    @pl.when(w < n_items)
    def _():
        first = ifirst_ref[w]
        nb = inb_ref[w]

        @pl.when(f == 0)
        def _():
            def issue(blk, slot):
                base = (first + blk) * sub

                def body(rr, carry):
                    for u in range(unroll):
                        r = rr * unroll + u
                        _row_copy(h_hbm, src_ref[base + r], stage.at[slot], r, sem_in.at[slot]).start(priority=u % 2)
                    return carry

                lax.fori_loop(0, sub // unroll, body, 0)

            issue(0, 0)

            @pl.when(w > 0)
            def _():
                drain_out(inb_ref[w - 1])

            def per_block(blk, carry):
                slot = blk % 2

                @pl.when(blk + 1 < nb)
                def _():
                    issue(blk + 1, 1 - slot)

                pltpu.make_async_copy(h_hbm.at[pl.ds(0, sub)], stage.at[slot], sem_in.at[slot]).wait()
                xbuf[block_rows(blk)] = stage[slot].astype(BF16)
                acc[block_rows(blk)] = jnp.zeros((sub, acc.shape[1]), F32)
                return carry

            lax.fori_loop(0, nb, per_block, 0)

        def swiglu(chunks):
            spans = [pl.ds(pl.multiple_of(start, sub), size) for start, size in chunks]
            xs = [xbuf[s] for s in spans]
            wgv, wuv, wdv = (ref[...].astype(BF16) for ref in (wg_ref, wu_ref, wd_ref))
            ab = [(jnp.dot(x, wgv, preferred_element_type=F32),
                   jnp.dot(x, wuv, preferred_element_type=F32)) for x in xs]
            hs = [(a * jax.nn.sigmoid(a) * b).astype(BF16) for a, b in ab]
            ds = [jnp.dot(hh, wdv, preferred_element_type=F32) for hh in hs]
            for s, d in zip(spans, ds):
                acc[s] += d

        wide = 2 * sub
        n_wide = nb // 2

        def pair(rp, carry):
            swiglu([(2 * rp * wide, wide), ((2 * rp + 1) * wide, wide)])
            return carry

        lax.fori_loop(0, n_wide // 2, pair, 0)

        @pl.when(n_wide % 2 == 1)
        def _():
            swiglu([((n_wide - 1) * wide, wide)])

        @pl.when(nb % 2 == 1)
        def _():
            swiglu([((nb - 1) * sub, sub)])

        @pl.when(f == n_f - 1)
        def _():
            def start(r, carry):
                out_copy(r, first + r).start()
                return carry

            lax.fori_loop(0, nb, start, 0)

            @pl.when(w == n_items - 1)
            def _():
                drain_out(nb)


def _moe_ffn(tables, h, wg, wu, wd, *, tf, max_blocks):
    item_expert, item_first, item_nb, counts, src, n_rows = tables
    T, D = h.shape
    sub = MOE_SUB
    n_f = wg.shape[2] // tf
    last = n_f - 1

    def live(w, cnt):
        return jnp.minimum(w, cnt[0] - 1)

    def fidx(w, f, cnt):
        return jnp.where(w < cnt[0], f, last)

    grid_spec = pltpu.PrefetchScalarGridSpec(
        num_scalar_prefetch=5,
        grid=(item_expert.shape[0], n_f),
        in_specs=[
            pl.BlockSpec(memory_space=pl.ANY),
            pl.BlockSpec((None, D, tf), lambda w, f, ie, i1, i2, cnt, s: (ie[live(w, cnt)], 0, fidx(w, f, cnt))),
            pl.BlockSpec((None, D, tf), lambda w, f, ie, i1, i2, cnt, s: (ie[live(w, cnt)], 0, fidx(w, f, cnt))),
            pl.BlockSpec((None, tf, D), lambda w, f, ie, i1, i2, cnt, s: (ie[live(w, cnt)], fidx(w, f, cnt), 0)),
        ],
        out_specs=pl.BlockSpec(memory_space=pl.ANY),
        scratch_shapes=[
            pltpu.VMEM((max_blocks * sub, D), BF16),
            pltpu.VMEM((max_blocks * sub, D), F32),
            pltpu.VMEM((2, sub, D), F32),
            pltpu.SemaphoreType.DMA((2,)), pltpu.SemaphoreType.DMA(()),
        ],
    )
    return pl.pallas_call(
        functools.partial(_moe_body, sub=sub, unroll=8),
        grid_spec=grid_spec,
        out_shape=jax.ShapeDtypeStruct((n_rows, D), F32),
        compiler_params=_params("arbitrary", "arbitrary"),
        name="moe_ffn",
    )(item_expert, item_first, item_nb, counts, src, h, wg, wu, wd)


def _combine_body(pos_ref, x_ref, gate_ref, y_hbm, *rest, tm, unroll, split_tile):
    o_refs, (buf, sem) = rest[:-2], rest[-2:]
    i = pl.program_id(0)
    slot = i % 2

    def gather(tile, to_slot):
        base = tile * tm * TOP_K

        def issue(rr, carry):
            for u in range(unroll):
                r = rr * unroll + u
                for k in range(TOP_K):
                    _row_copy(y_hbm, pos_ref[base + r * TOP_K + k], buf.at[to_slot, k], r,
                              sem.at[to_slot, k]).start(priority=k)
            return carry

        lax.fori_loop(0, tm // unroll, issue, 0)

    @pl.when(i == 0)
    def _():
        gather(0, 0)

    @pl.when(i + 1 < pl.num_programs(0))
    def _():
        gather(i + 1, 1 - slot)

    out = x_ref[...]
    for k in range(TOP_K):
        pltpu.make_async_copy(y_hbm.at[pl.ds(0, tm)], buf.at[slot, k], sem.at[slot, k]).wait()
        out = out + gate_ref[:, k:k + 1] * buf[slot, k]
    if split_tile is None:
        o_refs[0][...] = out
    else:
        @pl.when(pl.program_id(0) < split_tile)
        def _():
            o_refs[0][...] = out

        @pl.when(pl.program_id(0) >= split_tile)
        def _():
            o_refs[1][...] = out


def _combine(pos, x, gate, y, *, tm, split_rows=None):
    T, D = x.shape
    if split_rows is None:
        split_tile = None
        out_specs = [pl.BlockSpec((tm, D), lambda i, p: (i, 0))]
        out_shape = [jax.ShapeDtypeStruct((T, D), F32)]
    else:
        split_tile = split_rows // tm
        out_specs = [pl.BlockSpec((tm, D), lambda i, p: (jnp.minimum(i, split_tile - 1), 0)),
                     pl.BlockSpec((tm, D), lambda i, p: (jnp.maximum(i - split_tile, 0), 0))]
        out_shape = [jax.ShapeDtypeStruct((split_rows, D), F32), jax.ShapeDtypeStruct((T - split_rows, D), F32)]
    grid_spec = pltpu.PrefetchScalarGridSpec(
        num_scalar_prefetch=1,
        grid=(T // tm,),
        in_specs=[pl.BlockSpec((tm, D), lambda i, p: (i, 0)),
                  pl.BlockSpec((tm, HEAD_DIM), lambda i, p: (i, 0)),
                  pl.BlockSpec(memory_space=pl.ANY)],
        out_specs=out_specs,
        scratch_shapes=[pltpu.VMEM((2, TOP_K, tm, D), F32), pltpu.SemaphoreType.DMA((2, TOP_K))],
    )
    return pl.pallas_call(
        functools.partial(_combine_body, tm=tm, unroll=4, split_tile=split_tile),
        grid_spec=grid_spec,
        out_shape=out_shape,
        compiler_params=_params("arbitrary"),
        name="combine_rows",
    )(pos, x, gate, y)


def _route_tables(eid, rank, counts, sub, max_blocks):
    flat = eid.reshape(-1)
    rank = rank.reshape(-1)
    n_assign = flat.shape[0]
    n_exp = counts.shape[0]
    nsub = (counts + sub - 1) // sub
    sub_end = jnp.cumsum(nsub)
    sub_start = sub_end - nsub
    pos = (sub_start[flat] * sub + rank).astype(I32)
    n_rows = (n_assign // sub + n_exp) * sub
    src = jnp.zeros((n_rows,), I32).at[pos].set(jnp.arange(n_assign, dtype=I32) // TOP_K)
    items = (nsub + max_blocks - 1) // max_blocks
    item_end = jnp.cumsum(items)
    item_start = item_end - items
    n_items_max = (n_rows // sub + n_exp * (max_blocks - 1)) // max_blocks
    w = jnp.arange(n_items_max, dtype=I32)
    item_expert = jnp.minimum(jnp.sum((item_end[None, :] <= w[:, None]).astype(I32), axis=1), n_exp - 1)
    k = w - item_start[item_expert]
    item_first = sub_start[item_expert] + k * max_blocks
    item_nb = jnp.clip(nsub[item_expert] - k * max_blocks, 0, max_blocks)
    counts2 = jnp.stack([item_end[-1], sub_end[-1]]).astype(I32)
    return pos, (item_expert.astype(I32), item_first.astype(I32), item_nb.astype(I32), counts2, src, n_rows)


def _row_tile_multiple(n, cap):
    best = ROW_TILE
    for m in range(ROW_TILE, cap + 1, ROW_TILE):
        if n % m == 0:
            best = m
    return best


def kernel(x_prompt, x_sample, cache_k, cache_v, cache_logf, page_table, norm_mix_g, w_in, b_f,
           q_norm_g, k_norm_g, sgu_norm_g, w_spatial, b_spatial, head_norm_g, w_out, norm_ffn_g,
           dense_w_gate, dense_w_up, dense_w_down, router_w, moe_w_gate, moe_w_up, moe_w_down):
    B, S, D = x_prompt.shape
    DB, DS, _ = x_sample.shape
    depth = w_in.shape[0]
    n_phys, page = cache_k.shape[1], cache_k.shape[2]
    chunk = w_spatial.shape[-1]
    n_exp = router_w.shape[-1]
    assert cache_k.shape[3] == N_HEADS and cache_k.shape[4] == HEAD_DIM
    assert w_in.shape[2] == N_SECTIONS * HEAD_W + N_HEADS and D == 2 * HEAD_W
    Tp, Ts = B * S, DB * DS
    T = Tp + Ts
    assert Ts == ROW_TILE and S % ROW_TILE == 0 and ROW_TILE % chunk == 0
    rows = N_HEADS * DS
    scale = HEAD_DIM ** -0.5
    f_lo = 3 * HEAD_W

    x = jnp.concatenate([x_prompt.reshape(Tp, D), x_sample.reshape(Ts, D)], axis=0)
    cache_lf2 = cache_logf.reshape(depth, n_phys, 1, page * N_HEADS)
    ones = jnp.ones((HEAD_W,), F32)

    outs = {name: [] for name in ("kp", "vp", "fp", "ks", "vs", "fs", "ms")}
    for l in range(depth):
        w_main = jnp.concatenate([w_in[l][:, :f_lo], w_in[l][:, f_lo + N_HEADS:]], axis=1).astype(BF16)
        wf = jnp.pad(w_in[l][:, f_lo:f_lo + N_HEADS], ((0, 0), (0, HEAD_DIM - N_HEADS))).astype(BF16)
        bf = jnp.pad(b_f[l], (0, HEAD_DIM - N_HEADS)).reshape(1, HEAD_DIM)
        g_mix = norm_mix_g[l].reshape(1, D)
        gains = jnp.stack([(q_norm_g[l] * scale).reshape(-1), k_norm_g[l].reshape(-1), ones, ones,
                           sgu_norm_g[l].reshape(-1)]).reshape(N_SECTIONS, 1, HEAD_W)
        hg_att = head_norm_g[l][:N_HEADS]
        hg_mlp = head_norm_g[l][N_HEADS:].reshape(1, HEAD_W)
        wo = w_out[l].astype(BF16)
        g_ffn = norm_ffn_g[l].reshape(1, D)
        wm = jnp.tril(w_spatial[l])
        wc_p = wm.astype(BF16)
        bc_p = jnp.broadcast_to(b_spatial[l][:, :, None], (N_HEADS, chunk, HEAD_DIM))
        tok = jnp.arange(Ts, dtype=I32)
        pick = (tok[:, None] % DS == jnp.arange(DS, dtype=I32)[None, :]).astype(F32)
        same_seq = (tok[:, None] // DS == tok[None, :] // DS).astype(F32)
        wc_s = (jnp.einsum("ri,hij,cj->hrc", pick, wm[:, :DS, :DS], pick, precision=lax.Precision.HIGHEST)
                * same_seq).astype(BF16)
        bc_s = jnp.broadcast_to(jnp.tile(b_spatial[l][:, :DS], (1, DB))[:, :, None], (N_HEADS, Ts, HEAD_DIM))

        z, lf, c = _project(x, g_mix, w_main, wf, bf, gains, tm=_row_tile_multiple(T, 3 * ROW_TILE), seq=S)

        att_p = _fox_prompt(z, c, hg_att.reshape(N_HEADS, 1, HEAD_DIM), batch=B, seq=S, tq=ROW_TILE, hp=4)

        zs = z[Tp:]
        q_rows = zs[:, :HEAD_W].astype(BF16).reshape(DB, DS, N_HEADS, HEAD_DIM).transpose(0, 2, 1, 3).reshape(DB, rows, HEAD_DIM)
        kn = zs[:, HEAD_W:2 * HEAD_W].reshape(DB, rows, HEAD_DIM)
        vn = zs[:, 2 * HEAD_W:3 * HEAD_W].reshape(DB, rows, HEAD_DIM)
        lfs8 = lf[Tp:, :N_HEADS]
        att_rows = _fox_sample(l, page_table, q_rows, kn, vn, lfs8.reshape(DB, 1, rows), lfs8.reshape(DB, rows, 1),
                               jnp.repeat(hg_att, DS, axis=0), cache_k, cache_v, cache_lf2, pps=8)
        att_s = att_rows.reshape(DB, N_HEADS, DS, HEAD_DIM).transpose(0, 2, 1, 3).reshape(Ts, HEAD_W)

        moe_layer = l % 2 == 1
        x, h = _merge(att_p, att_s, z, wc_p, bc_p, wc_s, bc_s, hg_mlp, wo, x, g_ffn,
                      h_dtype=F32 if moe_layer else BF16)

        i = l // 2
        if not moe_layer:
            x = _dense_ffn(h, dense_w_gate[i], dense_w_up[i], dense_w_down[i], x,
                           tm=_row_tile_multiple(T, 3 * ROW_TILE), tf=256)
        else:
            rw = jnp.pad(router_w[i], ((0, 0), (0, HEAD_DIM - n_exp))).astype(BF16)
            eid, gate, counts = _router(h, rw, n_exp, tm=ROW_TILE)
            pos, tables = _route_tables(eid[:, :TOP_K], eid[:, TOP_K:2 * TOP_K], counts[0, :n_exp].astype(I32),
                                        MOE_SUB, MOE_ITEM_BLOCKS)
            y_sorted = _moe_ffn(tables, h, moe_w_gate[i], moe_w_up[i], moe_w_down[i], tf=256,
                                max_blocks=MOE_ITEM_BLOCKS)
            if l == depth - 1:
                x_split = _combine(pos, x, gate, y_sorted, tm=ROW_TILE, split_rows=Tp)
            else:
                x = _combine(pos, x, gate, y_sorted, tm=ROW_TILE)[0]

        outs["kp"].append(z[:Tp, HEAD_W:2 * HEAD_W].reshape(B, S, N_HEADS, HEAD_DIM))
        outs["vp"].append(z[:Tp, 2 * HEAD_W:3 * HEAD_W].reshape(B, S, N_HEADS, HEAD_DIM))
        outs["fp"].append(lf[:Tp, :N_HEADS].reshape(B, S, N_HEADS))
        outs["ks"].append(kn.reshape(DB, DS, N_HEADS, HEAD_DIM))
        outs["vs"].append(vn.reshape(DB, DS, N_HEADS, HEAD_DIM))
        outs["fs"].append(lfs8.reshape(DB, DS, N_HEADS))
        outs["ms"].append(zs[:, 4 * HEAD_W:].reshape(DB, DS, N_HEADS, HEAD_DIM))

    if depth % 2 == 0:
        y_p, y_s = x_split
    else:
        y_p, y_s = x[:Tp], x[Tp:]
    return (y_p.reshape(B, S, D), y_s.reshape(DB, DS, D),
            jnp.stack(outs["kp"]), jnp.stack(outs["vp"]), jnp.stack(outs["fp"]),
            jnp.stack(outs["ks"]), jnp.stack(outs["vs"]), jnp.stack(outs["fs"]), jnp.stack(outs["ms"]))
```

```python
import functools

import jax
import jax.numpy as jnp
from jax import lax
from jax.experimental import pallas as pl
from jax.experimental.pallas import tpu as pltpu

F32 = jnp.float32
BF16 = jnp.bfloat16
I32 = jnp.int32

HEAD_DIM = 128
N_HEADS = 8
HEAD_W = N_HEADS * HEAD_DIM
N_SECTIONS = 5
EPS = 1e-6
NEG_INF = float("-inf")
TOP_K = 2
ROW_TILE = 256
MOE_SUB = ROW_TILE // 2
MOE_ITEM_BLOCKS = 20
VMEM_LIMIT = 56 * 1024 * 1024


def _params(*sem):
    return pltpu.CompilerParams(dimension_semantics=sem, vmem_limit_bytes=VMEM_LIMIT)


def _nt_dot(a, b):
    return lax.dot_general(a, b, (((1,), (1,)), ((), ())), preferred_element_type=F32)


def _rms(x):
    return x * lax.rsqrt(jnp.mean(x * x, axis=-1, keepdims=True) + EPS)


def _split3(x):
    x1 = x.astype(BF16).astype(F32)
    r1 = x - x1
    x2 = r1.astype(BF16).astype(F32)
    x3 = (r1 - x2).astype(BF16).astype(F32)
    return x1, x2, x3


def _pick_lane(x, lane_idx):
    lane = lax.broadcasted_iota(I32, x.shape, 1)
    return jnp.sum(jnp.where(lane == lane_idx, x, 0.0), axis=1, keepdims=True)


def _proj_body(x_ref, g_ref, w_ref, wf_ref, bf_ref, gain_ref, z_ref, lf_ref, c_ref, h_scr, carry_scr,
               *, chunk, chunks_per_seq):
    i = pl.program_id(0)
    j = pl.program_id(1)
    n_chunks = x_ref.shape[0] // chunk

    def section(jj):
        first = jj == 0
        normed = jj in (0, 1, 4)
        w = w_ref[...]
        if first:
            r = lax.broadcasted_iota(I32, (chunk, chunk), 0)
            c = lax.broadcasted_iota(I32, (chunk, chunk), 1)
            lower = (c <= r).astype(BF16)

            @pl.when(i == 0)
            def _():
                carry_scr[...] = jnp.zeros(carry_scr.shape, F32)

            carry = carry_scr[...]
        for n in range(n_chunks):
            rows = slice(n * chunk, (n + 1) * chunk)
            if first:
                hb = (_rms(x_ref[rows, :]) * g_ref[...]).astype(BF16)
                h_scr[rows, :] = hb
                f = jnp.dot(hb, wf_ref[...], preferred_element_type=F32) + bf_ref[...]
                lf = jnp.minimum(f, 0.0) - jnp.log1p(jnp.exp(-jnp.abs(f)))
                lf_ref[rows, :] = lf
                first_of_seq = (i * n_chunks + n) % chunks_per_seq == 0
                cs = jnp.where(first_of_seq, 0.0, carry)
                for part in _split3(lf):
                    cs = cs + jnp.dot(lower, part.astype(BF16), preferred_element_type=F32)
                c_ref[rows, :] = cs
                carry = cs[chunk - 1:chunk, :]
            else:
                hb = h_scr[rows, :]
            z = jnp.dot(hb, w, preferred_element_type=F32)
            if normed:
                for h in range(N_HEADS):
                    sl = slice(h * HEAD_DIM, (h + 1) * HEAD_DIM)
                    z_ref[rows, sl] = _rms(z[:, sl]) * gain_ref[:, sl]
            else:
                z_ref[rows, :] = z
        if first:
            carry_scr[...] = carry

    for jj in range(N_SECTIONS):
        pl.when(j == jj)(functools.partial(section, jj))


def _project(x, g, w_main, wf, bf, gains, *, tm, seq):
    T, D = x.shape
    row = lambda i, j: (i, 0)
    const = lambda i, j: (0, 0)
    return pl.pallas_call(
        functools.partial(_proj_body, chunk=ROW_TILE, chunks_per_seq=seq // ROW_TILE),
        grid=(T // tm, N_SECTIONS),
        in_specs=[
            pl.BlockSpec((tm, D), row),
            pl.BlockSpec((1, D), const),
            pl.BlockSpec((D, HEAD_W), lambda i, j: (0, j)),
            pl.BlockSpec((D, HEAD_DIM), const),
            pl.BlockSpec((1, HEAD_DIM), const),
            pl.BlockSpec((None, 1, HEAD_W), lambda i, j: (j, 0, 0)),
        ],
        out_specs=[pl.BlockSpec((tm, HEAD_W), lambda i, j: (i, j)),
                   pl.BlockSpec((tm, HEAD_DIM), row),
                   pl.BlockSpec((tm, HEAD_DIM), row)],
        out_shape=[
            jax.ShapeDtypeStruct((T, N_SECTIONS * HEAD_W), F32),
            jax.ShapeDtypeStruct((T, HEAD_DIM), F32),
            jax.ShapeDtypeStruct((T, HEAD_DIM), F32),
        ],
        scratch_shapes=[pltpu.VMEM((tm, D), BF16), pltpu.VMEM((1, HEAD_DIM), F32)],
        compiler_params=_params("arbitrary", "arbitrary"),
        name="in_proj",
    )(x, g, w_main, wf, bf, gains)


def _bias_lanes(c_col, key_side):
    lane = lax.broadcasted_iota(I32, (1, HEAD_DIM), 1)
    c1, c2, c3 = _split3(c_col)
    if key_side:
        terms = (1.0, 1.0, 1.0, -c1, -c2, -c3)
    else:
        terms = (c1, c2, c3, 1.0, 1.0, 1.0)
    out = jnp.zeros((c_col.shape[0], HEAD_DIM), F32)
    for idx, t in enumerate(terms):
        out = jnp.where(lane == idx, t, out)
    return out.astype(BF16)


def _fox_prompt_body(q_ref, k_ref, v_ref, cq_ref, ck_ref, hg_ref, o_ref, ka, vt, m_scr, acc_scr, *, tq, hp):
    hgrp = pl.program_id(1)
    qi = pl.program_id(2)
    n_kv = vt.shape[1]
    ext = vt.shape[2] - HEAD_DIM

    @pl.when(qi == 0)
    def _():
        row = lax.broadcasted_iota(I32, (ext, tq), 0)
        ones_ext = jnp.where(row == 0, 1.0, 0.0).astype(BF16)
        for hh in range(hp):
            sl = slice(hh * HEAD_DIM, (hh + 1) * HEAD_DIM)
            ka[hh, :, :HEAD_DIM] = k_ref[:, sl].astype(BF16)
            ka[hh, :, HEAD_DIM:] = _bias_lanes(_pick_lane(ck_ref[...], hgrp * hp + hh), True)
            for n in range(n_kv):
                vt[hh, n, :HEAD_DIM, :] = v_ref[n * tq:(n + 1) * tq, sl].T.astype(BF16)
                vt[hh, n, HEAD_DIM:, :] = ones_ext

    qa = []
    for hh in range(hp):
        sl = slice(hh * HEAD_DIM, (hh + 1) * HEAD_DIM)
        bias = _bias_lanes(_pick_lane(cq_ref[...], hgrp * hp + hh), False)
        qa.append(jnp.concatenate([q_ref[:, sl].astype(BF16), bias], axis=1))
        m_scr[hh] = jnp.full((1, tq), NEG_INF, F32)
        acc_scr[hh] = jnp.zeros(acc_scr.shape[1:], F32)

    def block(kj, diagonal):
        start = pl.multiple_of(kj * tq, tq)
        st = [_nt_dot(ka[hh, pl.ds(start, tq), :], qa[hh]) for hh in range(hp)]
        vts = [vt[hh, kj] for hh in range(hp)]
        ms = [m_scr[hh] for hh in range(hp)]
        accs = [acc_scr[hh] for hh in range(hp)]
        if diagonal:
            key = lax.broadcasted_iota(I32, (tq, tq), 0)
            qry = lax.broadcasted_iota(I32, (tq, tq), 1)
            st = [jnp.where(key <= qry, s, NEG_INF) for s in st]
        m_new = [jnp.maximum(m, jnp.max(s, axis=0, keepdims=True)) for m, s in zip(ms, st)]
        pt = [jnp.exp(s - mn).astype(BF16) for s, mn in zip(st, m_new)]
        pv = [jnp.dot(v, p, preferred_element_type=F32) for v, p in zip(vts, pt)]
        new_acc = [jnp.exp(m - mn) * a + d for m, mn, a, d in zip(ms, m_new, accs, pv)]
        for hh in range(hp):
            acc_scr[hh] = new_acc[hh]
            m_scr[hh] = m_new[hh]

    def body(kj, carry):
        block(kj, False)
        return carry

    lax.fori_loop(0, qi, body, 0)
    block(qi, True)
    for hh in range(hp):
        acc = acc_scr[hh]
        ot = acc[:HEAD_DIM, :] / acc[HEAD_DIM:HEAD_DIM + 1, :]
        ot = ot * lax.rsqrt(jnp.mean(ot * ot, axis=0, keepdims=True) + EPS)
        o_ref[:, hh * HEAD_DIM:(hh + 1) * HEAD_DIM] = (ot.T * hg_ref[hh]).astype(o_ref.dtype)


def _fox_prompt(z, c, hg, *, batch, seq, tq, hp):
    nq = seq // tq
    ngrp = N_HEADS // hp
    w = hp * HEAD_DIM
    ext = 16
    return pl.pallas_call(
        functools.partial(_fox_prompt_body, tq=tq, hp=hp),
        grid=(batch, ngrp, nq),
        in_specs=[
            pl.BlockSpec((tq, w), lambda b, h, i: (b * nq + i, h)),
            pl.BlockSpec((seq, w), lambda b, h, i: (b, ngrp + h)),
            pl.BlockSpec((seq, w), lambda b, h, i: (b, 2 * ngrp + h)),
            pl.BlockSpec((tq, HEAD_DIM), lambda b, h, i: (b * nq + i, 0)),
            pl.BlockSpec((seq, HEAD_DIM), lambda b, h, i: (b, 0)),
            pl.BlockSpec((hp, 1, HEAD_DIM), lambda b, h, i: (h, 0, 0)),
        ],
        out_specs=pl.BlockSpec((tq, w), lambda b, h, i: (b * nq + i, h)),
        out_shape=jax.ShapeDtypeStruct((batch * seq, HEAD_W), BF16),
        scratch_shapes=[pltpu.VMEM((hp, seq, 2 * HEAD_DIM), BF16),
                        pltpu.VMEM((hp, nq, HEAD_DIM + ext, tq), BF16),
                        pltpu.VMEM((hp, 1, tq), F32), pltpu.VMEM((hp, HEAD_DIM + ext, tq), F32)],
        compiler_params=_params("arbitrary", "arbitrary", "arbitrary"),
        name="fox_prompt",
    )(z, z, z, c, c, hg)


def _fox_sample_body(pt_ref, q_ref, kn_ref, vn_ref, lfr_ref, lfc_ref, hg_ref, ck_hbm, cv_hbm, *rest,
                     layer, pps, page, ring):
    lp = rest[:pps]
    o_ref, m_scr, l_scr, acc_scr, carry_scr, negq_scr, kbuf, vbuf, sem = rest[pps:]
    j = pl.program_id(1)
    n_j = pl.num_programs(1)
    rows = q_ref.shape[0]
    dsq = rows // N_HEADS
    cols = page * N_HEADS
    n_pages = n_j * pps

    g = pl.program_id(0) * n_j + j
    n_g = pl.num_programs(0) * n_j

    def fetch(gg):
        slot = gg % ring
        seq = gg // n_j
        first = n_pages - 1 - (gg % n_j) * pps
        for p in range(pps):
            pg = pt_ref[seq, first - p]
            pltpu.make_async_copy(ck_hbm.at[layer, pg], kbuf.at[slot, p], sem.at[slot]).start()
            pltpu.make_async_copy(cv_hbm.at[layer, pg], vbuf.at[slot, p], sem.at[slot]).start()

    @pl.when(g == 0)
    def _():
        for ahead in range(ring - 1):
            @pl.when(ahead < n_g)
            def _():
                fetch(ahead)

    @pl.when(g + ring - 1 < n_g)
    def _():
        fetch(g + ring - 1)

    slot = g % ring
    pltpu.make_async_copy(ck_hbm.at[layer, pl.ds(0, pps)], kbuf.at[slot], sem.at[slot]).wait()
    pltpu.make_async_copy(cv_hbm.at[layer, pl.ds(0, pps)], vbuf.at[slot], sem.at[slot]).wait()

    rn = lax.broadcasted_iota(I32, (rows, rows), 0)
    cn = lax.broadcasted_iota(I32, (rows, rows), 1)
    keep_new = ((rn // dsq) == (cn % N_HEADS)) & ((cn // N_HEADS) <= (rn % dsq))
    cq_col = jnp.sum(jnp.where(keep_new, lfr_ref[...], 0.0), axis=1, keepdims=True)

    @pl.when(j == 0)
    def _():
        m_scr[...] = jnp.full(m_scr.shape, NEG_INF, F32)
        l_scr[...] = jnp.zeros(l_scr.shape, F32)
        acc_scr[...] = jnp.zeros(acc_scr.shape, F32)
        carry_scr[...] = jnp.zeros(carry_scr.shape, F32)
        rp = lax.broadcasted_iota(I32, (rows, cols), 0)
        cp = lax.broadcasted_iota(I32, (rows, cols), 1)
        negq_scr[...] = jnp.where((rp // dsq) == (cp % N_HEADS), cq_col, NEG_INF)

    def update(scores, values):
        m, l, acc = m_scr[...], l_scr[...], acc_scr[...]
        mx = functools.reduce(jnp.maximum, scores)
        m_new = jnp.maximum(m, jnp.max(mx, axis=1, keepdims=True))
        alpha = jnp.exp(m - m_new)
        ps = [jnp.exp(s - m_new) for s in scores]
        pv = [jnp.dot(p.astype(BF16), v, preferred_element_type=F32) for p, v in zip(ps, values)]
        l_scr[...] = alpha * l + jnp.sum(functools.reduce(jnp.add, ps), axis=1, keepdims=True)
        acc_scr[...] = alpha * acc + functools.reduce(jnp.add, pv)
        m_scr[...] = m_new

    q = q_ref[...]
    raw = [_nt_dot(q, kbuf[slot, p].reshape(cols, HEAD_DIM).astype(BF16)) for p in range(pps)]

    lpp = jnp.concatenate([lp[p][...] for p in range(pps)], axis=0)
    lane = lax.broadcasted_iota(I32, (pps, cols), 1)
    sub = lax.broadcasted_iota(I32, (pps, cols), 0)
    n_steps = (page - 1).bit_length()
    tot = lpp
    for t in range(n_steps):
        tot = tot + pltpu.roll(tot, N_HEADS << t, axis=1)
    suf = jnp.where(lane < cols - N_HEADS, pltpu.roll(lpp, cols - N_HEADS, axis=1), 0.0)
    for t in range(n_steps):
        sh = N_HEADS << t
        suf = suf + jnp.where(lane < cols - sh, pltpu.roll(suf, cols - sh, axis=1), 0.0)
    pre = jnp.where(sub >= 1, pltpu.roll(tot, 1, axis=0), 0.0)
    for t in range((pps - 1).bit_length()):
        sh = 1 << t
        pre = pre + jnp.where(sub >= sh, pltpu.roll(pre, sh, axis=0), 0.0)
    carry = carry_scr[...]
    bias = suf + pre + carry

    negq = negq_scr[...]
    v2 = [vbuf[slot, p].reshape(cols, HEAD_DIM).astype(BF16) for p in range(pps)]
    update([raw[p] + negq + bias[p:p + 1, :] for p in range(pps)], v2)
    carry_scr[...] = carry + jnp.sum(tot, axis=0, keepdims=True)

    @pl.when(j == pl.num_programs(1) - 1)
    def _():
        cn_row = jnp.sum(jnp.where(((rn % N_HEADS) == (cn % N_HEADS)) & ((rn // N_HEADS) <= (cn // N_HEADS)),
                                   lfc_ref[...], 0.0), axis=0, keepdims=True)
        s = jnp.where(keep_new, _nt_dot(q, kn_ref[...].astype(BF16)) + cq_col - cn_row, NEG_INF)
        update([s], [vn_ref[...].astype(BF16)])
        o = acc_scr[...] / l_scr[...]
        o_ref[...] = (_rms(o) * hg_ref[...]).astype(o_ref.dtype)


def _fox_sample(layer, page_table, q, kn, vn, lfr, lfc, hg_rows, cache_k, cache_v, cache_lf2, *, pps):
    DB, n_pages = page_table.shape
    rows = q.shape[1]
    page = cache_k.shape[2]
    cols = page * N_HEADS
    n_steps = n_pages // pps

    def seq_map(b, j, pt):
        return (b, 0, 0)

    def lf_map(p):
        def f(b, j, pt):
            return (layer, pt[b, n_pages - 1 - (j * pps + p)], 0, 0)
        return f

    ring = 3
    grid_spec = pltpu.PrefetchScalarGridSpec(
        num_scalar_prefetch=1,
        grid=(DB, n_steps),
        in_specs=[
            pl.BlockSpec((None, rows, HEAD_DIM), seq_map),
            pl.BlockSpec((None, rows, HEAD_DIM), seq_map),
            pl.BlockSpec((None, rows, HEAD_DIM), seq_map),
            pl.BlockSpec((None, 1, rows), seq_map),
            pl.BlockSpec((None, rows, 1), seq_map),
            pl.BlockSpec((rows, HEAD_DIM), lambda b, j, pt: (0, 0)),
            pl.BlockSpec(memory_space=pl.ANY),
            pl.BlockSpec(memory_space=pl.ANY),
        ] + [pl.BlockSpec((None, None, 1, cols), lf_map(p)) for p in range(pps)],
        out_specs=pl.BlockSpec((None, rows, HEAD_DIM), seq_map),
        scratch_shapes=[pltpu.VMEM((rows, 1), F32), pltpu.VMEM((rows, 1), F32),
                        pltpu.VMEM((rows, HEAD_DIM), F32), pltpu.VMEM((1, cols), F32),
                        pltpu.VMEM((rows, cols), F32),
                        pltpu.VMEM((ring, pps, page, N_HEADS, HEAD_DIM), F32),
                        pltpu.VMEM((ring, pps, page, N_HEADS, HEAD_DIM), F32),
                        pltpu.SemaphoreType.DMA((ring,))],
    )
    return pl.pallas_call(
        functools.partial(_fox_sample_body, layer=layer, pps=pps, page=page, ring=ring),
        grid_spec=grid_spec,
        out_shape=jax.ShapeDtypeStruct((DB, rows, HEAD_DIM), BF16),
        compiler_params=_params("arbitrary", "arbitrary"),
        name="fox_sample",
    )(page_table, q, kn, vn, lfr, lfc, hg_rows, cache_k, cache_v, *([cache_lf2] * pps))


def _merge_body(attp_ref, atts_ref, u_ref, vm_ref, wcp_ref, bcp_ref, wcs_ref, bcs_ref, hg_ref, wo_ref,
                x_ref, g_ref, *rest, n_prompt_tiles, n_exp):
    if n_exp:
        rw_ref, xo_ref, ho_ref, eid_ref, gate_ref, cnt_ref, att_scr, mlp_scr = rest
    else:
        xo_ref, ho_ref, att_scr, mlp_scr = rest
    i = pl.program_id(0)
    tm = x_ref.shape[0]

    def gate_heads(wc_ref, bc_ref):
        chunk = wc_ref.shape[1]
        for h in range(N_HEADS):
            sl = slice(h * HEAD_DIM, (h + 1) * HEAD_DIM)
            for c in range(tm // chunk):
                rows = slice(c * chunk, (c + 1) * chunk)
                s = jnp.dot(wc_ref[h], vm_ref[rows, sl].astype(BF16), preferred_element_type=F32) + bc_ref[h]
                o = u_ref[rows, sl] * s
                mlp_scr[rows, sl] = (_rms(o) * hg_ref[:, sl]).astype(BF16)

    @pl.when(i < n_prompt_tiles)
    def _():
        att_scr[...] = attp_ref[...]
        gate_heads(wcp_ref, bcp_ref)

    @pl.when(i >= n_prompt_tiles)
    def _():
        att_scr[...] = atts_ref[...]
        gate_heads(wcs_ref, bcs_ref)

    y = jnp.dot(att_scr[...], wo_ref[:HEAD_W, :], preferred_element_type=F32)
    y = y + jnp.dot(mlp_scr[...], wo_ref[HEAD_W:, :], preferred_element_type=F32)
    xn = x_ref[...] + y
    xo_ref[...] = xn
    hn = _rms(xn) * g_ref[...]
    ho_ref[...] = hn.astype(ho_ref.dtype)
    if n_exp:
        _route_tile(hn.astype(BF16), rw_ref, eid_ref, gate_ref, cnt_ref, n_exp)


def _merge(att_p, att_s, z, wc_p, bc_p, wc_s, bc_s, hg_mlp, wo, x, g, *, h_dtype, route_w=None, n_exp=0):
    T, D = x.shape
    tm = ROW_TILE
    n_prompt_tiles = att_p.shape[0] // tm
    assert att_s.shape[0] == tm and wc_s.shape[1] == tm
    chunk = wc_p.shape[1]
    row = lambda i: (i, 0)
    c2 = lambda i: (0, 0)
    c3 = lambda i: (0, 0, 0)
    route_in = [] if route_w is None else [pl.BlockSpec((D, HEAD_DIM), c2)]
    route_args = [] if route_w is None else [route_w]
    route_out = [] if route_w is None else [pl.BlockSpec((tm, HEAD_DIM), row), pl.BlockSpec((tm, HEAD_DIM), row),
                                            pl.BlockSpec((1, HEAD_DIM), c2)]
    route_shape = [] if route_w is None else [jax.ShapeDtypeStruct((T, HEAD_DIM), I32),
                                              jax.ShapeDtypeStruct((T, HEAD_DIM), F32),
                                              jax.ShapeDtypeStruct((1, HEAD_DIM), F32)]
    return pl.pallas_call(
        functools.partial(_merge_body, n_prompt_tiles=n_prompt_tiles, n_exp=n_exp if route_w is not None else 0),
        grid=(T // tm,),
        in_specs=[
            pl.BlockSpec((tm, HEAD_W), lambda i: (jnp.minimum(i, n_prompt_tiles - 1), 0)),
            pl.BlockSpec((tm, HEAD_W), c2),
            pl.BlockSpec((tm, HEAD_W), lambda i: (i, 3)),
            pl.BlockSpec((tm, HEAD_W), lambda i: (i, 4)),
            pl.BlockSpec((N_HEADS, chunk, chunk), c3),
            pl.BlockSpec((N_HEADS, chunk, HEAD_DIM), c3),
            pl.BlockSpec((N_HEADS, tm, tm), c3),
            pl.BlockSpec((N_HEADS, tm, HEAD_DIM), c3),
            pl.BlockSpec((1, HEAD_W), c2),
            pl.BlockSpec((2 * HEAD_W, D), c2),
            pl.BlockSpec((tm, D), row),
            pl.BlockSpec((1, D), c2),
        ] + route_in,
        out_specs=[pl.BlockSpec((tm, D), row), pl.BlockSpec((tm, D), row)] + route_out,
        out_shape=[jax.ShapeDtypeStruct((T, D), F32), jax.ShapeDtypeStruct((T, D), h_dtype)] + route_shape,
        scratch_shapes=[pltpu.VMEM((tm, HEAD_W), BF16), pltpu.VMEM((tm, HEAD_W), BF16)],
        compiler_params=_params("arbitrary"),
        name="mlp_merge_out",
    )(att_p, att_s, z, z, wc_p, bc_p, wc_s, bc_s, hg_mlp, wo, x, g, *route_args)


def _dense_body(h_ref, wg_ref, wu_ref, wd_ref, *rest, has_tail):
    if has_tail:
        wgt_ref, wut_ref, wdt_ref, x_ref, o_ref = rest
    else:
        x_ref, o_ref = rest
    f = pl.program_id(1)

    @pl.when(f == 0)
    def _():
        o_ref[...] = x_ref[...]

    n_chunks = h_ref.shape[0] // ROW_TILE

    def swiglu(wg, wu, wd):
        wgv, wuv, wdv = (ref[...].astype(BF16) for ref in (wg, wu, wd))
        hs = [h_ref[n * ROW_TILE:(n + 1) * ROW_TILE, :] for n in range(n_chunks)]
        ab = [(jnp.dot(h, wgv, preferred_element_type=F32), jnp.dot(h, wuv, preferred_element_type=F32)) for h in hs]
        gs = [(a * jax.nn.sigmoid(a) * b).astype(BF16) for a, b in ab]
        ds = [jnp.dot(g, wdv, preferred_element_type=F32) for g in gs]
        for n, d in enumerate(ds):
            o_ref[n * ROW_TILE:(n + 1) * ROW_TILE, :] += d

    swiglu(wg_ref, wu_ref, wd_ref)

    if has_tail:
        @pl.when(f == pl.num_programs(1) - 1)
        def _():
            swiglu(wgt_ref, wut_ref, wdt_ref)


def _dense_ffn(h, wg, wu, wd, x, *, tm, tf):
    T, D = x.shape
    d_ff = wg.shape[1]
    n_main = d_ff // tf
    tail = d_ff - n_main * tf
    assert tail % HEAD_DIM == 0 and (n_main * tf) % max(tail, 1) == 0
    in_specs = [
        pl.BlockSpec((tm, D), lambda i, f: (i, 0)),
        pl.BlockSpec((D, tf), lambda i, f: (0, f)),
        pl.BlockSpec((D, tf), lambda i, f: (0, f)),
        pl.BlockSpec((tf, D), lambda i, f: (f, 0)),
    ]
    args = [h, wg, wu, wd]
    if tail:
        t_idx = n_main * tf // tail
        in_specs += [pl.BlockSpec((D, tail), lambda i, f: (0, t_idx)),
                     pl.BlockSpec((D, tail), lambda i, f: (0, t_idx)),
                     pl.BlockSpec((tail, D), lambda i, f: (t_idx, 0))]
        args += [wg, wu, wd]
    in_specs.append(pl.BlockSpec((tm, D), lambda i, f: (i, 0)))
    args.append(x)
    return pl.pallas_call(
        functools.partial(_dense_body, has_tail=bool(tail)),
        grid=(T // tm, n_main),
        in_specs=in_specs,
        out_specs=pl.BlockSpec((tm, D), lambda i, f: (i, 0)),
        out_shape=jax.ShapeDtypeStruct((T, D), F32),
        compiler_params=_params("arbitrary", "arbitrary"),
        name="dense_ffn",
    )(*args)


def _route_tile(hb, w_ref, eid_ref, gate_ref, cnt_ref, n_exp):
    tm = hb.shape[0]
    logits = jnp.dot(hb, w_ref[...], preferred_element_type=F32)
    lane = lax.broadcasted_iota(I32, logits.shape, 1)
    big = logits.shape[1]
    l1 = jnp.where(lane < n_exp, logits, NEG_INF)
    m1 = jnp.max(l1, axis=1, keepdims=True)
    i1 = jnp.min(jnp.where(l1 == m1, lane, big), axis=1, keepdims=True)
    l2 = jnp.where(lane == i1, NEG_INF, l1)
    m2 = jnp.max(l2, axis=1, keepdims=True)
    i2 = jnp.min(jnp.where(l2 == m2, lane, big), axis=1, keepdims=True)
    e2 = jnp.exp(m2 - m1)
    p1 = 1.0 / (1.0 + e2)
    p2 = e2 / (1.0 + e2)
    gate_ref[...] = jnp.where(lane == 0, p1, jnp.where(lane == 1, p2, 0.0))

    @pl.when(pl.program_id(0) == 0)
    def _():
        cnt_ref[...] = jnp.zeros(cnt_ref.shape, F32)

    hit1 = (lane == i1).astype(F32)
    hit2 = (lane == i2).astype(F32)
    r = lax.broadcasted_iota(I32, (tm, tm), 0)
    c = lax.broadcasted_iota(I32, (tm, tm), 1)
    before = jnp.dot((c < r).astype(BF16), (hit1 + hit2).astype(BF16), preferred_element_type=F32)
    before = before + cnt_ref[...]
    rank1 = jnp.sum(hit1 * before, axis=1, keepdims=True).astype(I32)
    rank2 = jnp.sum(hit2 * before, axis=1, keepdims=True).astype(I32)
    cnt_ref[...] += jnp.sum(hit1 + hit2, axis=0, keepdims=True)
    eid_ref[...] = jnp.where(lane == 0, i1, jnp.where(lane == 1, i2,
                             jnp.where(lane == TOP_K, rank1, jnp.where(lane == TOP_K + 1, rank2, 0))))


def _row_copy(src_hbm, row, dst_vmem, r, sem):
    return pltpu.make_async_copy(src_hbm.at[pl.ds(row, 1)], dst_vmem.at[pl.ds(r, 1)], sem)


def _moe_body(ie_ref, ifirst_ref, inb_ref, cnt_ref, src_ref,
              h_hbm, wg_ref, wu_ref, wd_ref, y_hbm,
              xbuf, acc, stage, sem_in, sem_out, *, sub, unroll):
    del ie_ref
    w = pl.program_id(0)
    f = pl.program_id(1)
    n_f = pl.num_programs(1)
    n_items, n_used = cnt_ref[0], cnt_ref[1]
    n_total = y_hbm.shape[0] // sub

    def block_rows(r):
        return pl.ds(pl.multiple_of(r * sub, sub), sub)

    def out_copy(r_local, r_global):
        return pltpu.make_async_copy(acc.at[block_rows(r_local)], y_hbm.at[block_rows(r_global)], sem_out)

    def drain_out(n):
        def wait(r, carry):
            out_copy(r, r).wait()
            return carry

        lax.fori_loop(0, n, wait, 0)

    @pl.when((w == 0) & (f == 0))
    def _():
        acc[block_rows(0)] = jnp.zeros((sub, acc.shape[1]), F32)

        def fill(r, carry):
            out_copy(0, r).start()
            return carry

        def drain(r, carry):
            out_copy(0, r).wait()
            return carry

        lax.fori_loop(n_used, n_total, fill, 0)
        lax.fori_loop(n_used, n_total, drain, 0)

    @pl.when(w < n_items)
    def _():
        first = ifirst_ref[w]
        nb = inb_ref[w]

        @pl.when(f == 0)
        def _():
            def issue(blk, slot):
                base = (first + blk) * sub

                def body(rr, carry):
                    for u in range(unroll):
                        r = rr * unroll + u
                        _row_copy(h_hbm, src_ref[base + r], stage.at[slot], r, sem_in.at[slot]).start(priority=u % 2)
                    return carry

                lax.fori_loop(0, sub // unroll, body, 0)

            issue(0, 0)

            @pl.when(w > 0)
            def _():
                drain_out(inb_ref[w - 1])

            def per_block(blk, carry):
                slot = blk % 2

                @pl.when(blk + 1 < nb)
                def _():
                    issue(blk + 1, 1 - slot)

                pltpu.make_async_copy(h_hbm.at[pl.ds(0, sub)], stage.at[slot], sem_in.at[slot]).wait()
                xbuf[block_rows(blk)] = stage[slot].astype(BF16)
                acc[block_rows(blk)] = jnp.zeros((sub, acc.shape[1]), F32)
                return carry

            lax.fori_loop(0, nb, per_block, 0)

        def swiglu(chunks):
            spans = [pl.ds(start if isinstance(start, int) else pl.multiple_of(start, sub), size)
                     for start, size in chunks]
            xs = [xbuf[s] for s in spans]
            wgv, wuv, wdv = (ref[...].astype(BF16) for ref in (wg_ref, wu_ref, wd_ref))
            ab = [(jnp.dot(x, wgv, preferred_element_type=F32),
                   jnp.dot(x, wuv, preferred_element_type=F32)) for x in xs]
            hs = [(a * jax.nn.sigmoid(a) * b).astype(BF16) for a, b in ab]
            ds = [jnp.dot(hh, wdv, preferred_element_type=F32) for hh in hs]
            for s, d in zip(spans, ds):
                acc[s] += d

        wide = 2 * sub
        n_wide = nb // 2

        for rp in range(xbuf.shape[0] // (2 * wide)):
            @pl.when(rp < n_wide // 2)
            def _():
                swiglu([(2 * rp * wide, wide), ((2 * rp + 1) * wide, wide)])

        @pl.when(n_wide % 2 == 1)
        def _():
            swiglu([((n_wide - 1) * wide, wide)])

        @pl.when(nb % 2 == 1)
        def _():
            swiglu([((nb - 1) * sub, sub)])

        @pl.when(f == n_f - 1)
        def _():
            def start(r, carry):
                out_copy(r, first + r).start()
                return carry

            lax.fori_loop(0, nb, start, 0)

            @pl.when(w == n_items - 1)
            def _():
                drain_out(nb)


def _moe_ffn(tables, h, wg, wu, wd, *, tf, max_blocks):
    item_expert, item_first, item_nb, counts, src, n_rows = tables
    T, D = h.shape
    sub = MOE_SUB
    n_f = wg.shape[2] // tf
    last = n_f - 1

    def live(w, cnt):
        return jnp.minimum(w, cnt[0] - 1)

    def fidx(w, f, cnt):
        return jnp.where(w < cnt[0], f, last)

    grid_spec = pltpu.PrefetchScalarGridSpec(
        num_scalar_prefetch=5,
        grid=(item_expert.shape[0], n_f),
        in_specs=[
            pl.BlockSpec(memory_space=pl.ANY),
            pl.BlockSpec((None, D, tf), lambda w, f, ie, i1, i2, cnt, s: (ie[live(w, cnt)], 0, fidx(w, f, cnt))),
            pl.BlockSpec((None, D, tf), lambda w, f, ie, i1, i2, cnt, s: (ie[live(w, cnt)], 0, fidx(w, f, cnt))),
            pl.BlockSpec((None, tf, D), lambda w, f, ie, i1, i2, cnt, s: (ie[live(w, cnt)], fidx(w, f, cnt), 0)),
        ],
        out_specs=pl.BlockSpec(memory_space=pl.ANY),
        scratch_shapes=[
            pltpu.VMEM((max_blocks * sub, D), BF16),
            pltpu.VMEM((max_blocks * sub, D), F32),
            pltpu.VMEM((2, sub, D), F32),
            pltpu.SemaphoreType.DMA((2,)), pltpu.SemaphoreType.DMA(()),
        ],
    )
    return pl.pallas_call(
        functools.partial(_moe_body, sub=sub, unroll=8),
        grid_spec=grid_spec,
        out_shape=jax.ShapeDtypeStruct((n_rows, D), F32),
        compiler_params=_params("arbitrary", "arbitrary"),
        name="moe_ffn",
    )(item_expert, item_first, item_nb, counts, src, h, wg, wu, wd)


def _combine_body(pos_ref, x_ref, gate_ref, y_hbm, *rest, tm, unroll, split_tile):
    o_refs, (buf, sem) = rest[:-2], rest[-2:]
    i = pl.program_id(0)
    slot = i % 2

    def gather(tile, to_slot):
        base = tile * tm * TOP_K

        def issue(rr, carry):
            for u in range(unroll):
                r = rr * unroll + u
                for k in range(TOP_K):
                    _row_copy(y_hbm, pos_ref[base + r * TOP_K + k], buf.at[to_slot, k], r,
                              sem.at[to_slot, k]).start(priority=k)
            return carry

        lax.fori_loop(0, tm // unroll, issue, 0)

    @pl.when(i == 0)
    def _():
        gather(0, 0)

    @pl.when(i + 1 < pl.num_programs(0))
    def _():
        gather(i + 1, 1 - slot)

    out = x_ref[...]
    for k in range(TOP_K):
        pltpu.make_async_copy(y_hbm.at[pl.ds(0, tm)], buf.at[slot, k], sem.at[slot, k]).wait()
        out = out + gate_ref[:, k:k + 1] * buf[slot, k]
    if split_tile is None:
        o_refs[0][...] = out
    else:
        @pl.when(pl.program_id(0) < split_tile)
        def _():
            o_refs[0][...] = out

        @pl.when(pl.program_id(0) >= split_tile)
        def _():
            o_refs[1][...] = out


def _combine(pos, x, gate, y, *, tm, split_rows=None):
    T, D = x.shape
    if split_rows is None:
        split_tile = None
        out_specs = [pl.BlockSpec((tm, D), lambda i, p: (i, 0))]
        out_shape = [jax.ShapeDtypeStruct((T, D), F32)]
    else:
        split_tile = split_rows // tm
        out_specs = [pl.BlockSpec((tm, D), lambda i, p: (jnp.minimum(i, split_tile - 1), 0)),
                     pl.BlockSpec((tm, D), lambda i, p: (jnp.maximum(i - split_tile, 0), 0))]
        out_shape = [jax.ShapeDtypeStruct((split_rows, D), F32), jax.ShapeDtypeStruct((T - split_rows, D), F32)]
    grid_spec = pltpu.PrefetchScalarGridSpec(
        num_scalar_prefetch=1,
        grid=(T // tm,),
        in_specs=[pl.BlockSpec((tm, D), lambda i, p: (i, 0)),
                  pl.BlockSpec((tm, HEAD_DIM), lambda i, p: (i, 0)),
                  pl.BlockSpec(memory_space=pl.ANY)],
        out_specs=out_specs,
        scratch_shapes=[pltpu.VMEM((2, TOP_K, tm, D), F32), pltpu.SemaphoreType.DMA((2, TOP_K))],
    )
    return pl.pallas_call(
        functools.partial(_combine_body, tm=tm, unroll=4, split_tile=split_tile),
        grid_spec=grid_spec,
        out_shape=out_shape,
        compiler_params=_params("arbitrary"),
        name="combine_rows",
    )(pos, x, gate, y)


def _route_tables(eid, rank, counts, sub, max_blocks):
    flat = eid.reshape(-1)
    rank = rank.reshape(-1)
    n_assign = flat.shape[0]
    n_exp = counts.shape[0]
    nsub = (counts + sub - 1) // sub
    sub_end = jnp.cumsum(nsub)
    sub_start = sub_end - nsub
    pos = (sub_start[flat] * sub + rank).astype(I32)
    n_rows = (n_assign // sub + n_exp) * sub
    src = jnp.zeros((n_rows,), I32).at[pos].set(jnp.arange(n_assign, dtype=I32) // TOP_K)
    items = (nsub + max_blocks - 1) // max_blocks
    item_end = jnp.cumsum(items)
    item_start = item_end - items
    n_items_max = (n_rows // sub + n_exp * (max_blocks - 1)) // max_blocks
    w = jnp.arange(n_items_max, dtype=I32)
    item_expert = jnp.minimum(jnp.sum((item_end[None, :] <= w[:, None]).astype(I32), axis=1), n_exp - 1)
    k = w - item_start[item_expert]
    item_first = sub_start[item_expert] + k * max_blocks
    item_nb = jnp.clip(nsub[item_expert] - k * max_blocks, 0, max_blocks)
    counts2 = jnp.stack([item_end[-1], sub_end[-1]]).astype(I32)
    return pos, (item_expert.astype(I32), item_first.astype(I32), item_nb.astype(I32), counts2, src, n_rows)


def _row_tile_multiple(n, cap):
    best = ROW_TILE
    for m in range(ROW_TILE, cap + 1, ROW_TILE):
        if n % m == 0:
            best = m
    return best


def kernel(x_prompt, x_sample, cache_k, cache_v, cache_logf, page_table, norm_mix_g, w_in, b_f,
           q_norm_g, k_norm_g, sgu_norm_g, w_spatial, b_spatial, head_norm_g, w_out, norm_ffn_g,
           dense_w_gate, dense_w_up, dense_w_down, router_w, moe_w_gate, moe_w_up, moe_w_down):
    B, S, D = x_prompt.shape
    DB, DS, _ = x_sample.shape
    depth = w_in.shape[0]
    n_phys, page = cache_k.shape[1], cache_k.shape[2]
    chunk = w_spatial.shape[-1]
    n_exp = router_w.shape[-1]
    assert cache_k.shape[3] == N_HEADS and cache_k.shape[4] == HEAD_DIM
    assert w_in.shape[2] == N_SECTIONS * HEAD_W + N_HEADS and D == 2 * HEAD_W
    Tp, Ts = B * S, DB * DS
    T = Tp + Ts
    assert Ts == ROW_TILE and S % ROW_TILE == 0 and ROW_TILE % chunk == 0
    rows = N_HEADS * DS
    scale = HEAD_DIM ** -0.5
    f_lo = 3 * HEAD_W

    x = jnp.concatenate([x_prompt.reshape(Tp, D), x_sample.reshape(Ts, D)], axis=0)
    cache_lf2 = cache_logf.reshape(depth, n_phys, 1, page * N_HEADS)
    ones = jnp.ones((HEAD_W,), F32)

    outs = {name: [] for name in ("kp", "vp", "fp", "ks", "vs", "fs", "ms")}
    for l in range(depth):
        w_main = jnp.concatenate([w_in[l][:, :f_lo], w_in[l][:, f_lo + N_HEADS:]], axis=1).astype(BF16)
        wf = jnp.pad(w_in[l][:, f_lo:f_lo + N_HEADS], ((0, 0), (0, HEAD_DIM - N_HEADS))).astype(BF16)
        bf = jnp.pad(b_f[l], (0, HEAD_DIM - N_HEADS)).reshape(1, HEAD_DIM)
        g_mix = norm_mix_g[l].reshape(1, D)
        gains = jnp.stack([(q_norm_g[l] * scale).reshape(-1), k_norm_g[l].reshape(-1), ones, ones,
                           sgu_norm_g[l].reshape(-1)]).reshape(N_SECTIONS, 1, HEAD_W)
        hg_att = head_norm_g[l][:N_HEADS]
        hg_mlp = head_norm_g[l][N_HEADS:].reshape(1, HEAD_W)
        wo = w_out[l].astype(BF16)
        g_ffn = norm_ffn_g[l].reshape(1, D)
        wm = jnp.tril(w_spatial[l])
        wc_p = wm.astype(BF16)
        bc_p = jnp.broadcast_to(b_spatial[l][:, :, None], (N_HEADS, chunk, HEAD_DIM))
        tok = jnp.arange(Ts, dtype=I32)
        pick = (tok[:, None] % DS == jnp.arange(DS, dtype=I32)[None, :]).astype(F32)
        same_seq = (tok[:, None] // DS == tok[None, :] // DS).astype(F32)
        wc_s = (jnp.einsum("ri,hij,cj->hrc", pick, wm[:, :DS, :DS], pick, precision=lax.Precision.HIGHEST)
                * same_seq).astype(BF16)
        bc_s = jnp.broadcast_to(jnp.tile(b_spatial[l][:, :DS], (1, DB))[:, :, None], (N_HEADS, Ts, HEAD_DIM))

        z, lf, c = _project(x, g_mix, w_main, wf, bf, gains, tm=_row_tile_multiple(T, 3 * ROW_TILE), seq=S)

        att_p = _fox_prompt(z, c, hg_att.reshape(N_HEADS, 1, HEAD_DIM), batch=B, seq=S, tq=ROW_TILE, hp=4)

        zs = z[Tp:]
        q_rows = zs[:, :HEAD_W].astype(BF16).reshape(DB, DS, N_HEADS, HEAD_DIM).transpose(0, 2, 1, 3).reshape(DB, rows, HEAD_DIM)
        kn = zs[:, HEAD_W:2 * HEAD_W].reshape(DB, rows, HEAD_DIM)
        vn = zs[:, 2 * HEAD_W:3 * HEAD_W].reshape(DB, rows, HEAD_DIM)
        lfs8 = lf[Tp:, :N_HEADS]
        att_rows = _fox_sample(l, page_table, q_rows, kn, vn, lfs8.reshape(DB, 1, rows), lfs8.reshape(DB, rows, 1),
                               jnp.repeat(hg_att, DS, axis=0), cache_k, cache_v, cache_lf2, pps=8)
        att_s = att_rows.reshape(DB, N_HEADS, DS, HEAD_DIM).transpose(0, 2, 1, 3).reshape(Ts, HEAD_W)

        moe_layer = l % 2 == 1
        i = l // 2
        if moe_layer:
            rw = jnp.pad(router_w[i], ((0, 0), (0, HEAD_DIM - n_exp))).astype(BF16)
            x, h, eid, gate, counts = _merge(att_p, att_s, z, wc_p, bc_p, wc_s, bc_s, hg_mlp, wo, x, g_ffn,
                                             h_dtype=F32, route_w=rw, n_exp=n_exp)
        else:
            x, h = _merge(att_p, att_s, z, wc_p, bc_p, wc_s, bc_s, hg_mlp, wo, x, g_ffn, h_dtype=BF16)

        if not moe_layer:
            x = _dense_ffn(h, dense_w_gate[i], dense_w_up[i], dense_w_down[i], x,
                           tm=_row_tile_multiple(T, 3 * ROW_TILE), tf=256)
        else:
            pos, tables = _route_tables(eid[:, :TOP_K], eid[:, TOP_K:2 * TOP_K], counts[0, :n_exp].astype(I32),
                                        MOE_SUB, MOE_ITEM_BLOCKS)
            y_sorted = _moe_ffn(tables, h, moe_w_gate[i], moe_w_up[i], moe_w_down[i], tf=256,
                                max_blocks=MOE_ITEM_BLOCKS)
            if l == depth - 1:
                x_split = _combine(pos, x, gate, y_sorted, tm=ROW_TILE, split_rows=Tp)
            else:
                x = _combine(pos, x, gate, y_sorted, tm=ROW_TILE)[0]

        outs["kp"].append(z[:Tp, HEAD_W:2 * HEAD_W].reshape(B, S, N_HEADS, HEAD_DIM))
        outs["vp"].append(z[:Tp, 2 * HEAD_W:3 * HEAD_W].reshape(B, S, N_HEADS, HEAD_DIM))
        outs["fp"].append(lf[:Tp, :N_HEADS].reshape(B, S, N_HEADS))
        outs["ks"].append(kn.reshape(DB, DS, N_HEADS, HEAD_DIM))
        outs["vs"].append(vn.reshape(DB, DS, N_HEADS, HEAD_DIM))
        outs["fs"].append(lfs8.reshape(DB, DS, N_HEADS))
        outs["ms"].append(zs[:, 4 * HEAD_W:].reshape(DB, DS, N_HEADS, HEAD_DIM))

    if depth % 2 == 0:
        y_p, y_s = x_split
    else:
        y_p, y_s = x[:Tp], x[Tp:]
    return (y_p.reshape(B, S, D), y_s.reshape(DB, DS, D),
            jnp.stack(outs["kp"]), jnp.stack(outs["vp"]), jnp.stack(outs["fp"]),
            jnp.stack(outs["ks"]), jnp.stack(outs["vs"]), jnp.stack(outs["fs"]), jnp.stack(outs["ms"]))
```

```python
import functools

import jax
import jax.numpy as jnp
from jax import lax
from jax.experimental import pallas as pl
from jax.experimental.pallas import tpu as pltpu

F32 = jnp.float32
BF16 = jnp.bfloat16
I32 = jnp.int32

HEAD_DIM = 128
N_HEADS = 8
HEAD_W = N_HEADS * HEAD_DIM
N_SECTIONS = 5
EPS = 1e-6
NEG_INF = float("-inf")
TOP_K = 2
ROW_TILE = 256
MOE_SUB = ROW_TILE // 2
MOE_ITEM_BLOCKS = 20
VMEM_LIMIT = 56 * 1024 * 1024


def _params(*sem):
    return pltpu.CompilerParams(dimension_semantics=sem, vmem_limit_bytes=VMEM_LIMIT)


def _nt_dot(a, b):
    return lax.dot_general(a, b, (((1,), (1,)), ((), ())), preferred_element_type=F32)


def _rms(x):
    return x * lax.rsqrt(jnp.mean(x * x, axis=-1, keepdims=True) + EPS)


def _split3(x):
    x1 = x.astype(BF16).astype(F32)
    r1 = x - x1
    x2 = r1.astype(BF16).astype(F32)
    x3 = (r1 - x2).astype(BF16).astype(F32)
    return x1, x2, x3


def _pick_lane(x, lane_idx):
    lane = lax.broadcasted_iota(I32, x.shape, 1)
    return jnp.sum(jnp.where(lane == lane_idx, x, 0.0), axis=1, keepdims=True)


def _proj_body(x_ref, g_ref, w_ref, wf_ref, bf_ref, gain_ref, z_ref, lf_ref, c_ref, h_scr, carry_scr,
               *, chunk, chunks_per_seq):
    i = pl.program_id(0)
    j = pl.program_id(1)
    n_chunks = x_ref.shape[0] // chunk

    def section(jj):
        first = jj == 0
        normed = jj in (0, 1, 4)
        w = w_ref[...]
        if first:
            r = lax.broadcasted_iota(I32, (chunk, chunk), 0)
            c = lax.broadcasted_iota(I32, (chunk, chunk), 1)
            lower = (c <= r).astype(BF16)

            @pl.when(i == 0)
            def _():
                carry_scr[...] = jnp.zeros(carry_scr.shape, F32)

            carry = carry_scr[...]
        for n in range(n_chunks):
            rows = slice(n * chunk, (n + 1) * chunk)
            if first:
                hb = (_rms(x_ref[rows, :]) * g_ref[...]).astype(BF16)
                h_scr[rows, :] = hb
                f = jnp.dot(hb, wf_ref[...], preferred_element_type=F32) + bf_ref[...]
                lf = jnp.minimum(f, 0.0) - jnp.log1p(jnp.exp(-jnp.abs(f)))
                lf_ref[rows, :] = lf
                first_of_seq = (i * n_chunks + n) % chunks_per_seq == 0
                cs = jnp.where(first_of_seq, 0.0, carry)
                for part in _split3(lf):
                    cs = cs + jnp.dot(lower, part.astype(BF16), preferred_element_type=F32)
                c_ref[rows, :] = cs
                carry = cs[chunk - 1:chunk, :]
            else:
                hb = h_scr[rows, :]
            z = jnp.dot(hb, w, preferred_element_type=F32)
            if normed:
                for h in range(N_HEADS):
                    sl = slice(h * HEAD_DIM, (h + 1) * HEAD_DIM)
                    z_ref[rows, sl] = _rms(z[:, sl]) * gain_ref[:, sl]
            else:
                z_ref[rows, :] = z
        if first:
            carry_scr[...] = carry

    for jj in range(N_SECTIONS):
        pl.when(j == jj)(functools.partial(section, jj))


def _project(x, g, w_main, wf, bf, gains, *, tm, seq):
    T, D = x.shape
    row = lambda i, j: (i, 0)
    const = lambda i, j: (0, 0)
    return pl.pallas_call(
        functools.partial(_proj_body, chunk=ROW_TILE, chunks_per_seq=seq // ROW_TILE),
        grid=(T // tm, N_SECTIONS),
        in_specs=[
            pl.BlockSpec((tm, D), row),
            pl.BlockSpec((1, D), const),
            pl.BlockSpec((D, HEAD_W), lambda i, j: (0, j)),
            pl.BlockSpec((D, HEAD_DIM), const),
            pl.BlockSpec((1, HEAD_DIM), const),
            pl.BlockSpec((None, 1, HEAD_W), lambda i, j: (j, 0, 0)),
        ],
        out_specs=[pl.BlockSpec((tm, HEAD_W), lambda i, j: (i, j)),
                   pl.BlockSpec((tm, HEAD_DIM), row),
                   pl.BlockSpec((tm, HEAD_DIM), row)],
        out_shape=[
            jax.ShapeDtypeStruct((T, N_SECTIONS * HEAD_W), F32),
            jax.ShapeDtypeStruct((T, HEAD_DIM), F32),
            jax.ShapeDtypeStruct((T, HEAD_DIM), F32),
        ],
        scratch_shapes=[pltpu.VMEM((tm, D), BF16), pltpu.VMEM((1, HEAD_DIM), F32)],
        compiler_params=_params("arbitrary", "arbitrary"),
        name="in_proj",
    )(x, g, w_main, wf, bf, gains)


def _bias_lanes(c_col, key_side):
    lane = lax.broadcasted_iota(I32, (1, HEAD_DIM), 1)
    c1, c2, c3 = _split3(c_col)
    if key_side:
        terms = (1.0, 1.0, 1.0, -c1, -c2, -c3)
    else:
        terms = (c1, c2, c3, 1.0, 1.0, 1.0)
    out = jnp.zeros((c_col.shape[0], HEAD_DIM), F32)
    for idx, t in enumerate(terms):
        out = jnp.where(lane == idx, t, out)
    return out.astype(BF16)


def _fox_prompt_body(q_ref, k_ref, v_ref, cq_ref, ck_ref, hg_ref, o_ref, ka, vt, m_scr, acc_scr, *, tq, hp):
    hgrp = pl.program_id(1)
    qi = pl.program_id(2)
    n_kv = vt.shape[1]
    ext = vt.shape[2] - HEAD_DIM

    @pl.when(qi == 0)
    def _():
        row = lax.broadcasted_iota(I32, (ext, tq), 0)
        ones_ext = jnp.where(row == 0, 1.0, 0.0).astype(BF16)
        for hh in range(hp):
            sl = slice(hh * HEAD_DIM, (hh + 1) * HEAD_DIM)
            ka[hh, :, :HEAD_DIM] = k_ref[:, sl].astype(BF16)
            ka[hh, :, HEAD_DIM:] = _bias_lanes(_pick_lane(ck_ref[...], hgrp * hp + hh), True)
            for n in range(n_kv):
                vt[hh, n, :HEAD_DIM, :] = v_ref[n * tq:(n + 1) * tq, sl].T.astype(BF16)
                vt[hh, n, HEAD_DIM:, :] = ones_ext

    qa = []
    for hh in range(hp):
        sl = slice(hh * HEAD_DIM, (hh + 1) * HEAD_DIM)
        bias = _bias_lanes(_pick_lane(cq_ref[...], hgrp * hp + hh), False)
        qa.append(jnp.concatenate([q_ref[:, sl].astype(BF16), bias], axis=1))
        m_scr[hh] = jnp.full((1, tq), NEG_INF, F32)
        acc_scr[hh] = jnp.zeros(acc_scr.shape[1:], F32)

    def block(kj, diagonal):
        start = pl.multiple_of(kj * tq, tq)
        st = [_nt_dot(ka[hh, pl.ds(start, tq), :], qa[hh]) for hh in range(hp)]
        vts = [vt[hh, kj] for hh in range(hp)]
        ms = [m_scr[hh] for hh in range(hp)]
        accs = [acc_scr[hh] for hh in range(hp)]
        if diagonal:
            key = lax.broadcasted_iota(I32, (tq, tq), 0)
            qry = lax.broadcasted_iota(I32, (tq, tq), 1)
            st = [jnp.where(key <= qry, s, NEG_INF) for s in st]
        m_new = [jnp.maximum(m, jnp.max(s, axis=0, keepdims=True)) for m, s in zip(ms, st)]
        pt = [jnp.exp(s - mn).astype(BF16) for s, mn in zip(st, m_new)]
        pv = [jnp.dot(v, p, preferred_element_type=F32) for v, p in zip(vts, pt)]
        new_acc = [jnp.exp(m - mn) * a + d for m, mn, a, d in zip(ms, m_new, accs, pv)]
        for hh in range(hp):
            acc_scr[hh] = new_acc[hh]
            m_scr[hh] = m_new[hh]

    def body(kj, carry):
        block(kj, False)
        return carry

    lax.fori_loop(0, qi, body, 0)
    block(qi, True)
    for hh in range(hp):
        acc = acc_scr[hh]
        ot = acc[:HEAD_DIM, :] / acc[HEAD_DIM:HEAD_DIM + 1, :]
        ot = ot * lax.rsqrt(jnp.mean(ot * ot, axis=0, keepdims=True) + EPS)
        o_ref[:, hh * HEAD_DIM:(hh + 1) * HEAD_DIM] = (ot.T * hg_ref[hh]).astype(o_ref.dtype)


def _fox_prompt(z, c, hg, *, batch, seq, tq, hp):
    nq = seq // tq
    ngrp = N_HEADS // hp
    w = hp * HEAD_DIM
    ext = 16
    return pl.pallas_call(
        functools.partial(_fox_prompt_body, tq=tq, hp=hp),
        grid=(batch, ngrp, nq),
        in_specs=[
            pl.BlockSpec((tq, w), lambda b, h, i: (b * nq + i, h)),
            pl.BlockSpec((seq, w), lambda b, h, i: (b, ngrp + h)),
            pl.BlockSpec((seq, w), lambda b, h, i: (b, 2 * ngrp + h)),
            pl.BlockSpec((tq, HEAD_DIM), lambda b, h, i: (b * nq + i, 0)),
            pl.BlockSpec((seq, HEAD_DIM), lambda b, h, i: (b, 0)),
            pl.BlockSpec((hp, 1, HEAD_DIM), lambda b, h, i: (h, 0, 0)),
        ],
        out_specs=pl.BlockSpec((tq, w), lambda b, h, i: (b * nq + i, h)),
        out_shape=jax.ShapeDtypeStruct((batch * seq, HEAD_W), BF16),
        scratch_shapes=[pltpu.VMEM((hp, seq, 2 * HEAD_DIM), BF16),
                        pltpu.VMEM((hp, nq, HEAD_DIM + ext, tq), BF16),
                        pltpu.VMEM((hp, 1, tq), F32), pltpu.VMEM((hp, HEAD_DIM + ext, tq), F32)],
        compiler_params=_params("arbitrary", "arbitrary", "arbitrary"),
        name="fox_prompt",
    )(z, z, z, c, c, hg)


def _fox_sample_body(pt_ref, q_ref, kn_ref, vn_ref, lfr_ref, lfc_ref, hg_ref, ck_hbm, cv_hbm, *rest,
                     layer, pps, page, ring):
    lp = rest[:pps]
    o_ref, m_scr, l_scr, acc_scr, carry_scr, negq_scr, kbuf, vbuf, sem = rest[pps:]
    j = pl.program_id(1)
    n_j = pl.num_programs(1)
    rows = q_ref.shape[0]
    dsq = rows // N_HEADS
    cols = page * N_HEADS
    n_pages = n_j * pps

    g = pl.program_id(0) * n_j + j
    n_g = pl.num_programs(0) * n_j

    def fetch(gg):
        slot = gg % ring
        seq = gg // n_j
        first = n_pages - 1 - (gg % n_j) * pps
        for p in range(pps):
            pg = pt_ref[seq, first - p]
            pltpu.make_async_copy(ck_hbm.at[layer, pg], kbuf.at[slot, p], sem.at[slot]).start()
            pltpu.make_async_copy(cv_hbm.at[layer, pg], vbuf.at[slot, p], sem.at[slot]).start()

    @pl.when(g == 0)
    def _():
        for ahead in range(ring - 1):
            @pl.when(ahead < n_g)
            def _():
                fetch(ahead)

    @pl.when(g + ring - 1 < n_g)
    def _():
        fetch(g + ring - 1)

    slot = g % ring
    pltpu.make_async_copy(ck_hbm.at[layer, pl.ds(0, pps)], kbuf.at[slot], sem.at[slot]).wait()
    pltpu.make_async_copy(cv_hbm.at[layer, pl.ds(0, pps)], vbuf.at[slot], sem.at[slot]).wait()

    rn = lax.broadcasted_iota(I32, (rows, rows), 0)
    cn = lax.broadcasted_iota(I32, (rows, rows), 1)
    keep_new = ((rn // dsq) == (cn % N_HEADS)) & ((cn // N_HEADS) <= (rn % dsq))
    cq_col = jnp.sum(jnp.where(keep_new, lfr_ref[...], 0.0), axis=1, keepdims=True)

    @pl.when(j == 0)
    def _():
        m_scr[...] = jnp.full(m_scr.shape, NEG_INF, F32)
        l_scr[...] = jnp.zeros(l_scr.shape, F32)
        acc_scr[...] = jnp.zeros(acc_scr.shape, F32)
        carry_scr[...] = jnp.zeros(carry_scr.shape, F32)
        rp = lax.broadcasted_iota(I32, (rows, cols), 0)
        cp = lax.broadcasted_iota(I32, (rows, cols), 1)
        negq_scr[...] = jnp.where((rp // dsq) == (cp % N_HEADS), cq_col, NEG_INF)

    def update(scores, values):
        m, l, acc = m_scr[...], l_scr[...], acc_scr[...]
        mx = functools.reduce(jnp.maximum, scores)
        m_new = jnp.maximum(m, jnp.max(mx, axis=1, keepdims=True))
        alpha = jnp.exp(m - m_new)
        ps = [jnp.exp(s - m_new) for s in scores]
        pv = [jnp.dot(p.astype(BF16), v, preferred_element_type=F32) for p, v in zip(ps, values)]
        l_scr[...] = alpha * l + jnp.sum(functools.reduce(jnp.add, ps), axis=1, keepdims=True)
        acc_scr[...] = alpha * acc + functools.reduce(jnp.add, pv)
        m_scr[...] = m_new

    q = q_ref[...]
    raw = [_nt_dot(q, kbuf[slot, p].reshape(cols, HEAD_DIM).astype(BF16)) for p in range(pps)]

    lpp = jnp.concatenate([lp[p][...] for p in range(pps)], axis=0)
    lane = lax.broadcasted_iota(I32, (pps, cols), 1)
    sub = lax.broadcasted_iota(I32, (pps, cols), 0)
    n_steps = (page - 1).bit_length()
    tot = lpp
    for t in range(n_steps):
        tot = tot + pltpu.roll(tot, N_HEADS << t, axis=1)
    suf = jnp.where(lane < cols - N_HEADS, pltpu.roll(lpp, cols - N_HEADS, axis=1), 0.0)
    for t in range(n_steps):
        sh = N_HEADS << t
        suf = suf + jnp.where(lane < cols - sh, pltpu.roll(suf, cols - sh, axis=1), 0.0)
    pre = jnp.where(sub >= 1, pltpu.roll(tot, 1, axis=0), 0.0)
    for t in range((pps - 1).bit_length()):
        sh = 1 << t
        pre = pre + jnp.where(sub >= sh, pltpu.roll(pre, sh, axis=0), 0.0)
    carry = carry_scr[...]
    bias = suf + pre + carry

    negq = negq_scr[...]
    v2 = [vbuf[slot, p].reshape(cols, HEAD_DIM).astype(BF16) for p in range(pps)]
    update([raw[p] + negq + bias[p:p + 1, :] for p in range(pps)], v2)
    carry_scr[...] = carry + jnp.sum(tot, axis=0, keepdims=True)

    @pl.when(j == pl.num_programs(1) - 1)
    def _():
        cn_row = jnp.sum(jnp.where(((rn % N_HEADS) == (cn % N_HEADS)) & ((rn // N_HEADS) <= (cn // N_HEADS)),
                                   lfc_ref[...], 0.0), axis=0, keepdims=True)
        s = jnp.where(keep_new, _nt_dot(q, kn_ref[...].astype(BF16)) + cq_col - cn_row, NEG_INF)
        update([s], [vn_ref[...].astype(BF16)])
        o = acc_scr[...] / l_scr[...]
        o_ref[...] = (_rms(o) * hg_ref[...]).astype(o_ref.dtype)


def _fox_sample(layer, page_table, q, kn, vn, lfr, lfc, hg_rows, cache_k, cache_v, cache_lf2, *, pps):
    DB, n_pages = page_table.shape
    rows = q.shape[1]
    page = cache_k.shape[2]
    cols = page * N_HEADS
    n_steps = n_pages // pps

    def seq_map(b, j, pt):
        return (b, 0, 0)

    def lf_map(p):
        def f(b, j, pt):
            return (layer, pt[b, n_pages - 1 - (j * pps + p)], 0, 0)
        return f

    ring = 3
    grid_spec = pltpu.PrefetchScalarGridSpec(
        num_scalar_prefetch=1,
        grid=(DB, n_steps),
        in_specs=[
            pl.BlockSpec((None, rows, HEAD_DIM), seq_map),
            pl.BlockSpec((None, rows, HEAD_DIM), seq_map),
            pl.BlockSpec((None, rows, HEAD_DIM), seq_map),
            pl.BlockSpec((None, 1, rows), seq_map),
            pl.BlockSpec((None, rows, 1), seq_map),
            pl.BlockSpec((rows, HEAD_DIM), lambda b, j, pt: (0, 0)),
            pl.BlockSpec(memory_space=pl.ANY),
            pl.BlockSpec(memory_space=pl.ANY),
        ] + [pl.BlockSpec((None, None, 1, cols), lf_map(p)) for p in range(pps)],
        out_specs=pl.BlockSpec((None, rows, HEAD_DIM), seq_map),
        scratch_shapes=[pltpu.VMEM((rows, 1), F32), pltpu.VMEM((rows, 1), F32),
                        pltpu.VMEM((rows, HEAD_DIM), F32), pltpu.VMEM((1, cols), F32),
                        pltpu.VMEM((rows, cols), F32),
                        pltpu.VMEM((ring, pps, page, N_HEADS, HEAD_DIM), F32),
                        pltpu.VMEM((ring, pps, page, N_HEADS, HEAD_DIM), F32),
                        pltpu.SemaphoreType.DMA((ring,))],
    )
    return pl.pallas_call(
        functools.partial(_fox_sample_body, layer=layer, pps=pps, page=page, ring=ring),
        grid_spec=grid_spec,
        out_shape=jax.ShapeDtypeStruct((DB, rows, HEAD_DIM), BF16),
        compiler_params=_params("arbitrary", "arbitrary"),
        name="fox_sample",
    )(page_table, q, kn, vn, lfr, lfc, hg_rows, cache_k, cache_v, *([cache_lf2] * pps))


def _merge_body(attp_ref, atts_ref, u_ref, vm_ref, wcp_ref, bcp_ref, wcs_ref, bcs_ref, hg_ref, wo_ref,
                x_ref, g_ref, *rest, n_prompt_tiles, n_exp):
    if n_exp:
        rw_ref, xo_ref, ho_ref, eid_ref, gate_ref, cnt_ref, att_scr, mlp_scr = rest
    else:
        xo_ref, ho_ref, att_scr, mlp_scr = rest
    i = pl.program_id(0)
    tm = x_ref.shape[0]

    def gate_heads(wc_ref, bc_ref):
        chunk = wc_ref.shape[1]
        for h in range(N_HEADS):
            sl = slice(h * HEAD_DIM, (h + 1) * HEAD_DIM)
            for c in range(tm // chunk):
                rows = slice(c * chunk, (c + 1) * chunk)
                s = jnp.dot(wc_ref[h], vm_ref[rows, sl].astype(BF16), preferred_element_type=F32) + bc_ref[h]
                o = u_ref[rows, sl] * s
                mlp_scr[rows, sl] = (_rms(o) * hg_ref[:, sl]).astype(BF16)

    @pl.when(i < n_prompt_tiles)
    def _():
        att_scr[...] = attp_ref[...]
        gate_heads(wcp_ref, bcp_ref)

    @pl.when(i >= n_prompt_tiles)
    def _():
        att_scr[...] = atts_ref[...]
        gate_heads(wcs_ref, bcs_ref)

    y = jnp.dot(att_scr[...], wo_ref[:HEAD_W, :], preferred_element_type=F32)
    y = y + jnp.dot(mlp_scr[...], wo_ref[HEAD_W:, :], preferred_element_type=F32)
    xn = x_ref[...] + y
    xo_ref[...] = xn
    hn = _rms(xn) * g_ref[...]
    ho_ref[...] = hn.astype(ho_ref.dtype)
    if n_exp:
        _route_tile(hn.astype(BF16), rw_ref, eid_ref, gate_ref, cnt_ref, n_exp)


def _merge(att_p, att_s, z, wc_p, bc_p, wc_s, bc_s, hg_mlp, wo, x, g, *, h_dtype, route_w=None, n_exp=0):
    T, D = x.shape
    tm = ROW_TILE
    n_prompt_tiles = att_p.shape[0] // tm
    assert att_s.shape[0] == tm and wc_s.shape[1] == tm
    chunk = wc_p.shape[1]
    row = lambda i: (i, 0)
    c2 = lambda i: (0, 0)
    c3 = lambda i: (0, 0, 0)
    route_in = [] if route_w is None else [pl.BlockSpec((D, HEAD_DIM), c2)]
    route_args = [] if route_w is None else [route_w]
    route_out = [] if route_w is None else [pl.BlockSpec((tm, HEAD_DIM), row), pl.BlockSpec((tm, HEAD_DIM), row),
                                            pl.BlockSpec((1, HEAD_DIM), c2)]
    route_shape = [] if route_w is None else [jax.ShapeDtypeStruct((T, HEAD_DIM), I32),
                                              jax.ShapeDtypeStruct((T, HEAD_DIM), F32),
                                              jax.ShapeDtypeStruct((1, HEAD_DIM), F32)]
    return pl.pallas_call(
        functools.partial(_merge_body, n_prompt_tiles=n_prompt_tiles, n_exp=n_exp if route_w is not None else 0),
        grid=(T // tm,),
        in_specs=[
            pl.BlockSpec((tm, HEAD_W), lambda i: (jnp.minimum(i, n_prompt_tiles - 1), 0)),
            pl.BlockSpec((tm, HEAD_W), c2),
            pl.BlockSpec((tm, HEAD_W), lambda i: (i, 3)),
            pl.BlockSpec((tm, HEAD_W), lambda i: (i, 4)),
            pl.BlockSpec((N_HEADS, chunk, chunk), c3),
            pl.BlockSpec((N_HEADS, chunk, HEAD_DIM), c3),
            pl.BlockSpec((N_HEADS, tm, tm), c3),
            pl.BlockSpec((N_HEADS, tm, HEAD_DIM), c3),
            pl.BlockSpec((1, HEAD_W), c2),
            pl.BlockSpec((2 * HEAD_W, D), c2),
            pl.BlockSpec((tm, D), row),
            pl.BlockSpec((1, D), c2),
        ] + route_in,
        out_specs=[pl.BlockSpec((tm, D), row), pl.BlockSpec((tm, D), row)] + route_out,
        out_shape=[jax.ShapeDtypeStruct((T, D), F32), jax.ShapeDtypeStruct((T, D), h_dtype)] + route_shape,
        scratch_shapes=[pltpu.VMEM((tm, HEAD_W), BF16), pltpu.VMEM((tm, HEAD_W), BF16)],
        compiler_params=_params("arbitrary"),
        name="mlp_merge_out",
    )(att_p, att_s, z, z, wc_p, bc_p, wc_s, bc_s, hg_mlp, wo, x, g, *route_args)


def _dense_body(h_ref, wg_ref, wu_ref, wd_ref, *rest, has_tail):
    if has_tail:
        wgt_ref, wut_ref, wdt_ref, x_ref, o_ref = rest
    else:
        x_ref, o_ref = rest
    f = pl.program_id(1)

    @pl.when(f == 0)
    def _():
        o_ref[...] = x_ref[...]

    n_chunks = h_ref.shape[0] // ROW_TILE

    def swiglu(wg, wu, wd):
        wgv, wuv, wdv = (ref[...].astype(BF16) for ref in (wg, wu, wd))
        hs = [h_ref[n * ROW_TILE:(n + 1) * ROW_TILE, :] for n in range(n_chunks)]
        ab = [(jnp.dot(h, wgv, preferred_element_type=F32), jnp.dot(h, wuv, preferred_element_type=F32)) for h in hs]
        gs = [(a * jax.nn.sigmoid(a) * b).astype(BF16) for a, b in ab]
        ds = [jnp.dot(g, wdv, preferred_element_type=F32) for g in gs]
        for n, d in enumerate(ds):
            o_ref[n * ROW_TILE:(n + 1) * ROW_TILE, :] += d

    swiglu(wg_ref, wu_ref, wd_ref)

    if has_tail:
        @pl.when(f == pl.num_programs(1) - 1)
        def _():
            swiglu(wgt_ref, wut_ref, wdt_ref)


def _dense_ffn(h, wg, wu, wd, x, *, tm, tf):
    T, D = x.shape
    d_ff = wg.shape[1]
    n_main = d_ff // tf
    tail = d_ff - n_main * tf
    assert tail % HEAD_DIM == 0 and (n_main * tf) % max(tail, 1) == 0
    in_specs = [
        pl.BlockSpec((tm, D), lambda i, f: (i, 0)),
        pl.BlockSpec((D, tf), lambda i, f: (0, f)),
        pl.BlockSpec((D, tf), lambda i, f: (0, f)),
        pl.BlockSpec((tf, D), lambda i, f: (f, 0)),
    ]
    args = [h, wg, wu, wd]
    if tail:
        t_idx = n_main * tf // tail
        in_specs += [pl.BlockSpec((D, tail), lambda i, f: (0, t_idx)),
                     pl.BlockSpec((D, tail), lambda i, f: (0, t_idx)),
                     pl.BlockSpec((tail, D), lambda i, f: (t_idx, 0))]
        args += [wg, wu, wd]
    in_specs.append(pl.BlockSpec((tm, D), lambda i, f: (i, 0)))
    args.append(x)
    return pl.pallas_call(
        functools.partial(_dense_body, has_tail=bool(tail)),
        grid=(T // tm, n_main),
        in_specs=in_specs,
        out_specs=pl.BlockSpec((tm, D), lambda i, f: (i, 0)),
        out_shape=jax.ShapeDtypeStruct((T, D), F32),
        compiler_params=_params("arbitrary", "arbitrary"),
        name="dense_ffn",
    )(*args)


def _route_tile(hb, w_ref, eid_ref, gate_ref, cnt_ref, n_exp):
    tm = hb.shape[0]
    logits = jnp.dot(hb, w_ref[...], preferred_element_type=F32)
    lane = lax.broadcasted_iota(I32, logits.shape, 1)
    big = logits.shape[1]
    l1 = jnp.where(lane < n_exp, logits, NEG_INF)
    m1 = jnp.max(l1, axis=1, keepdims=True)
    i1 = jnp.min(jnp.where(l1 == m1, lane, big), axis=1, keepdims=True)
    l2 = jnp.where(lane == i1, NEG_INF, l1)
    m2 = jnp.max(l2, axis=1, keepdims=True)
    i2 = jnp.min(jnp.where(l2 == m2, lane, big), axis=1, keepdims=True)
    e2 = jnp.exp(m2 - m1)
    p1 = 1.0 / (1.0 + e2)
    p2 = e2 / (1.0 + e2)
    gate_ref[...] = jnp.where(lane == 0, p1, jnp.where(lane == 1, p2, 0.0))

    @pl.when(pl.program_id(0) == 0)
    def _():
        cnt_ref[...] = jnp.zeros(cnt_ref.shape, F32)

    hit1 = (lane == i1).astype(F32)
    hit2 = (lane == i2).astype(F32)
    r = lax.broadcasted_iota(I32, (tm, tm), 0)
    c = lax.broadcasted_iota(I32, (tm, tm), 1)
    before = jnp.dot((c < r).astype(BF16), (hit1 + hit2).astype(BF16), preferred_element_type=F32)
    before = before + cnt_ref[...]
    rank1 = jnp.sum(hit1 * before, axis=1, keepdims=True).astype(I32)
    rank2 = jnp.sum(hit2 * before, axis=1, keepdims=True).astype(I32)
    cnt_ref[...] += jnp.sum(hit1 + hit2, axis=0, keepdims=True)
    eid_ref[...] = jnp.where(lane == 0, i1, jnp.where(lane == 1, i2,
                             jnp.where(lane == TOP_K, rank1, jnp.where(lane == TOP_K + 1, rank2, 0))))


def _row_copy(src_hbm, row, dst_vmem, r, sem):
    return pltpu.make_async_copy(src_hbm.at[pl.ds(row, 1)], dst_vmem.at[pl.ds(r, 1)], sem)


def _moe_body(ie_ref, ifirst_ref, inb_ref, cnt_ref, src_ref,
              h_hbm, wg_ref, wu_ref, wd_ref, y_hbm,
              xbuf, acc, stage, sem_in, sem_out, *, sub, unroll):
    del ie_ref
    w = pl.program_id(0)
    f = pl.program_id(1)
    n_f = pl.num_programs(1)
    n_items, n_used = cnt_ref[0], cnt_ref[1]
    n_total = y_hbm.shape[0] // sub

    def block_rows(r):
        return pl.ds(pl.multiple_of(r * sub, sub), sub)

    def out_copy(r_local, r_global):
        return pltpu.make_async_copy(acc.at[block_rows(r_local)], y_hbm.at[block_rows(r_global)], sem_out)

    def drain_out(n):
        def wait(r, carry):
            out_copy(r, r).wait()
            return carry

        lax.fori_loop(0, n, wait, 0)

    @pl.when((w == 0) & (f == 0))
    def _():
        acc[block_rows(0)] = jnp.zeros((sub, acc.shape[1]), F32)

        def fill(r, carry):
            out_copy(0, r).start()
            return carry

        def drain(r, carry):
            out_copy(0, r).wait()
            return carry

        lax.fori_loop(n_used, n_total, fill, 0)
        lax.fori_loop(n_used, n_total, drain, 0)

    @pl.when(w < n_items)
    def _():
        first = ifirst_ref[w]
        nb = inb_ref[w]

        @pl.when(f == 0)
        def _():
            def issue(blk, slot):
                base = (first + blk) * sub

                def body(rr, carry):
                    for u in range(unroll):
                        r = rr * unroll + u
                        _row_copy(h_hbm, src_ref[base + r], stage.at[slot], r, sem_in.at[slot]).start(priority=u % 2)
                    return carry

                lax.fori_loop(0, sub // unroll, body, 0)

            issue(0, 0)

            @pl.when(w > 0)
            def _():
                drain_out(inb_ref[w - 1])

            def per_block(blk, carry):
                slot = blk % 2

                @pl.when(blk + 1 < nb)
                def _():
                    issue(blk + 1, 1 - slot)

                pltpu.make_async_copy(h_hbm.at[pl.ds(0, sub)], stage.at[slot], sem_in.at[slot]).wait()
                xbuf[block_rows(blk)] = stage[slot].astype(BF16)
                acc[block_rows(blk)] = jnp.zeros((sub, acc.shape[1]), F32)
                return carry

            lax.fori_loop(0, nb, per_block, 0)

        def swiglu(chunks):
            spans = [pl.ds(start if isinstance(start, int) else pl.multiple_of(start, sub), size)
                     for start, size in chunks]
            xs = [xbuf[s] for s in spans]
            wgv, wuv, wdv = (ref[...].astype(BF16) for ref in (wg_ref, wu_ref, wd_ref))
            ab = [(jnp.dot(x, wgv, preferred_element_type=F32),
                   jnp.dot(x, wuv, preferred_element_type=F32)) for x in xs]
            hs = [(a * jax.nn.sigmoid(a) * b).astype(BF16) for a, b in ab]
            ds = [jnp.dot(hh, wdv, preferred_element_type=F32) for hh in hs]
            for s, d in zip(spans, ds):
                acc[s] += d

        wide = 2 * sub
        n_wide = nb // 2

        for rp in range(xbuf.shape[0] // (2 * wide)):
            @pl.when(rp < n_wide // 2)
            def _():
                swiglu([(2 * rp * wide, wide), ((2 * rp + 1) * wide, wide)])

        @pl.when(n_wide % 2 == 1)
        def _():
            swiglu([((n_wide - 1) * wide, wide)])

        @pl.when(nb % 2 == 1)
        def _():
            swiglu([((nb - 1) * sub, sub)])

        @pl.when(f == n_f - 1)
        def _():
            def start(r, carry):
                out_copy(r, first + r).start()
                return carry

            lax.fori_loop(0, nb, start, 0)

            @pl.when(w == n_items - 1)
            def _():
                drain_out(nb)


def _moe_ffn(tables, h, wg, wu, wd, *, tf, max_blocks):
    item_expert, item_first, item_nb, counts, src, n_rows = tables
    T, D = h.shape
    sub = MOE_SUB
    n_f = wg.shape[2] // tf
    last = n_f - 1

    def live(w, cnt):
        return jnp.minimum(w, cnt[0] - 1)

    def fidx(w, f, cnt):
        return jnp.where(w < cnt[0], f, last)

    grid_spec = pltpu.PrefetchScalarGridSpec(
        num_scalar_prefetch=5,
        grid=(item_expert.shape[0], n_f),
        in_specs=[
            pl.BlockSpec(memory_space=pl.ANY),
            pl.BlockSpec((None, D, tf), lambda w, f, ie, i1, i2, cnt, s: (ie[live(w, cnt)], 0, fidx(w, f, cnt))),
            pl.BlockSpec((None, D, tf), lambda w, f, ie, i1, i2, cnt, s: (ie[live(w, cnt)], 0, fidx(w, f, cnt))),
            pl.BlockSpec((None, tf, D), lambda w, f, ie, i1, i2, cnt, s: (ie[live(w, cnt)], fidx(w, f, cnt), 0)),
        ],
        out_specs=pl.BlockSpec(memory_space=pl.ANY),
        scratch_shapes=[
            pltpu.VMEM((max_blocks * sub, D), BF16),
            pltpu.VMEM((max_blocks * sub, D), F32),
            pltpu.VMEM((2, sub, D), F32),
            pltpu.SemaphoreType.DMA((2,)), pltpu.SemaphoreType.DMA(()),
        ],
    )
    return pl.pallas_call(
        functools.partial(_moe_body, sub=sub, unroll=8),
        grid_spec=grid_spec,
        out_shape=jax.ShapeDtypeStruct((n_rows, D), F32),
        compiler_params=_params("arbitrary", "arbitrary"),
        name="moe_ffn",
    )(item_expert, item_first, item_nb, counts, src, h, wg, wu, wd)


def _combine_body(pos_ref, x_ref, gate_ref, y_hbm, *rest, tm, unroll, split_tile):
    o_refs, (buf, sem) = rest[:-2], rest[-2:]
    i = pl.program_id(0)
    slot = i % 2

    def gather(tile, to_slot):
        base = tile * tm * TOP_K

        def issue(rr, carry):
            for u in range(unroll):
                r = rr * unroll + u
                for k in range(TOP_K):
                    _row_copy(y_hbm, pos_ref[base + r * TOP_K + k], buf.at[to_slot, k], r,
                              sem.at[to_slot, k]).start(priority=k)
            return carry

        lax.fori_loop(0, tm // unroll, issue, 0)

    @pl.when(i == 0)
    def _():
        gather(0, 0)

    @pl.when(i + 1 < pl.num_programs(0))
    def _():
        gather(i + 1, 1 - slot)

    out = x_ref[...]
    for k in range(TOP_K):
        pltpu.make_async_copy(y_hbm.at[pl.ds(0, tm)], buf.at[slot, k], sem.at[slot, k]).wait()
        out = out + gate_ref[:, k:k + 1] * buf[slot, k]
    if split_tile is None:
        o_refs[0][...] = out
    else:
        @pl.when(pl.program_id(0) < split_tile)
        def _():
            o_refs[0][...] = out

        @pl.when(pl.program_id(0) >= split_tile)
        def _():
            o_refs[1][...] = out


def _combine(pos, x, gate, y, *, tm, split_rows=None):
    T, D = x.shape
    if split_rows is None:
        split_tile = None
        out_specs = [pl.BlockSpec((tm, D), lambda i, p: (i, 0))]
        out_shape = [jax.ShapeDtypeStruct((T, D), F32)]
    else:
        split_tile = split_rows // tm
        out_specs = [pl.BlockSpec((tm, D), lambda i, p: (jnp.minimum(i, split_tile - 1), 0)),
                     pl.BlockSpec((tm, D), lambda i, p: (jnp.maximum(i - split_tile, 0), 0))]
        out_shape = [jax.ShapeDtypeStruct((split_rows, D), F32), jax.ShapeDtypeStruct((T - split_rows, D), F32)]
    grid_spec = pltpu.PrefetchScalarGridSpec(
        num_scalar_prefetch=1,
        grid=(T // tm,),
        in_specs=[pl.BlockSpec((tm, D), lambda i, p: (i, 0)),
                  pl.BlockSpec((tm, HEAD_DIM), lambda i, p: (i, 0)),
                  pl.BlockSpec(memory_space=pl.ANY)],
        out_specs=out_specs,
        scratch_shapes=[pltpu.VMEM((2, TOP_K, tm, D), F32), pltpu.SemaphoreType.DMA((2, TOP_K))],
    )
    return pl.pallas_call(
        functools.partial(_combine_body, tm=tm, unroll=4, split_tile=split_tile),
        grid_spec=grid_spec,
        out_shape=out_shape,
        compiler_params=_params("arbitrary"),
        name="combine_rows",
    )(pos, x, gate, y)


def _kv_pack_body(*refs, n_layers):
    z_refs, (ko_ref, vo_ref) = refs[:2 * n_layers], refs[2 * n_layers:]
    layer = pl.program_id(0)
    tm = z_refs[0].shape[0]
    for l in range(n_layers):
        @pl.when(layer == l)
        def _():
            for src, dst in ((z_refs[2 * l], ko_ref), (z_refs[2 * l + 1], vo_ref)):
                for h in range(N_HEADS):
                    dst[pl.ds(h, tm, stride=N_HEADS), :] = src[:, h * HEAD_DIM:(h + 1) * HEAD_DIM]


def _kv_pack(zs, n_rows, *, tm):
    n_layers = len(zs)
    n_tiles = n_rows // tm
    in_specs, args = [], []
    for l, z in enumerate(zs):
        for section in (1, 2):
            in_specs.append(pl.BlockSpec(
                (tm, HEAD_W),
                lambda lay, i, l=l, section=section: (jnp.where(lay == l, i, jnp.where(lay < l, 0, n_tiles - 1)), section)))
            args.append(z)
    out_spec = pl.BlockSpec((None, tm * N_HEADS, HEAD_DIM), lambda lay, i: (lay, i, 0))
    shape = jax.ShapeDtypeStruct((n_layers, n_rows * N_HEADS, HEAD_DIM), F32)
    return pl.pallas_call(
        functools.partial(_kv_pack_body, n_layers=n_layers),
        grid=(n_layers, n_tiles),
        in_specs=in_specs,
        out_specs=[out_spec, out_spec],
        out_shape=[shape, shape],
        compiler_params=_params("arbitrary", "arbitrary"),
        name="kv_pack",
    )(*args)


def _route_tables(eid, rank, counts, sub, max_blocks):
    flat = eid.reshape(-1)
    rank = rank.reshape(-1)
    n_assign = flat.shape[0]
    n_exp = counts.shape[0]
    nsub = (counts + sub - 1) // sub
    sub_end = jnp.cumsum(nsub)
    sub_start = sub_end - nsub
    pos = (sub_start[flat] * sub + rank).astype(I32)
    n_rows = (n_assign // sub + n_exp) * sub
    src = jnp.zeros((n_rows,), I32).at[pos].set(jnp.arange(n_assign, dtype=I32) // TOP_K)
    items = (nsub + max_blocks - 1) // max_blocks
    item_end = jnp.cumsum(items)
    item_start = item_end - items
    n_items_max = (n_rows // sub + n_exp * (max_blocks - 1)) // max_blocks
    w = jnp.arange(n_items_max, dtype=I32)
    item_expert = jnp.minimum(jnp.sum((item_end[None, :] <= w[:, None]).astype(I32), axis=1), n_exp - 1)
    k = w - item_start[item_expert]
    item_first = sub_start[item_expert] + k * max_blocks
    item_nb = jnp.clip(nsub[item_expert] - k * max_blocks, 0, max_blocks)
    counts2 = jnp.stack([item_end[-1], sub_end[-1]]).astype(I32)
    return pos, (item_expert.astype(I32), item_first.astype(I32), item_nb.astype(I32), counts2, src, n_rows)


def _row_tile_multiple(n, cap):
    best = ROW_TILE
    for m in range(ROW_TILE, cap + 1, ROW_TILE):
        if n % m == 0:
            best = m
    return best


def kernel(x_prompt, x_sample, cache_k, cache_v, cache_logf, page_table, norm_mix_g, w_in, b_f,
           q_norm_g, k_norm_g, sgu_norm_g, w_spatial, b_spatial, head_norm_g, w_out, norm_ffn_g,
           dense_w_gate, dense_w_up, dense_w_down, router_w, moe_w_gate, moe_w_up, moe_w_down):
    B, S, D = x_prompt.shape
    DB, DS, _ = x_sample.shape
    depth = w_in.shape[0]
    n_phys, page = cache_k.shape[1], cache_k.shape[2]
    chunk = w_spatial.shape[-1]
    n_exp = router_w.shape[-1]
    assert cache_k.shape[3] == N_HEADS and cache_k.shape[4] == HEAD_DIM
    assert w_in.shape[2] == N_SECTIONS * HEAD_W + N_HEADS and D == 2 * HEAD_W
    Tp, Ts = B * S, DB * DS
    T = Tp + Ts
    assert Ts == ROW_TILE and S % ROW_TILE == 0 and ROW_TILE % chunk == 0
    rows = N_HEADS * DS
    scale = HEAD_DIM ** -0.5
    f_lo = 3 * HEAD_W

    x = jnp.concatenate([x_prompt.reshape(Tp, D), x_sample.reshape(Ts, D)], axis=0)
    cache_lf2 = cache_logf.reshape(depth, n_phys, 1, page * N_HEADS)
    ones = jnp.ones((HEAD_W,), F32)

    outs = {name: [] for name in ("z", "fp", "ks", "vs", "fs", "ms")}
    for l in range(depth):
        w_main = jnp.concatenate([w_in[l][:, :f_lo], w_in[l][:, f_lo + N_HEADS:]], axis=1).astype(BF16)
        wf = jnp.pad(w_in[l][:, f_lo:f_lo + N_HEADS], ((0, 0), (0, HEAD_DIM - N_HEADS))).astype(BF16)
        bf = jnp.pad(b_f[l], (0, HEAD_DIM - N_HEADS)).reshape(1, HEAD_DIM)
        g_mix = norm_mix_g[l].reshape(1, D)
        gains = jnp.stack([(q_norm_g[l] * scale).reshape(-1), k_norm_g[l].reshape(-1), ones, ones,
                           sgu_norm_g[l].reshape(-1)]).reshape(N_SECTIONS, 1, HEAD_W)
        hg_att = head_norm_g[l][:N_HEADS]
        hg_mlp = head_norm_g[l][N_HEADS:].reshape(1, HEAD_W)
        wo = w_out[l].astype(BF16)
        g_ffn = norm_ffn_g[l].reshape(1, D)
        wm = jnp.tril(w_spatial[l])
        wc_p = wm.astype(BF16)
        bc_p = jnp.broadcast_to(b_spatial[l][:, :, None], (N_HEADS, chunk, HEAD_DIM))
        tok = jnp.arange(Ts, dtype=I32)
        pick = (tok[:, None] % DS == jnp.arange(DS, dtype=I32)[None, :]).astype(F32)
        same_seq = (tok[:, None] // DS == tok[None, :] // DS).astype(F32)
        wc_s = (jnp.einsum("ri,hij,cj->hrc", pick, wm[:, :DS, :DS], pick, precision=lax.Precision.HIGHEST)
                * same_seq).astype(BF16)
        bc_s = jnp.broadcast_to(jnp.tile(b_spatial[l][:, :DS], (1, DB))[:, :, None], (N_HEADS, Ts, HEAD_DIM))

        z, lf, c = _project(x, g_mix, w_main, wf, bf, gains, tm=_row_tile_multiple(T, 3 * ROW_TILE), seq=S)

        att_p = _fox_prompt(z, c, hg_att.reshape(N_HEADS, 1, HEAD_DIM), batch=B, seq=S, tq=ROW_TILE, hp=8)

        zs = z[Tp:]
        q_rows = zs[:, :HEAD_W].astype(BF16).reshape(DB, DS, N_HEADS, HEAD_DIM).transpose(0, 2, 1, 3).reshape(DB, rows, HEAD_DIM)
        kn = zs[:, HEAD_W:2 * HEAD_W].reshape(DB, rows, HEAD_DIM)
        vn = zs[:, 2 * HEAD_W:3 * HEAD_W].reshape(DB, rows, HEAD_DIM)
        lfs8 = lf[Tp:, :N_HEADS]
        att_rows = _fox_sample(l, page_table, q_rows, kn, vn, lfs8.reshape(DB, 1, rows), lfs8.reshape(DB, rows, 1),
                               jnp.repeat(hg_att, DS, axis=0), cache_k, cache_v, cache_lf2, pps=8)
        att_s = att_rows.reshape(DB, N_HEADS, DS, HEAD_DIM).transpose(0, 2, 1, 3).reshape(Ts, HEAD_W)

        moe_layer = l % 2 == 1
        i = l // 2
        if moe_layer:
            rw = jnp.pad(router_w[i], ((0, 0), (0, HEAD_DIM - n_exp))).astype(BF16)
            x, h, eid, gate, counts = _merge(att_p, att_s, z, wc_p, bc_p, wc_s, bc_s, hg_mlp, wo, x, g_ffn,
                                             h_dtype=F32, route_w=rw, n_exp=n_exp)
        else:
            x, h = _merge(att_p, att_s, z, wc_p, bc_p, wc_s, bc_s, hg_mlp, wo, x, g_ffn, h_dtype=BF16)

        if not moe_layer:
            x = _dense_ffn(h, dense_w_gate[i], dense_w_up[i], dense_w_down[i], x,
                           tm=_row_tile_multiple(T, 3 * ROW_TILE), tf=256)
        else:
            pos, tables = _route_tables(eid[:, :TOP_K], eid[:, TOP_K:2 * TOP_K], counts[0, :n_exp].astype(I32),
                                        MOE_SUB, MOE_ITEM_BLOCKS)
            y_sorted = _moe_ffn(tables, h, moe_w_gate[i], moe_w_up[i], moe_w_down[i], tf=256,
                                max_blocks=MOE_ITEM_BLOCKS)
            if l == depth - 1:
                x_split = _combine(pos, x, gate, y_sorted, tm=ROW_TILE, split_rows=Tp)
            else:
                x = _combine(pos, x, gate, y_sorted, tm=ROW_TILE)[0]

        outs["z"].append(z)
        outs["fp"].append(lf[:Tp, :N_HEADS].reshape(B, S, N_HEADS))
        outs["ks"].append(kn.reshape(DB, DS, N_HEADS, HEAD_DIM))
        outs["vs"].append(vn.reshape(DB, DS, N_HEADS, HEAD_DIM))
        outs["fs"].append(lfs8.reshape(DB, DS, N_HEADS))
        outs["ms"].append(zs[:, 4 * HEAD_W:].reshape(DB, DS, N_HEADS, HEAD_DIM))

    k_prompt, v_prompt = _kv_pack(outs["z"], Tp, tm=min(2 * ROW_TILE, S))
    if depth % 2 == 0:
        y_p, y_s = x_split
    else:
        y_p, y_s = x[:Tp], x[Tp:]
    return (y_p.reshape(B, S, D), y_s.reshape(DB, DS, D),
            k_prompt.reshape(depth, B, S, N_HEADS, HEAD_DIM), v_prompt.reshape(depth, B, S, N_HEADS, HEAD_DIM),
            jnp.stack(outs["fp"]),
            jnp.stack(outs["ks"]), jnp.stack(outs["vs"]), jnp.stack(outs["fs"]), jnp.stack(outs["ms"]))
```

```python
import functools

import jax
import jax.numpy as jnp
from jax import lax
from jax.experimental import pallas as pl
from jax.experimental.pallas import tpu as pltpu

F32 = jnp.float32
BF16 = jnp.bfloat16
I32 = jnp.int32

HEAD_DIM = 128
N_HEADS = 8
HEAD_W = N_HEADS * HEAD_DIM
N_SECTIONS = 5
EPS = 1e-6
NEG_INF = float("-inf")
TOP_K = 2
ROW_TILE = 256
MOE_SUB = ROW_TILE // 2
MOE_ITEM_BLOCKS = 20
VMEM_LIMIT = 56 * 1024 * 1024


def _params(*sem):
    return pltpu.CompilerParams(dimension_semantics=sem, vmem_limit_bytes=VMEM_LIMIT)


def _nt_dot(a, b):
    return lax.dot_general(a, b, (((1,), (1,)), ((), ())), preferred_element_type=F32)


def _rms(x):
    return x * lax.rsqrt(jnp.mean(x * x, axis=-1, keepdims=True) + EPS)


def _split3(x):
    x1 = x.astype(BF16).astype(F32)
    r1 = x - x1
    x2 = r1.astype(BF16).astype(F32)
    x3 = (r1 - x2).astype(BF16).astype(F32)
    return x1, x2, x3


def _pick_lane(x, lane_idx):
    lane = lax.broadcasted_iota(I32, x.shape, 1)
    return jnp.sum(jnp.where(lane == lane_idx, x, 0.0), axis=1, keepdims=True)


def _proj_body(x_ref, g_ref, w_ref, wf_ref, bf_ref, gain_ref, z_ref, lf_ref, c_ref, h_scr, carry_scr,
               *, chunk, chunks_per_seq):
    i = pl.program_id(0)
    j = pl.program_id(1)
    n_chunks = x_ref.shape[0] // chunk

    def section(jj):
        first = jj == 0
        normed = jj in (0, 1, 4)
        w = w_ref[...]
        if first:
            r = lax.broadcasted_iota(I32, (chunk, chunk), 0)
            c = lax.broadcasted_iota(I32, (chunk, chunk), 1)
            lower = (c <= r).astype(BF16)

            @pl.when(i == 0)
            def _():
                carry_scr[...] = jnp.zeros(carry_scr.shape, F32)

            carry = carry_scr[...]
        for n in range(n_chunks):
            rows = slice(n * chunk, (n + 1) * chunk)
            if first:
                hb = (_rms(x_ref[rows, :]) * g_ref[...]).astype(BF16)
                h_scr[rows, :] = hb
                f = jnp.dot(hb, wf_ref[...], preferred_element_type=F32) + bf_ref[...]
                lf = jnp.minimum(f, 0.0) - jnp.log1p(jnp.exp(-jnp.abs(f)))
                lf_ref[rows, :] = lf
                first_of_seq = (i * n_chunks + n) % chunks_per_seq == 0
                cs = jnp.where(first_of_seq, 0.0, carry)
                for part in _split3(lf):
                    cs = cs + jnp.dot(lower, part.astype(BF16), preferred_element_type=F32)
                c_ref[rows, :] = cs
                carry = cs[chunk - 1:chunk, :]
            else:
                hb = h_scr[rows, :]
            z = jnp.dot(hb, w, preferred_element_type=F32)
            if normed:
                for h in range(N_HEADS):
                    sl = slice(h * HEAD_DIM, (h + 1) * HEAD_DIM)
                    z_ref[rows, sl] = _rms(z[:, sl]) * gain_ref[:, sl]
            else:
                z_ref[rows, :] = z
        if first:
            carry_scr[...] = carry

    for jj in range(N_SECTIONS):
        pl.when(j == jj)(functools.partial(section, jj))


def _project(x, g, w_main, wf, bf, gains, *, tm, seq):
    T, D = x.shape
    row = lambda i, j: (i, 0)
    const = lambda i, j: (0, 0)
    return pl.pallas_call(
        functools.partial(_proj_body, chunk=ROW_TILE, chunks_per_seq=seq // ROW_TILE),
        grid=(T // tm, N_SECTIONS),
        in_specs=[
            pl.BlockSpec((tm, D), row),
            pl.BlockSpec((1, D), const),
            pl.BlockSpec((D, HEAD_W), lambda i, j: (0, j)),
            pl.BlockSpec((D, HEAD_DIM), const),
            pl.BlockSpec((1, HEAD_DIM), const),
            pl.BlockSpec((None, 1, HEAD_W), lambda i, j: (j, 0, 0)),
        ],
        out_specs=[pl.BlockSpec((tm, HEAD_W), lambda i, j: (i, j)),
                   pl.BlockSpec((tm, HEAD_DIM), row),
                   pl.BlockSpec((tm, HEAD_DIM), row)],
        out_shape=[
            jax.ShapeDtypeStruct((T, N_SECTIONS * HEAD_W), F32),
            jax.ShapeDtypeStruct((T, HEAD_DIM), F32),
            jax.ShapeDtypeStruct((T, HEAD_DIM), F32),
        ],
        scratch_shapes=[pltpu.VMEM((tm, D), BF16), pltpu.VMEM((1, HEAD_DIM), F32)],
        compiler_params=_params("arbitrary", "arbitrary"),
        name="in_proj",
    )(x, g, w_main, wf, bf, gains)


def _bias_lanes(c_col, key_side):
    lane = lax.broadcasted_iota(I32, (1, HEAD_DIM), 1)
    c1, c2, c3 = _split3(c_col)
    if key_side:
        terms = (1.0, 1.0, 1.0, -c1, -c2, -c3)
    else:
        terms = (c1, c2, c3, 1.0, 1.0, 1.0)
    out = jnp.zeros((c_col.shape[0], HEAD_DIM), F32)
    for idx, t in enumerate(terms):
        out = jnp.where(lane == idx, t, out)
    return out.astype(BF16)


def _fox_prompt_body(q_ref, k_ref, v_ref, cq_ref, ck_ref, hg_ref, o_ref, ka, vt, m_scr, acc_scr, *, tq, hp):
    hgrp = pl.program_id(1)
    qi = pl.program_id(2)
    n_kv = vt.shape[1]
    ext = vt.shape[2] - HEAD_DIM

    @pl.when(qi == 0)
    def _():
        row = lax.broadcasted_iota(I32, (ext, tq), 0)
        ones_ext = jnp.where(row == 0, 1.0, 0.0).astype(BF16)
        for hh in range(hp):
            sl = slice(hh * HEAD_DIM, (hh + 1) * HEAD_DIM)
            ka[hh, :, :HEAD_DIM] = k_ref[:, sl].astype(BF16)
            ka[hh, :, HEAD_DIM:] = _bias_lanes(_pick_lane(ck_ref[...], hgrp * hp + hh), True)
            for n in range(n_kv):
                vt[hh, n, :HEAD_DIM, :] = v_ref[n * tq:(n + 1) * tq, sl].T.astype(BF16)
                vt[hh, n, HEAD_DIM:, :] = ones_ext

    qa = []
    for hh in range(hp):
        sl = slice(hh * HEAD_DIM, (hh + 1) * HEAD_DIM)
        bias = _bias_lanes(_pick_lane(cq_ref[...], hgrp * hp + hh), False)
        qa.append(jnp.concatenate([q_ref[:, sl].astype(BF16), bias], axis=1))
        m_scr[hh] = jnp.full((1, tq), NEG_INF, F32)
        acc_scr[hh] = jnp.zeros(acc_scr.shape[1:], F32)

    def block(kj, diagonal):
        start = pl.multiple_of(kj * tq, tq)
        st = [_nt_dot(ka[hh, pl.ds(start, tq), :], qa[hh]) for hh in range(hp)]
        vts = [vt[hh, kj] for hh in range(hp)]
        ms = [m_scr[hh] for hh in range(hp)]
        accs = [acc_scr[hh] for hh in range(hp)]
        if diagonal:
            key = lax.broadcasted_iota(I32, (tq, tq), 0)
            qry = lax.broadcasted_iota(I32, (tq, tq), 1)
            st = [jnp.where(key <= qry, s, NEG_INF) for s in st]
        m_new = [jnp.maximum(m, jnp.max(s, axis=0, keepdims=True)) for m, s in zip(ms, st)]
        pt = [jnp.exp(s - mn).astype(BF16) for s, mn in zip(st, m_new)]
        pv = [jnp.dot(v, p, preferred_element_type=F32) for v, p in zip(vts, pt)]
        new_acc = [jnp.exp(m - mn) * a + d for m, mn, a, d in zip(ms, m_new, accs, pv)]
        for hh in range(hp):
            acc_scr[hh] = new_acc[hh]
            m_scr[hh] = m_new[hh]

    def body(kj, carry):
        block(kj, False)
        return carry

    lax.fori_loop(0, qi, body, 0)
    block(qi, True)
    for hh in range(hp):
        acc = acc_scr[hh]
        ot = acc[:HEAD_DIM, :] / acc[HEAD_DIM:HEAD_DIM + 1, :]
        ot = ot * lax.rsqrt(jnp.mean(ot * ot, axis=0, keepdims=True) + EPS)
        o_ref[:, hh * HEAD_DIM:(hh + 1) * HEAD_DIM] = (ot.T * hg_ref[hh]).astype(o_ref.dtype)


def _fox_prompt(z, c, hg, *, batch, seq, tq, hp):
    nq = seq // tq
    ngrp = N_HEADS // hp
    w = hp * HEAD_DIM
    ext = 16
    return pl.pallas_call(
        functools.partial(_fox_prompt_body, tq=tq, hp=hp),
        grid=(batch, ngrp, nq),
        in_specs=[
            pl.BlockSpec((tq, w), lambda b, h, i: (b * nq + i, h)),
            pl.BlockSpec((seq, w), lambda b, h, i: (b, ngrp + h)),
            pl.BlockSpec((seq, w), lambda b, h, i: (b, 2 * ngrp + h)),
            pl.BlockSpec((tq, HEAD_DIM), lambda b, h, i: (b * nq + i, 0)),
            pl.BlockSpec((seq, HEAD_DIM), lambda b, h, i: (b, 0)),
            pl.BlockSpec((hp, 1, HEAD_DIM), lambda b, h, i: (h, 0, 0)),
        ],
        out_specs=pl.BlockSpec((tq, w), lambda b, h, i: (b * nq + i, h)),
        out_shape=jax.ShapeDtypeStruct((batch * seq, HEAD_W), BF16),
        scratch_shapes=[pltpu.VMEM((hp, seq, 2 * HEAD_DIM), BF16),
                        pltpu.VMEM((hp, nq, HEAD_DIM + ext, tq), BF16),
                        pltpu.VMEM((hp, 1, tq), F32), pltpu.VMEM((hp, HEAD_DIM + ext, tq), F32)],
        compiler_params=_params("arbitrary", "arbitrary", "arbitrary"),
        name="fox_prompt",
    )(z, z, z, c, c, hg)


def _fox_sample_body(pt_ref, q_ref, kn_ref, vn_ref, lfr_ref, lfc_ref, hg_ref, ck_hbm, cv_hbm, *rest,
                     layer, pps, page, ring):
    lp = rest[:pps]
    o_ref, m_scr, l_scr, acc_scr, carry_scr, negq_scr, kbuf, vbuf, sem = rest[pps:]
    j = pl.program_id(1)
    n_j = pl.num_programs(1)
    rows = q_ref.shape[0]
    dsq = rows // N_HEADS
    cols = page * N_HEADS
    n_pages = n_j * pps

    g = pl.program_id(0) * n_j + j
    n_g = pl.num_programs(0) * n_j

    def fetch(gg):
        slot = gg % ring
        seq = gg // n_j
        first = n_pages - 1 - (gg % n_j) * pps
        for p in range(pps):
            pg = pt_ref[seq, first - p]
            pltpu.make_async_copy(ck_hbm.at[layer, pg], kbuf.at[slot, p], sem.at[slot]).start()
            pltpu.make_async_copy(cv_hbm.at[layer, pg], vbuf.at[slot, p], sem.at[slot]).start()

    @pl.when(g == 0)
    def _():
        for ahead in range(ring - 1):
            @pl.when(ahead < n_g)
            def _():
                fetch(ahead)

    @pl.when(g + ring - 1 < n_g)
    def _():
        fetch(g + ring - 1)

    slot = g % ring
    pltpu.make_async_copy(ck_hbm.at[layer, pl.ds(0, pps)], kbuf.at[slot], sem.at[slot]).wait()
    pltpu.make_async_copy(cv_hbm.at[layer, pl.ds(0, pps)], vbuf.at[slot], sem.at[slot]).wait()

    rn = lax.broadcasted_iota(I32, (rows, rows), 0)
    cn = lax.broadcasted_iota(I32, (rows, rows), 1)
    keep_new = ((rn // dsq) == (cn % N_HEADS)) & ((cn // N_HEADS) <= (rn % dsq))
    cq_col = jnp.sum(jnp.where(keep_new, lfr_ref[...], 0.0), axis=1, keepdims=True)

    @pl.when(j == 0)
    def _():
        m_scr[...] = jnp.full(m_scr.shape, NEG_INF, F32)
        l_scr[...] = jnp.zeros(l_scr.shape, F32)
        acc_scr[...] = jnp.zeros(acc_scr.shape, F32)
        carry_scr[...] = jnp.zeros(carry_scr.shape, F32)
        rp = lax.broadcasted_iota(I32, (rows, cols), 0)
        cp = lax.broadcasted_iota(I32, (rows, cols), 1)
        negq_scr[...] = jnp.where((rp // dsq) == (cp % N_HEADS), cq_col, NEG_INF)

    def update(scores, values):
        m, l, acc = m_scr[...], l_scr[...], acc_scr[...]
        mx = functools.reduce(jnp.maximum, scores)
        m_new = jnp.maximum(m, jnp.max(mx, axis=1, keepdims=True))
        alpha = jnp.exp(m - m_new)
        ps = [jnp.exp(s - m_new) for s in scores]
        pv = [jnp.dot(p.astype(BF16), v, preferred_element_type=F32) for p, v in zip(ps, values)]
        l_scr[...] = alpha * l + jnp.sum(functools.reduce(jnp.add, ps), axis=1, keepdims=True)
        acc_scr[...] = alpha * acc + functools.reduce(jnp.add, pv)
        m_scr[...] = m_new

    q = q_ref[...]
    raw = [_nt_dot(q, kbuf[slot, p].reshape(cols, HEAD_DIM).astype(BF16)) for p in range(pps)]

    lpp = jnp.concatenate([lp[p][...] for p in range(pps)], axis=0)
    lane = lax.broadcasted_iota(I32, (pps, cols), 1)
    sub = lax.broadcasted_iota(I32, (pps, cols), 0)
    n_steps = (page - 1).bit_length()
    tot = lpp
    for t in range(n_steps):
        tot = tot + pltpu.roll(tot, N_HEADS << t, axis=1)
    suf = jnp.where(lane < cols - N_HEADS, pltpu.roll(lpp, cols - N_HEADS, axis=1), 0.0)
    for t in range(n_steps):
        sh = N_HEADS << t
        suf = suf + jnp.where(lane < cols - sh, pltpu.roll(suf, cols - sh, axis=1), 0.0)
    pre = jnp.where(sub >= 1, pltpu.roll(tot, 1, axis=0), 0.0)
    for t in range((pps - 1).bit_length()):
        sh = 1 << t
        pre = pre + jnp.where(sub >= sh, pltpu.roll(pre, sh, axis=0), 0.0)
    carry = carry_scr[...]
    bias = suf + pre + carry

    negq = negq_scr[...]
    v2 = [vbuf[slot, p].reshape(cols, HEAD_DIM).astype(BF16) for p in range(pps)]
    update([raw[p] + negq + bias[p:p + 1, :] for p in range(pps)], v2)
    carry_scr[...] = carry + jnp.sum(tot, axis=0, keepdims=True)

    @pl.when(j == pl.num_programs(1) - 1)
    def _():
        cn_row = jnp.sum(jnp.where(((rn % N_HEADS) == (cn % N_HEADS)) & ((rn // N_HEADS) <= (cn // N_HEADS)),
                                   lfc_ref[...], 0.0), axis=0, keepdims=True)
        s = jnp.where(keep_new, _nt_dot(q, kn_ref[...].astype(BF16)) + cq_col - cn_row, NEG_INF)
        update([s], [vn_ref[...].astype(BF16)])
        o = acc_scr[...] / l_scr[...]
        o_ref[...] = (_rms(o) * hg_ref[...]).astype(o_ref.dtype)


def _fox_sample(layer, page_table, q, kn, vn, lfr, lfc, hg_rows, cache_k, cache_v, cache_lf2, *, pps):
    DB, n_pages = page_table.shape
    rows = q.shape[1]
    page = cache_k.shape[2]
    cols = page * N_HEADS
    n_steps = n_pages // pps

    def seq_map(b, j, pt):
        return (b, 0, 0)

    def lf_map(p):
        def f(b, j, pt):
            return (layer, pt[b, n_pages - 1 - (j * pps + p)], 0, 0)
        return f

    ring = 3
    grid_spec = pltpu.PrefetchScalarGridSpec(
        num_scalar_prefetch=1,
        grid=(DB, n_steps),
        in_specs=[
            pl.BlockSpec((None, rows, HEAD_DIM), seq_map),
            pl.BlockSpec((None, rows, HEAD_DIM), seq_map),
            pl.BlockSpec((None, rows, HEAD_DIM), seq_map),
            pl.BlockSpec((None, 1, rows), seq_map),
            pl.BlockSpec((None, rows, 1), seq_map),
            pl.BlockSpec((rows, HEAD_DIM), lambda b, j, pt: (0, 0)),
            pl.BlockSpec(memory_space=pl.ANY),
            pl.BlockSpec(memory_space=pl.ANY),
        ] + [pl.BlockSpec((None, None, 1, cols), lf_map(p)) for p in range(pps)],
        out_specs=pl.BlockSpec((None, rows, HEAD_DIM), seq_map),
        scratch_shapes=[pltpu.VMEM((rows, 1), F32), pltpu.VMEM((rows, 1), F32),
                        pltpu.VMEM((rows, HEAD_DIM), F32), pltpu.VMEM((1, cols), F32),
                        pltpu.VMEM((rows, cols), F32),
                        pltpu.VMEM((ring, pps, page, N_HEADS, HEAD_DIM), F32),
                        pltpu.VMEM((ring, pps, page, N_HEADS, HEAD_DIM), F32),
                        pltpu.SemaphoreType.DMA((ring,))],
    )
    return pl.pallas_call(
        functools.partial(_fox_sample_body, layer=layer, pps=pps, page=page, ring=ring),
        grid_spec=grid_spec,
        out_shape=jax.ShapeDtypeStruct((DB, rows, HEAD_DIM), BF16),
        compiler_params=_params("arbitrary", "arbitrary"),
        name="fox_sample",
    )(page_table, q, kn, vn, lfr, lfc, hg_rows, cache_k, cache_v, *([cache_lf2] * pps))


def _merge_body(attp_ref, atts_ref, u_ref, vm_ref, wcp_ref, bcp_ref, wcs_ref, bcs_ref, hg_ref, wo_ref,
                x_ref, g_ref, *rest, n_prompt_tiles, n_exp):
    if n_exp:
        rw_ref, xo_ref, ho_ref, eid_ref, gate_ref, cnt_ref, att_scr, mlp_scr = rest
    else:
        xo_ref, ho_ref, att_scr, mlp_scr = rest
    i = pl.program_id(0)
    tm = x_ref.shape[0]

    def gate_heads(wc_ref, bc_ref):
        chunk = wc_ref.shape[1]
        for h in range(N_HEADS):
            sl = slice(h * HEAD_DIM, (h + 1) * HEAD_DIM)
            for c in range(tm // chunk):
                rows = slice(c * chunk, (c + 1) * chunk)
                s = jnp.dot(wc_ref[h], vm_ref[rows, sl].astype(BF16), preferred_element_type=F32) + bc_ref[h]
                o = u_ref[rows, sl] * s
                mlp_scr[rows, sl] = (_rms(o) * hg_ref[:, sl]).astype(BF16)

    @pl.when(i < n_prompt_tiles)
    def _():
        att_scr[...] = attp_ref[...]
        gate_heads(wcp_ref, bcp_ref)

    @pl.when(i >= n_prompt_tiles)
    def _():
        att_scr[...] = atts_ref[...]
        gate_heads(wcs_ref, bcs_ref)

    y = jnp.dot(att_scr[...], wo_ref[:HEAD_W, :], preferred_element_type=F32)
    y = y + jnp.dot(mlp_scr[...], wo_ref[HEAD_W:, :], preferred_element_type=F32)
    xn = x_ref[...] + y
    xo_ref[...] = xn
    hn = _rms(xn) * g_ref[...]
    if n_exp:
        n_lt = hn.shape[1] // HEAD_DIM
        for c in range(n_lt):
            ho_ref[pl.ds(c, tm, stride=n_lt), :] = hn[:, c * HEAD_DIM:(c + 1) * HEAD_DIM]
        _route_tile(hn.astype(BF16), rw_ref, eid_ref, gate_ref, cnt_ref, n_exp)
    else:
        ho_ref[...] = hn.astype(ho_ref.dtype)


def _merge(att_p, att_s, z, wc_p, bc_p, wc_s, bc_s, hg_mlp, wo, x, g, *, h_dtype, route_w=None, n_exp=0):
    T, D = x.shape
    tm = ROW_TILE
    n_prompt_tiles = att_p.shape[0] // tm
    assert att_s.shape[0] == tm and wc_s.shape[1] == tm
    chunk = wc_p.shape[1]
    row = lambda i: (i, 0)
    c2 = lambda i: (0, 0)
    c3 = lambda i: (0, 0, 0)
    if route_w is None:
        h_spec, h_shape = pl.BlockSpec((tm, D), row), jax.ShapeDtypeStruct((T, D), h_dtype)
    else:
        n_lt = D // HEAD_DIM
        h_spec, h_shape = pl.BlockSpec((tm * n_lt, HEAD_DIM), row), jax.ShapeDtypeStruct((T * n_lt, HEAD_DIM), F32)
    route_in = [] if route_w is None else [pl.BlockSpec((D, HEAD_DIM), c2)]
    route_args = [] if route_w is None else [route_w]
    route_out = [] if route_w is None else [pl.BlockSpec((tm, HEAD_DIM), row), pl.BlockSpec((tm, HEAD_DIM), row),
                                            pl.BlockSpec((1, HEAD_DIM), c2)]
    route_shape = [] if route_w is None else [jax.ShapeDtypeStruct((T, HEAD_DIM), I32),
                                              jax.ShapeDtypeStruct((T, HEAD_DIM), F32),
                                              jax.ShapeDtypeStruct((1, HEAD_DIM), F32)]
    return pl.pallas_call(
        functools.partial(_merge_body, n_prompt_tiles=n_prompt_tiles, n_exp=n_exp if route_w is not None else 0),
        grid=(T // tm,),
        in_specs=[
            pl.BlockSpec((tm, HEAD_W), lambda i: (jnp.minimum(i, n_prompt_tiles - 1), 0)),
            pl.BlockSpec((tm, HEAD_W), c2),
            pl.BlockSpec((tm, HEAD_W), lambda i: (i, 3)),
            pl.BlockSpec((tm, HEAD_W), lambda i: (i, 4)),
            pl.BlockSpec((N_HEADS, chunk, chunk), c3),
            pl.BlockSpec((N_HEADS, chunk, HEAD_DIM), c3),
            pl.BlockSpec((N_HEADS, tm, tm), c3),
            pl.BlockSpec((N_HEADS, tm, HEAD_DIM), c3),
            pl.BlockSpec((1, HEAD_W), c2),
            pl.BlockSpec((2 * HEAD_W, D), c2),
            pl.BlockSpec((tm, D), row),
            pl.BlockSpec((1, D), c2),
        ] + route_in,
        out_specs=[pl.BlockSpec((tm, D), row), h_spec] + route_out,
        out_shape=[jax.ShapeDtypeStruct((T, D), F32), h_shape] + route_shape,
        scratch_shapes=[pltpu.VMEM((tm, HEAD_W), BF16), pltpu.VMEM((tm, HEAD_W), BF16)],
        compiler_params=_params("arbitrary"),
        name="mlp_merge_out",
    )(att_p, att_s, z, z, wc_p, bc_p, wc_s, bc_s, hg_mlp, wo, x, g, *route_args)


def _dense_body(h_ref, wg_ref, wu_ref, wd_ref, *rest, has_tail):
    if has_tail:
        wgt_ref, wut_ref, wdt_ref, x_ref, o_ref = rest
    else:
        x_ref, o_ref = rest
    f = pl.program_id(1)

    @pl.when(f == 0)
    def _():
        o_ref[...] = x_ref[...]

    n_chunks = h_ref.shape[0] // ROW_TILE

    def swiglu(wg, wu, wd):
        wgv, wuv, wdv = (ref[...].astype(BF16) for ref in (wg, wu, wd))
        hs = [h_ref[n * ROW_TILE:(n + 1) * ROW_TILE, :] for n in range(n_chunks)]
        ab = [(jnp.dot(h, wgv, preferred_element_type=F32), jnp.dot(h, wuv, preferred_element_type=F32)) for h in hs]
        gs = [(a * jax.nn.sigmoid(a) * b).astype(BF16) for a, b in ab]
        ds = [jnp.dot(g, wdv, preferred_element_type=F32) for g in gs]
        for n, d in enumerate(ds):
            o_ref[n * ROW_TILE:(n + 1) * ROW_TILE, :] += d

    swiglu(wg_ref, wu_ref, wd_ref)

    if has_tail:
        @pl.when(f == pl.num_programs(1) - 1)
        def _():
            swiglu(wgt_ref, wut_ref, wdt_ref)


def _dense_ffn(h, wg, wu, wd, x, *, tm, tf):
    T, D = x.shape
    d_ff = wg.shape[1]
    n_main = d_ff // tf
    tail = d_ff - n_main * tf
    assert tail % HEAD_DIM == 0 and (n_main * tf) % max(tail, 1) == 0
    in_specs = [
        pl.BlockSpec((tm, D), lambda i, f: (i, 0)),
        pl.BlockSpec((D, tf), lambda i, f: (0, f)),
        pl.BlockSpec((D, tf), lambda i, f: (0, f)),
        pl.BlockSpec((tf, D), lambda i, f: (f, 0)),
    ]
    args = [h, wg, wu, wd]
    if tail:
        t_idx = n_main * tf // tail
        in_specs += [pl.BlockSpec((D, tail), lambda i, f: (0, t_idx)),
                     pl.BlockSpec((D, tail), lambda i, f: (0, t_idx)),
                     pl.BlockSpec((tail, D), lambda i, f: (t_idx, 0))]
        args += [wg, wu, wd]
    in_specs.append(pl.BlockSpec((tm, D), lambda i, f: (i, 0)))
    args.append(x)
    return pl.pallas_call(
        functools.partial(_dense_body, has_tail=bool(tail)),
        grid=(T // tm, n_main),
        in_specs=in_specs,
        out_specs=pl.BlockSpec((tm, D), lambda i, f: (i, 0)),
        out_shape=jax.ShapeDtypeStruct((T, D), F32),
        compiler_params=_params("arbitrary", "arbitrary"),
        name="dense_ffn",
    )(*args)


def _route_tile(hb, w_ref, eid_ref, gate_ref, cnt_ref, n_exp):
    tm = hb.shape[0]
    logits = jnp.dot(hb, w_ref[...], preferred_element_type=F32)
    lane = lax.broadcasted_iota(I32, logits.shape, 1)
    big = logits.shape[1]
    l1 = jnp.where(lane < n_exp, logits, NEG_INF)
    m1 = jnp.max(l1, axis=1, keepdims=True)
    i1 = jnp.min(jnp.where(l1 == m1, lane, big), axis=1, keepdims=True)
    l2 = jnp.where(lane == i1, NEG_INF, l1)
    m2 = jnp.max(l2, axis=1, keepdims=True)
    i2 = jnp.min(jnp.where(l2 == m2, lane, big), axis=1, keepdims=True)
    e2 = jnp.exp(m2 - m1)
    p1 = 1.0 / (1.0 + e2)
    p2 = e2 / (1.0 + e2)
    gate_ref[...] = jnp.where(lane == 0, p1, jnp.where(lane == 1, p2, 0.0))

    @pl.when(pl.program_id(0) == 0)
    def _():
        cnt_ref[...] = jnp.zeros(cnt_ref.shape, F32)

    hit1 = (lane == i1).astype(F32)
    hit2 = (lane == i2).astype(F32)
    r = lax.broadcasted_iota(I32, (tm, tm), 0)
    c = lax.broadcasted_iota(I32, (tm, tm), 1)
    before = jnp.dot((c < r).astype(BF16), (hit1 + hit2).astype(BF16), preferred_element_type=F32)
    before = before + cnt_ref[...]
    rank1 = jnp.sum(hit1 * before, axis=1, keepdims=True).astype(I32)
    rank2 = jnp.sum(hit2 * before, axis=1, keepdims=True).astype(I32)
    cnt_ref[...] += jnp.sum(hit1 + hit2, axis=0, keepdims=True)
    eid_ref[...] = jnp.where(lane == 0, i1, jnp.where(lane == 1, i2,
                             jnp.where(lane == TOP_K, rank1, jnp.where(lane == TOP_K + 1, rank2, 0))))


def _row_copy(src_hbm, row, dst_vmem, r, sem):
    return pltpu.make_async_copy(src_hbm.at[pl.ds(row, 1)], dst_vmem.at[pl.ds(r, 1)], sem)


def _moe_body(ie_ref, ifirst_ref, inb_ref, cnt_ref, src_ref,
              h_hbm, wg_ref, wu_ref, wd_ref, y_hbm,
              xbuf, acc, stage, sem_in, sem_out, *, sub, unroll):
    del ie_ref
    w = pl.program_id(0)
    f = pl.program_id(1)
    n_f = pl.num_programs(1)
    n_items, n_used = cnt_ref[0], cnt_ref[1]
    n_total = y_hbm.shape[0] // sub
    n_lt = xbuf.shape[1] // HEAD_DIM

    def block_rows(r):
        return pl.ds(pl.multiple_of(r * sub, sub), sub)

    def out_copy(r_local, r_global):
        return pltpu.make_async_copy(acc.at[block_rows(r_local)], y_hbm.at[block_rows(r_global)], sem_out)

    def drain_out(n):
        def wait(r, carry):
            out_copy(r, r).wait()
            return carry

        lax.fori_loop(0, n, wait, 0)

    @pl.when((w == 0) & (f == 0))
    def _():
        acc[block_rows(0)] = jnp.zeros((sub, acc.shape[1]), F32)

        def fill(r, carry):
            out_copy(0, r).start()
            return carry

        def drain(r, carry):
            out_copy(0, r).wait()
            return carry

        lax.fori_loop(n_used, n_total, fill, 0)
        lax.fori_loop(n_used, n_total, drain, 0)

    @pl.when(w < n_items)
    def _():
        first = ifirst_ref[w]
        nb = inb_ref[w]

        @pl.when(f == 0)
        def _():
            def issue(blk, slot):
                base = (first + blk) * sub

                def body(rr, carry):
                    for u in range(unroll):
                        r = rr * unroll + u
                        tok = pl.multiple_of(src_ref[base + r] * n_lt, n_lt)
                        pltpu.make_async_copy(h_hbm.at[pl.ds(tok, n_lt)], stage.at[slot, pl.ds(r * n_lt, n_lt)],
                                              sem_in.at[slot]).start(priority=u % 2)
                    return carry

                lax.fori_loop(0, sub // unroll, body, 0)

            issue(0, 0)

            @pl.when(w > 0)
            def _():
                drain_out(inb_ref[w - 1])

            def per_block(blk, carry):
                slot = blk % 2

                @pl.when(blk + 1 < nb)
                def _():
                    issue(blk + 1, 1 - slot)

                pltpu.make_async_copy(h_hbm.at[pl.ds(0, sub * n_lt)], stage.at[slot], sem_in.at[slot]).wait()
                for c in range(n_lt):
                    xbuf[block_rows(blk), c * HEAD_DIM:(c + 1) * HEAD_DIM] = (
                        stage.at[slot][pl.ds(c, sub, stride=n_lt), :].astype(BF16))
                acc[block_rows(blk)] = jnp.zeros((sub, acc.shape[1]), F32)
                return carry

            lax.fori_loop(0, nb, per_block, 0)

        def swiglu(chunks):
            spans = [pl.ds(start if isinstance(start, int) else pl.multiple_of(start, sub), size)
                     for start, size in chunks]
            xs = [xbuf[s] for s in spans]
            wgv, wuv, wdv = (ref[...].astype(BF16) for ref in (wg_ref, wu_ref, wd_ref))
            ab = [(jnp.dot(x, wgv, preferred_element_type=F32),
                   jnp.dot(x, wuv, preferred_element_type=F32)) for x in xs]
            hs = [(a * jax.nn.sigmoid(a) * b).astype(BF16) for a, b in ab]
            ds = [jnp.dot(hh, wdv, preferred_element_type=F32) for hh in hs]
            for s, d in zip(spans, ds):
                acc[s] += d

        wide = 2 * sub
        n_wide = nb // 2

        for rp in range(xbuf.shape[0] // (2 * wide)):
            @pl.when(rp < n_wide // 2)
            def _():
                swiglu([(2 * rp * wide, wide), ((2 * rp + 1) * wide, wide)])

        @pl.when(n_wide % 2 == 1)
        def _():
            swiglu([((n_wide - 1) * wide, wide)])

        @pl.when(nb % 2 == 1)
        def _():
            swiglu([((nb - 1) * sub, sub)])

        @pl.when(f == n_f - 1)
        def _():
            def start(r, carry):
                out_copy(r, first + r).start()
                return carry

            lax.fori_loop(0, nb, start, 0)

            @pl.when(w == n_items - 1)
            def _():
                drain_out(nb)


def _moe_ffn(tables, h, wg, wu, wd, *, tf, max_blocks):
    item_expert, item_first, item_nb, counts, src, n_rows = tables
    D = wg.shape[1]
    assert h.shape[1] == HEAD_DIM
    sub = MOE_SUB
    n_f = wg.shape[2] // tf
    last = n_f - 1

    def live(w, cnt):
        return jnp.minimum(w, cnt[0] - 1)

    def fidx(w, f, cnt):
        return jnp.where(w < cnt[0], f, last)

    grid_spec = pltpu.PrefetchScalarGridSpec(
        num_scalar_prefetch=5,
        grid=(item_expert.shape[0], n_f),
        in_specs=[
            pl.BlockSpec(memory_space=pl.ANY),
            pl.BlockSpec((None, D, tf), lambda w, f, ie, i1, i2, cnt, s: (ie[live(w, cnt)], 0, fidx(w, f, cnt))),
            pl.BlockSpec((None, D, tf), lambda w, f, ie, i1, i2, cnt, s: (ie[live(w, cnt)], 0, fidx(w, f, cnt))),
            pl.BlockSpec((None, tf, D), lambda w, f, ie, i1, i2, cnt, s: (ie[live(w, cnt)], fidx(w, f, cnt), 0)),
        ],
        out_specs=pl.BlockSpec(memory_space=pl.ANY),
        scratch_shapes=[
            pltpu.VMEM((max_blocks * sub, D), BF16),
            pltpu.VMEM((max_blocks * sub, D), F32),
            pltpu.VMEM((2, sub * (D // HEAD_DIM), HEAD_DIM), F32),
            pltpu.SemaphoreType.DMA((2,)), pltpu.SemaphoreType.DMA(()),
        ],
    )
    return pl.pallas_call(
        functools.partial(_moe_body, sub=sub, unroll=8),
        grid_spec=grid_spec,
        out_shape=jax.ShapeDtypeStruct((n_rows, D), F32),
        compiler_params=_params("arbitrary", "arbitrary"),
        name="moe_ffn",
    )(item_expert, item_first, item_nb, counts, src, h, wg, wu, wd)


def _combine_body(pos_ref, x_ref, gate_ref, y_hbm, *rest, tm, unroll, split_tile):
    o_refs, (buf, sem) = rest[:-2], rest[-2:]
    i = pl.program_id(0)
    slot = i % 2

    def gather(tile, to_slot):
        base = tile * tm * TOP_K

        def issue(rr, carry):
            for u in range(unroll):
                r = rr * unroll + u
                for k in range(TOP_K):
                    _row_copy(y_hbm, pos_ref[base + r * TOP_K + k], buf.at[to_slot, k], r,
                              sem.at[to_slot, k]).start(priority=k)
            return carry

        lax.fori_loop(0, tm // unroll, issue, 0)

    @pl.when(i == 0)
    def _():
        gather(0, 0)

    @pl.when(i + 1 < pl.num_programs(0))
    def _():
        gather(i + 1, 1 - slot)

    out = x_ref[...]
    for k in range(TOP_K):
        pltpu.make_async_copy(y_hbm.at[pl.ds(0, tm)], buf.at[slot, k], sem.at[slot, k]).wait()
        out = out + gate_ref[:, k:k + 1] * buf[slot, k]
    if split_tile is None:
        o_refs[0][...] = out
    else:
        @pl.when(pl.program_id(0) < split_tile)
        def _():
            o_refs[0][...] = out

        @pl.when(pl.program_id(0) >= split_tile)
        def _():
            o_refs[1][...] = out


def _combine(pos, x, gate, y, *, tm, split_rows=None):
    T, D = x.shape
    if split_rows is None:
        split_tile = None
        out_specs = [pl.BlockSpec((tm, D), lambda i, p: (i, 0))]
        out_shape = [jax.ShapeDtypeStruct((T, D), F32)]
    else:
        split_tile = split_rows // tm
        out_specs = [pl.BlockSpec((tm, D), lambda i, p: (jnp.minimum(i, split_tile - 1), 0)),
                     pl.BlockSpec((tm, D), lambda i, p: (jnp.maximum(i - split_tile, 0), 0))]
        out_shape = [jax.ShapeDtypeStruct((split_rows, D), F32), jax.ShapeDtypeStruct((T - split_rows, D), F32)]
    grid_spec = pltpu.PrefetchScalarGridSpec(
        num_scalar_prefetch=1,
        grid=(T // tm,),
        in_specs=[pl.BlockSpec((tm, D), lambda i, p: (i, 0)),
                  pl.BlockSpec((tm, HEAD_DIM), lambda i, p: (i, 0)),
                  pl.BlockSpec(memory_space=pl.ANY)],
        out_specs=out_specs,
        scratch_shapes=[pltpu.VMEM((2, TOP_K, tm, D), F32), pltpu.SemaphoreType.DMA((2, TOP_K))],
    )
    return pl.pallas_call(
        functools.partial(_combine_body, tm=tm, unroll=4, split_tile=split_tile),
        grid_spec=grid_spec,
        out_shape=out_shape,
        compiler_params=_params("arbitrary"),
        name="combine_rows",
    )(pos, x, gate, y)


def _kv_pack_body(*refs, n_layers):
    z_refs, (ko_ref, vo_ref) = refs[:2 * n_layers], refs[2 * n_layers:]
    layer = pl.program_id(0)
    tm = z_refs[0].shape[0]
    for l in range(n_layers):
        @pl.when(layer == l)
        def _():
            for src, dst in ((z_refs[2 * l], ko_ref), (z_refs[2 * l + 1], vo_ref)):
                for h in range(N_HEADS):
                    dst[pl.ds(h, tm, stride=N_HEADS), :] = src[:, h * HEAD_DIM:(h + 1) * HEAD_DIM]


def _kv_pack(zs, n_rows, *, tm):
    n_layers = len(zs)
    n_tiles = n_rows // tm
    in_specs, args = [], []
    for l, z in enumerate(zs):
        for section in (1, 2):
            in_specs.append(pl.BlockSpec(
                (tm, HEAD_W),
                lambda lay, i, l=l, section=section: (jnp.where(lay == l, i, jnp.where(lay < l, 0, n_tiles - 1)), section)))
            args.append(z)
    out_spec = pl.BlockSpec((None, tm * N_HEADS, HEAD_DIM), lambda lay, i: (lay, i, 0))
    shape = jax.ShapeDtypeStruct((n_layers, n_rows * N_HEADS, HEAD_DIM), F32)
    return pl.pallas_call(
        functools.partial(_kv_pack_body, n_layers=n_layers),
        grid=(n_layers, n_tiles),
        in_specs=in_specs,
        out_specs=[out_spec, out_spec],
        out_shape=[shape, shape],
        compiler_params=_params("arbitrary", "arbitrary"),
        name="kv_pack",
    )(*args)


def _route_tables(eid, rank, counts, sub, max_blocks):
    flat = eid.reshape(-1)
    rank = rank.reshape(-1)
    n_assign = flat.shape[0]
    n_exp = counts.shape[0]
    nsub = (counts + sub - 1) // sub
    sub_end = jnp.cumsum(nsub)
    sub_start = sub_end - nsub
    pos = (sub_start[flat] * sub + rank).astype(I32)
    n_rows = (n_assign // sub + n_exp) * sub
    src = jnp.zeros((n_rows,), I32).at[pos].set(jnp.arange(n_assign, dtype=I32) // TOP_K)
    items = (nsub + max_blocks - 1) // max_blocks
    item_end = jnp.cumsum(items)
    item_start = item_end - items
    n_items_max = (n_rows // sub + n_exp * (max_blocks - 1)) // max_blocks
    w = jnp.arange(n_items_max, dtype=I32)
    item_expert = jnp.minimum(jnp.sum((item_end[None, :] <= w[:, None]).astype(I32), axis=1), n_exp - 1)
    k = w - item_start[item_expert]
    item_first = sub_start[item_expert] + k * max_blocks
    item_nb = jnp.clip(nsub[item_expert] - k * max_blocks, 0, max_blocks)
    counts2 = jnp.stack([item_end[-1], sub_end[-1]]).astype(I32)
    return pos, (item_expert.astype(I32), item_first.astype(I32), item_nb.astype(I32), counts2, src, n_rows)


def _row_tile_multiple(n, cap):
    best = ROW_TILE
    for m in range(ROW_TILE, cap + 1, ROW_TILE):
        if n % m == 0:
            best = m
    return best


def kernel(x_prompt, x_sample, cache_k, cache_v, cache_logf, page_table, norm_mix_g, w_in, b_f,
           q_norm_g, k_norm_g, sgu_norm_g, w_spatial, b_spatial, head_norm_g, w_out, norm_ffn_g,
           dense_w_gate, dense_w_up, dense_w_down, router_w, moe_w_gate, moe_w_up, moe_w_down):
    B, S, D = x_prompt.shape
    DB, DS, _ = x_sample.shape
    depth = w_in.shape[0]
    n_phys, page = cache_k.shape[1], cache_k.shape[2]
    chunk = w_spatial.shape[-1]
    n_exp = router_w.shape[-1]
    assert cache_k.shape[3] == N_HEADS and cache_k.shape[4] == HEAD_DIM
    assert w_in.shape[2] == N_SECTIONS * HEAD_W + N_HEADS and D == 2 * HEAD_W
    Tp, Ts = B * S, DB * DS
    T = Tp + Ts
    assert Ts == ROW_TILE and S % ROW_TILE == 0 and ROW_TILE % chunk == 0
    rows = N_HEADS * DS
    scale = HEAD_DIM ** -0.5
    f_lo = 3 * HEAD_W

    x = jnp.concatenate([x_prompt.reshape(Tp, D), x_sample.reshape(Ts, D)], axis=0)
    cache_lf2 = cache_logf.reshape(depth, n_phys, 1, page * N_HEADS)
    ones = jnp.ones((HEAD_W,), F32)

    outs = {name: [] for name in ("z", "fp", "ks", "vs", "fs", "ms")}
    for l in range(depth):
        w_main = jnp.concatenate([w_in[l][:, :f_lo], w_in[l][:, f_lo + N_HEADS:]], axis=1).astype(BF16)
        wf = jnp.pad(w_in[l][:, f_lo:f_lo + N_HEADS], ((0, 0), (0, HEAD_DIM - N_HEADS))).astype(BF16)
        bf = jnp.pad(b_f[l], (0, HEAD_DIM - N_HEADS)).reshape(1, HEAD_DIM)
        g_mix = norm_mix_g[l].reshape(1, D)
        gains = jnp.stack([(q_norm_g[l] * scale).reshape(-1), k_norm_g[l].reshape(-1), ones, ones,
                           sgu_norm_g[l].reshape(-1)]).reshape(N_SECTIONS, 1, HEAD_W)
        hg_att = head_norm_g[l][:N_HEADS]
        hg_mlp = head_norm_g[l][N_HEADS:].reshape(1, HEAD_W)
        wo = w_out[l].astype(BF16)
        g_ffn = norm_ffn_g[l].reshape(1, D)
        wm = jnp.tril(w_spatial[l])
        wc_p = wm.astype(BF16)
        bc_p = jnp.broadcast_to(b_spatial[l][:, :, None], (N_HEADS, chunk, HEAD_DIM))
        tok = jnp.arange(Ts, dtype=I32)
        pick = (tok[:, None] % DS == jnp.arange(DS, dtype=I32)[None, :]).astype(F32)
        same_seq = (tok[:, None] // DS == tok[None, :] // DS).astype(F32)
        wc_s = (jnp.einsum("ri,hij,cj->hrc", pick, wm[:, :DS, :DS], pick, precision=lax.Precision.HIGHEST)
                * same_seq).astype(BF16)
        bc_s = jnp.broadcast_to(jnp.tile(b_spatial[l][:, :DS], (1, DB))[:, :, None], (N_HEADS, Ts, HEAD_DIM))

        z, lf, c = _project(x, g_mix, w_main, wf, bf, gains, tm=_row_tile_multiple(T, 3 * ROW_TILE), seq=S)

        att_p = _fox_prompt(z, c, hg_att.reshape(N_HEADS, 1, HEAD_DIM), batch=B, seq=S, tq=ROW_TILE, hp=8)

        zs = z[Tp:]
        q_rows = zs[:, :HEAD_W].astype(BF16).reshape(DB, DS, N_HEADS, HEAD_DIM).transpose(0, 2, 1, 3).reshape(DB, rows, HEAD_DIM)
        kn = zs[:, HEAD_W:2 * HEAD_W].reshape(DB, rows, HEAD_DIM)
        vn = zs[:, 2 * HEAD_W:3 * HEAD_W].reshape(DB, rows, HEAD_DIM)
        lfs8 = lf[Tp:, :N_HEADS]
        att_rows = _fox_sample(l, page_table, q_rows, kn, vn, lfs8.reshape(DB, 1, rows), lfs8.reshape(DB, rows, 1),
                               jnp.repeat(hg_att, DS, axis=0), cache_k, cache_v, cache_lf2, pps=8)
        att_s = att_rows.reshape(DB, N_HEADS, DS, HEAD_DIM).transpose(0, 2, 1, 3).reshape(Ts, HEAD_W)

        moe_layer = l % 2 == 1
        i = l // 2
        if moe_layer:
            rw = jnp.pad(router_w[i], ((0, 0), (0, HEAD_DIM - n_exp))).astype(BF16)
            x, h, eid, gate, counts = _merge(att_p, att_s, z, wc_p, bc_p, wc_s, bc_s, hg_mlp, wo, x, g_ffn,
                                             h_dtype=F32, route_w=rw, n_exp=n_exp)
        else:
            x, h = _merge(att_p, att_s, z, wc_p, bc_p, wc_s, bc_s, hg_mlp, wo, x, g_ffn, h_dtype=BF16)

        if not moe_layer:
            x = _dense_ffn(h, dense_w_gate[i], dense_w_up[i], dense_w_down[i], x,
                           tm=_row_tile_multiple(T, 3 * ROW_TILE), tf=256)
        else:
            pos, tables = _route_tables(eid[:, :TOP_K], eid[:, TOP_K:2 * TOP_K], counts[0, :n_exp].astype(I32),
                                        MOE_SUB, MOE_ITEM_BLOCKS)
            y_sorted = _moe_ffn(tables, h, moe_w_gate[i], moe_w_up[i], moe_w_down[i], tf=256,
                                max_blocks=MOE_ITEM_BLOCKS)
            if l == depth - 1:
                x_split = _combine(pos, x, gate, y_sorted, tm=ROW_TILE, split_rows=Tp)
            else:
                x = _combine(pos, x, gate, y_sorted, tm=ROW_TILE)[0]

        outs["z"].append(z)
        outs["fp"].append(lf[:Tp, :N_HEADS].reshape(B, S, N_HEADS))
        outs["ks"].append(kn.reshape(DB, DS, N_HEADS, HEAD_DIM))
        outs["vs"].append(vn.reshape(DB, DS, N_HEADS, HEAD_DIM))
        outs["fs"].append(lfs8.reshape(DB, DS, N_HEADS))
        outs["ms"].append(zs[:, 4 * HEAD_W:].reshape(DB, DS, N_HEADS, HEAD_DIM))

    k_prompt, v_prompt = _kv_pack(outs["z"], Tp, tm=min(2 * ROW_TILE, S))
    if depth % 2 == 0:
        y_p, y_s = x_split
    else:
        y_p, y_s = x[:Tp], x[Tp:]
    return (y_p.reshape(B, S, D), y_s.reshape(DB, DS, D),
            k_prompt.reshape(depth, B, S, N_HEADS, HEAD_DIM), v_prompt.reshape(depth, B, S, N_HEADS, HEAD_DIM),
            jnp.stack(outs["fp"]),
            jnp.stack(outs["ks"]), jnp.stack(outs["vs"]), jnp.stack(outs["fs"]), jnp.stack(outs["ms"]))
```

```python
import functools

import jax
import jax.numpy as jnp
from jax import lax
from jax.experimental import pallas as pl
from jax.experimental.pallas import tpu as pltpu

F32 = jnp.float32
BF16 = jnp.bfloat16
I32 = jnp.int32

HEAD_DIM = 128
N_HEADS = 8
HEAD_W = N_HEADS * HEAD_DIM
N_SECTIONS = 5
EPS = 1e-6
NEG_INF = float("-inf")
TOP_K = 2
ROW_TILE = 256
MOE_SUB = ROW_TILE // 2
MOE_ITEM_BLOCKS = 20
VMEM_LIMIT = 56 * 1024 * 1024


def _params(*sem):
    return pltpu.CompilerParams(dimension_semantics=sem, vmem_limit_bytes=VMEM_LIMIT)


def _nt_dot(a, b):
    return lax.dot_general(a, b, (((1,), (1,)), ((), ())), preferred_element_type=F32)


def _rms(x):
    return x * lax.rsqrt(jnp.mean(x * x, axis=-1, keepdims=True) + EPS)


def _split3(x):
    x1 = x.astype(BF16).astype(F32)
    r1 = x - x1
    x2 = r1.astype(BF16).astype(F32)
    x3 = (r1 - x2).astype(BF16).astype(F32)
    return x1, x2, x3


def _pick_lane(x, lane_idx):
    lane = lax.broadcasted_iota(I32, x.shape, 1)
    return jnp.sum(jnp.where(lane == lane_idx, x, 0.0), axis=1, keepdims=True)


def _proj_body(x_ref, g_ref, w_ref, wf_ref, bf_ref, gain_ref, z_ref, lf_ref, c_ref, h_scr, carry_scr,
               *, chunk, chunks_per_seq):
    i = pl.program_id(0)
    j = pl.program_id(1)
    n_chunks = x_ref.shape[0] // chunk

    def section(jj):
        first = jj == 0
        normed = jj in (0, 1, 4)
        w = w_ref[...]
        if first:
            r = lax.broadcasted_iota(I32, (chunk, chunk), 0)
            c = lax.broadcasted_iota(I32, (chunk, chunk), 1)
            lower = (c <= r).astype(BF16)

            @pl.when(i == 0)
            def _():
                carry_scr[...] = jnp.zeros(carry_scr.shape, F32)

            carry = carry_scr[...]
        for n in range(n_chunks):
            rows = slice(n * chunk, (n + 1) * chunk)
            if first:
                hb = (_rms(x_ref[rows, :]) * g_ref[...]).astype(BF16)
                h_scr[rows, :] = hb
                f = jnp.dot(hb, wf_ref[...], preferred_element_type=F32) + bf_ref[...]
                lf = jnp.minimum(f, 0.0) - jnp.log1p(jnp.exp(-jnp.abs(f)))
                lf_ref[rows, :] = lf
                first_of_seq = (i * n_chunks + n) % chunks_per_seq == 0
                cs = jnp.where(first_of_seq, 0.0, carry)
                for part in _split3(lf):
                    cs = cs + jnp.dot(lower, part.astype(BF16), preferred_element_type=F32)
                c_ref[rows, :] = cs
                carry = cs[chunk - 1:chunk, :]
            else:
                hb = h_scr[rows, :]
            z = jnp.dot(hb, w, preferred_element_type=F32)
            if normed:
                for h in range(N_HEADS):
                    sl = slice(h * HEAD_DIM, (h + 1) * HEAD_DIM)
                    z_ref[rows, sl] = _rms(z[:, sl]) * gain_ref[:, sl]
            else:
                z_ref[rows, :] = z
        if first:
            carry_scr[...] = carry

    for jj in range(N_SECTIONS):
        pl.when(j == jj)(functools.partial(section, jj))


def _project(x, g, w_main, wf, bf, gains, *, tm, seq):
    T, D = x.shape
    row = lambda i, j: (i, 0)
    const = lambda i, j: (0, 0)
    return pl.pallas_call(
        functools.partial(_proj_body, chunk=ROW_TILE, chunks_per_seq=seq // ROW_TILE),
        grid=(T // tm, N_SECTIONS),
        in_specs=[
            pl.BlockSpec((tm, D), row),
            pl.BlockSpec((1, D), const),
            pl.BlockSpec((D, HEAD_W), lambda i, j: (0, j)),
            pl.BlockSpec((D, HEAD_DIM), const),
            pl.BlockSpec((1, HEAD_DIM), const),
            pl.BlockSpec((None, 1, HEAD_W), lambda i, j: (j, 0, 0)),
        ],
        out_specs=[pl.BlockSpec((tm, HEAD_W), lambda i, j: (i, j)),
                   pl.BlockSpec((tm, HEAD_DIM), row),
                   pl.BlockSpec((tm, HEAD_DIM), row)],
        out_shape=[
            jax.ShapeDtypeStruct((T, N_SECTIONS * HEAD_W), F32),
            jax.ShapeDtypeStruct((T, HEAD_DIM), F32),
            jax.ShapeDtypeStruct((T, HEAD_DIM), F32),
        ],
        scratch_shapes=[pltpu.VMEM((tm, D), BF16), pltpu.VMEM((1, HEAD_DIM), F32)],
        compiler_params=_params("arbitrary", "arbitrary"),
        name="in_proj",
    )(x, g, w_main, wf, bf, gains)


def _bias_lanes(c_col, key_side):
    lane = lax.broadcasted_iota(I32, (1, HEAD_DIM), 1)
    c1, c2, c3 = _split3(c_col)
    if key_side:
        terms = (1.0, 1.0, 1.0, -c1, -c2, -c3)
    else:
        terms = (c1, c2, c3, 1.0, 1.0, 1.0)
    out = jnp.zeros((c_col.shape[0], HEAD_DIM), F32)
    for idx, t in enumerate(terms):
        out = jnp.where(lane == idx, t, out)
    return out.astype(BF16)


def _fox_prompt_body(q_ref, k_ref, v_ref, cq_ref, ck_ref, hg_ref, o_ref, ka, vt, m_scr, acc_scr, *, tq, hp):
    hgrp = pl.program_id(1)
    qi = pl.program_id(2)
    n_kv = vt.shape[1]
    ext = vt.shape[2] - HEAD_DIM

    @pl.when(qi == 0)
    def _():
        row = lax.broadcasted_iota(I32, (ext, tq), 0)
        ones_ext = jnp.where(row == 0, 1.0, 0.0).astype(BF16)
        for hh in range(hp):
            sl = slice(hh * HEAD_DIM, (hh + 1) * HEAD_DIM)
            ka[hh, :, :HEAD_DIM] = k_ref[:, sl].astype(BF16)
            ka[hh, :, HEAD_DIM:] = _bias_lanes(_pick_lane(ck_ref[...], hgrp * hp + hh), True)
            for n in range(n_kv):
                vt[hh, n, :HEAD_DIM, :] = v_ref[n * tq:(n + 1) * tq, sl].T.astype(BF16)
                vt[hh, n, HEAD_DIM:, :] = ones_ext

    qa = []
    for hh in range(hp):
        sl = slice(hh * HEAD_DIM, (hh + 1) * HEAD_DIM)
        bias = _bias_lanes(_pick_lane(cq_ref[...], hgrp * hp + hh), False)
        qa.append(jnp.concatenate([q_ref[:, sl].astype(BF16), bias], axis=1))
        m_scr[hh] = jnp.full((1, tq), NEG_INF, F32)
        acc_scr[hh] = jnp.zeros(acc_scr.shape[1:], F32)

    def block(kj, diagonal):
        start = pl.multiple_of(kj * tq, tq)
        st = [_nt_dot(ka[hh, pl.ds(start, tq), :], qa[hh]) for hh in range(hp)]
        vts = [vt[hh, kj] for hh in range(hp)]
        ms = [m_scr[hh] for hh in range(hp)]
        accs = [acc_scr[hh] for hh in range(hp)]
        if diagonal:
            key = lax.broadcasted_iota(I32, (tq, tq), 0)
            qry = lax.broadcasted_iota(I32, (tq, tq), 1)
            st = [jnp.where(key <= qry, s, NEG_INF) for s in st]
        m_new = [jnp.maximum(m, jnp.max(s, axis=0, keepdims=True)) for m, s in zip(ms, st)]
        pt = [jnp.exp(s - mn).astype(BF16) for s, mn in zip(st, m_new)]
        pv = [jnp.dot(v, p, preferred_element_type=F32) for v, p in zip(vts, pt)]
        new_acc = [jnp.exp(m - mn) * a + d for m, mn, a, d in zip(ms, m_new, accs, pv)]
        for hh in range(hp):
            acc_scr[hh] = new_acc[hh]
            m_scr[hh] = m_new[hh]

    def body(kj, carry):
        block(kj, False)
        return carry

    lax.fori_loop(0, qi, body, 0)
    block(qi, True)
    for hh in range(hp):
        acc = acc_scr[hh]
        ot = acc[:HEAD_DIM, :] / acc[HEAD_DIM:HEAD_DIM + 1, :]
        ot = ot * lax.rsqrt(jnp.mean(ot * ot, axis=0, keepdims=True) + EPS)
        o_ref[:, hh * HEAD_DIM:(hh + 1) * HEAD_DIM] = (ot.T * hg_ref[hh]).astype(o_ref.dtype)


def _fox_prompt(z, c, hg, *, batch, seq, tq, hp):
    nq = seq // tq
    ngrp = N_HEADS // hp
    w = hp * HEAD_DIM
    ext = 16
    return pl.pallas_call(
        functools.partial(_fox_prompt_body, tq=tq, hp=hp),
        grid=(batch, ngrp, nq),
        in_specs=[
            pl.BlockSpec((tq, w), lambda b, h, i: (b * nq + i, h)),
            pl.BlockSpec((seq, w), lambda b, h, i: (b, ngrp + h)),
            pl.BlockSpec((seq, w), lambda b, h, i: (b, 2 * ngrp + h)),
            pl.BlockSpec((tq, HEAD_DIM), lambda b, h, i: (b * nq + i, 0)),
            pl.BlockSpec((seq, HEAD_DIM), lambda b, h, i: (b, 0)),
            pl.BlockSpec((hp, 1, HEAD_DIM), lambda b, h, i: (h, 0, 0)),
        ],
        out_specs=pl.BlockSpec((tq, w), lambda b, h, i: (b * nq + i, h)),
        out_shape=jax.ShapeDtypeStruct((batch * seq, HEAD_W), BF16),
        scratch_shapes=[pltpu.VMEM((hp, seq, 2 * HEAD_DIM), BF16),
                        pltpu.VMEM((hp, nq, HEAD_DIM + ext, tq), BF16),
                        pltpu.VMEM((hp, 1, tq), F32), pltpu.VMEM((hp, HEAD_DIM + ext, tq), F32)],
        compiler_params=_params("arbitrary", "arbitrary", "arbitrary"),
        name="fox_prompt",
    )(z, z, z, c, c, hg)


def _fox_sample_body(pt_ref, q_ref, kn_ref, vn_ref, lfr_ref, lfc_ref, hg_ref, ck_hbm, cv_hbm, *rest,
                     layer, pps, page, ring):
    lp = rest[:pps]
    o_ref, m_scr, l_scr, acc_scr, carry_scr, negq_scr, kbuf, vbuf, sem = rest[pps:]
    j = pl.program_id(1)
    n_j = pl.num_programs(1)
    rows = q_ref.shape[0]
    dsq = rows // N_HEADS
    cols = page * N_HEADS
    n_pages = n_j * pps

    g = pl.program_id(0) * n_j + j
    n_g = pl.num_programs(0) * n_j

    def fetch(gg):
        slot = gg % ring
        seq = gg // n_j
        first = n_pages - 1 - (gg % n_j) * pps
        for p in range(pps):
            pg = pt_ref[seq, first - p]
            pltpu.make_async_copy(ck_hbm.at[layer, pg], kbuf.at[slot, p], sem.at[slot]).start()
            pltpu.make_async_copy(cv_hbm.at[layer, pg], vbuf.at[slot, p], sem.at[slot]).start()

    @pl.when(g == 0)
    def _():
        for ahead in range(ring - 1):
            @pl.when(ahead < n_g)
            def _():
                fetch(ahead)

    @pl.when(g + ring - 1 < n_g)
    def _():
        fetch(g + ring - 1)

    slot = g % ring
    pltpu.make_async_copy(ck_hbm.at[layer, pl.ds(0, pps)], kbuf.at[slot], sem.at[slot]).wait()
    pltpu.make_async_copy(cv_hbm.at[layer, pl.ds(0, pps)], vbuf.at[slot], sem.at[slot]).wait()

    rn = lax.broadcasted_iota(I32, (rows, rows), 0)
    cn = lax.broadcasted_iota(I32, (rows, rows), 1)
    keep_new = ((rn // dsq) == (cn % N_HEADS)) & ((cn // N_HEADS) <= (rn % dsq))
    cq_col = jnp.sum(jnp.where(keep_new, lfr_ref[...], 0.0), axis=1, keepdims=True)

    @pl.when(j == 0)
    def _():
        m_scr[...] = jnp.full(m_scr.shape, NEG_INF, F32)
        l_scr[...] = jnp.zeros(l_scr.shape, F32)
        acc_scr[...] = jnp.zeros(acc_scr.shape, F32)
        carry_scr[...] = jnp.zeros(carry_scr.shape, F32)
        rp = lax.broadcasted_iota(I32, (rows, cols), 0)
        cp = lax.broadcasted_iota(I32, (rows, cols), 1)
        negq_scr[...] = jnp.where((rp // dsq) == (cp % N_HEADS), cq_col, NEG_INF)

    def update(scores, values):
        m, l, acc = m_scr[...], l_scr[...], acc_scr[...]
        mx = functools.reduce(jnp.maximum, scores)
        m_new = jnp.maximum(m, jnp.max(mx, axis=1, keepdims=True))
        alpha = jnp.exp(m - m_new)
        ps = [jnp.exp(s - m_new) for s in scores]
        pv = [jnp.dot(p.astype(BF16), v, preferred_element_type=F32) for p, v in zip(ps, values)]
        l_scr[...] = alpha * l + jnp.sum(functools.reduce(jnp.add, ps), axis=1, keepdims=True)
        acc_scr[...] = alpha * acc + functools.reduce(jnp.add, pv)
        m_scr[...] = m_new

    q = q_ref[...]
    raw = [_nt_dot(q, kbuf[slot, p].reshape(cols, HEAD_DIM).astype(BF16)) for p in range(pps)]

    lpp = jnp.concatenate([lp[p][...] for p in range(pps)], axis=0)
    lane = lax.broadcasted_iota(I32, (pps, cols), 1)
    sub = lax.broadcasted_iota(I32, (pps, cols), 0)
    n_steps = (page - 1).bit_length()
    tot = lpp
    for t in range(n_steps):
        tot = tot + pltpu.roll(tot, N_HEADS << t, axis=1)
    suf = jnp.where(lane < cols - N_HEADS, pltpu.roll(lpp, cols - N_HEADS, axis=1), 0.0)
    for t in range(n_steps):
        sh = N_HEADS << t
        suf = suf + jnp.where(lane < cols - sh, pltpu.roll(suf, cols - sh, axis=1), 0.0)
    pre = jnp.where(sub >= 1, pltpu.roll(tot, 1, axis=0), 0.0)
    for t in range((pps - 1).bit_length()):
        sh = 1 << t
        pre = pre + jnp.where(sub >= sh, pltpu.roll(pre, sh, axis=0), 0.0)
    carry = carry_scr[...]
    bias = suf + pre + carry

    negq = negq_scr[...]
    v2 = [vbuf[slot, p].reshape(cols, HEAD_DIM).astype(BF16) for p in range(pps)]
    update([raw[p] + negq + bias[p:p + 1, :] for p in range(pps)], v2)
    carry_scr[...] = carry + jnp.sum(tot, axis=0, keepdims=True)

    @pl.when(j == pl.num_programs(1) - 1)
    def _():
        cn_row = jnp.sum(jnp.where(((rn % N_HEADS) == (cn % N_HEADS)) & ((rn // N_HEADS) <= (cn // N_HEADS)),
                                   lfc_ref[...], 0.0), axis=0, keepdims=True)
        s = jnp.where(keep_new, _nt_dot(q, kn_ref[...].astype(BF16)) + cq_col - cn_row, NEG_INF)
        update([s], [vn_ref[...].astype(BF16)])
        o = acc_scr[...] / l_scr[...]
        o_ref[...] = (_rms(o) * hg_ref[...]).astype(o_ref.dtype)


def _fox_sample(layer, page_table, q, kn, vn, lfr, lfc, hg_rows, cache_k, cache_v, cache_lf2, *, pps):
    DB, n_pages = page_table.shape
    rows = q.shape[1]
    page = cache_k.shape[2]
    cols = page * N_HEADS
    n_steps = n_pages // pps

    def seq_map(b, j, pt):
        return (b, 0, 0)

    def lf_map(p):
        def f(b, j, pt):
            return (layer, pt[b, n_pages - 1 - (j * pps + p)], 0, 0)
        return f

    ring = 3
    grid_spec = pltpu.PrefetchScalarGridSpec(
        num_scalar_prefetch=1,
        grid=(DB, n_steps),
        in_specs=[
            pl.BlockSpec((None, rows, HEAD_DIM), seq_map),
            pl.BlockSpec((None, rows, HEAD_DIM), seq_map),
            pl.BlockSpec((None, rows, HEAD_DIM), seq_map),
            pl.BlockSpec((None, 1, rows), seq_map),
            pl.BlockSpec((None, rows, 1), seq_map),
            pl.BlockSpec((rows, HEAD_DIM), lambda b, j, pt: (0, 0)),
            pl.BlockSpec(memory_space=pl.ANY),
            pl.BlockSpec(memory_space=pl.ANY),
        ] + [pl.BlockSpec((None, None, 1, cols), lf_map(p)) for p in range(pps)],
        out_specs=pl.BlockSpec((None, rows, HEAD_DIM), seq_map),
        scratch_shapes=[pltpu.VMEM((rows, 1), F32), pltpu.VMEM((rows, 1), F32),
                        pltpu.VMEM((rows, HEAD_DIM), F32), pltpu.VMEM((1, cols), F32),
                        pltpu.VMEM((rows, cols), F32),
                        pltpu.VMEM((ring, pps, page, N_HEADS, HEAD_DIM), F32),
                        pltpu.VMEM((ring, pps, page, N_HEADS, HEAD_DIM), F32),
                        pltpu.SemaphoreType.DMA((ring,))],
    )
    return pl.pallas_call(
        functools.partial(_fox_sample_body, layer=layer, pps=pps, page=page, ring=ring),
        grid_spec=grid_spec,
        out_shape=jax.ShapeDtypeStruct((DB, rows, HEAD_DIM), BF16),
        compiler_params=_params("arbitrary", "arbitrary"),
        name="fox_sample",
    )(page_table, q, kn, vn, lfr, lfc, hg_rows, cache_k, cache_v, *([cache_lf2] * pps))


def _merge_body(attp_ref, atts_ref, u_ref, vm_ref, wcp_ref, bcp_ref, wcs_ref, bcs_ref, hg_ref, wo_ref,
                x_ref, g_ref, *rest, n_prompt_tiles, n_exp):
    if n_exp:
        rw_ref, xo_ref, ho_ref, eid_ref, gate_ref, cnt_ref, att_scr, mlp_scr = rest
    else:
        xo_ref, ho_ref, att_scr, mlp_scr = rest
    i = pl.program_id(0)
    tm = x_ref.shape[0]

    def gate_heads(wc_ref, bc_ref):
        chunk = wc_ref.shape[1]
        for h in range(N_HEADS):
            sl = slice(h * HEAD_DIM, (h + 1) * HEAD_DIM)
            for c in range(tm // chunk):
                rows = slice(c * chunk, (c + 1) * chunk)
                s = jnp.dot(wc_ref[h], vm_ref[rows, sl].astype(BF16), preferred_element_type=F32) + bc_ref[h]
                o = u_ref[rows, sl] * s
                mlp_scr[rows, sl] = (_rms(o) * hg_ref[:, sl]).astype(BF16)

    @pl.when(i < n_prompt_tiles)
    def _():
        att_scr[...] = attp_ref[...]
        gate_heads(wcp_ref, bcp_ref)

    @pl.when(i >= n_prompt_tiles)
    def _():
        att_scr[...] = atts_ref[...]
        gate_heads(wcs_ref, bcs_ref)

    y = jnp.dot(att_scr[...], wo_ref[:HEAD_W, :], preferred_element_type=F32)
    y = y + jnp.dot(mlp_scr[...], wo_ref[HEAD_W:, :], preferred_element_type=F32)
    xn = x_ref[...] + y
    xo_ref[...] = xn
    hn = _rms(xn) * g_ref[...]
    ho_ref[...] = hn.astype(ho_ref.dtype)
    if n_exp:
        _route_tile(hn.astype(BF16), rw_ref, eid_ref, gate_ref, cnt_ref, n_exp)


def _merge(att_p, att_s, z, wc_p, bc_p, wc_s, bc_s, hg_mlp, wo, x, g, *, h_dtype, route_w=None, n_exp=0):
    T, D = x.shape
    tm = ROW_TILE
    n_prompt_tiles = att_p.shape[0] // tm
    assert att_s.shape[0] == tm and wc_s.shape[1] == tm
    chunk = wc_p.shape[1]
    row = lambda i: (i, 0)
    c2 = lambda i: (0, 0)
    c3 = lambda i: (0, 0, 0)
    route_in = [] if route_w is None else [pl.BlockSpec((D, HEAD_DIM), c2)]
    route_args = [] if route_w is None else [route_w]
    route_out = [] if route_w is None else [pl.BlockSpec((tm, HEAD_DIM), row), pl.BlockSpec((tm, HEAD_DIM), row),
                                            pl.BlockSpec((1, HEAD_DIM), c2)]
    route_shape = [] if route_w is None else [jax.ShapeDtypeStruct((T, HEAD_DIM), I32),
                                              jax.ShapeDtypeStruct((T, HEAD_DIM), F32),
                                              jax.ShapeDtypeStruct((1, HEAD_DIM), F32)]
    return pl.pallas_call(
        functools.partial(_merge_body, n_prompt_tiles=n_prompt_tiles, n_exp=n_exp if route_w is not None else 0),
        grid=(T // tm,),
        in_specs=[
            pl.BlockSpec((tm, HEAD_W), lambda i: (jnp.minimum(i, n_prompt_tiles - 1), 0)),
            pl.BlockSpec((tm, HEAD_W), c2),
            pl.BlockSpec((tm, HEAD_W), lambda i: (i, 3)),
            pl.BlockSpec((tm, HEAD_W), lambda i: (i, 4)),
            pl.BlockSpec((N_HEADS, chunk, chunk), c3),
            pl.BlockSpec((N_HEADS, chunk, HEAD_DIM), c3),
            pl.BlockSpec((N_HEADS, tm, tm), c3),
            pl.BlockSpec((N_HEADS, tm, HEAD_DIM), c3),
            pl.BlockSpec((1, HEAD_W), c2),
            pl.BlockSpec((2 * HEAD_W, D), c2),
            pl.BlockSpec((tm, D), row),
            pl.BlockSpec((1, D), c2),
        ] + route_in,
        out_specs=[pl.BlockSpec((tm, D), row), pl.BlockSpec((tm, D), row)] + route_out,
        out_shape=[jax.ShapeDtypeStruct((T, D), F32), jax.ShapeDtypeStruct((T, D), h_dtype)] + route_shape,
        scratch_shapes=[pltpu.VMEM((tm, HEAD_W), BF16), pltpu.VMEM((tm, HEAD_W), BF16)],
        compiler_params=_params("arbitrary"),
        name="mlp_merge_out",
    )(att_p, att_s, z, z, wc_p, bc_p, wc_s, bc_s, hg_mlp, wo, x, g, *route_args)


def _dense_body(h_ref, wg_ref, wu_ref, wd_ref, *rest, has_tail):
    if has_tail:
        wgt_ref, wut_ref, wdt_ref, x_ref, o_ref = rest
    else:
        x_ref, o_ref = rest
    f = pl.program_id(1)

    @pl.when(f == 0)
    def _():
        o_ref[...] = x_ref[...]

    n_chunks = h_ref.shape[0] // ROW_TILE

    def swiglu(wg, wu, wd):
        wgv, wuv, wdv = (ref[...].astype(BF16) for ref in (wg, wu, wd))
        hs = [h_ref[n * ROW_TILE:(n + 1) * ROW_TILE, :] for n in range(n_chunks)]
        ab = [(jnp.dot(h, wgv, preferred_element_type=F32), jnp.dot(h, wuv, preferred_element_type=F32)) for h in hs]
        gs = [(a * jax.nn.sigmoid(a) * b).astype(BF16) for a, b in ab]
        ds = [jnp.dot(g, wdv, preferred_element_type=F32) for g in gs]
        for n, d in enumerate(ds):
            o_ref[n * ROW_TILE:(n + 1) * ROW_TILE, :] += d

    swiglu(wg_ref, wu_ref, wd_ref)

    if has_tail:
        @pl.when(f == pl.num_programs(1) - 1)
        def _():
            swiglu(wgt_ref, wut_ref, wdt_ref)


def _dense_ffn(h, wg, wu, wd, x, *, tm, tf):
    T, D = x.shape
    d_ff = wg.shape[1]
    n_main = d_ff // tf
    tail = d_ff - n_main * tf
    assert tail % HEAD_DIM == 0 and (n_main * tf) % max(tail, 1) == 0
    in_specs = [
        pl.BlockSpec((tm, D), lambda i, f: (i, 0)),
        pl.BlockSpec((D, tf), lambda i, f: (0, f)),
        pl.BlockSpec((D, tf), lambda i, f: (0, f)),
        pl.BlockSpec((tf, D), lambda i, f: (f, 0)),
    ]
    args = [h, wg, wu, wd]
    if tail:
        t_idx = n_main * tf // tail
        in_specs += [pl.BlockSpec((D, tail), lambda i, f: (0, t_idx)),
                     pl.BlockSpec((D, tail), lambda i, f: (0, t_idx)),
                     pl.BlockSpec((tail, D), lambda i, f: (t_idx, 0))]
        args += [wg, wu, wd]
    in_specs.append(pl.BlockSpec((tm, D), lambda i, f: (i, 0)))
    args.append(x)
    return pl.pallas_call(
        functools.partial(_dense_body, has_tail=bool(tail)),
        grid=(T // tm, n_main),
        in_specs=in_specs,
        out_specs=pl.BlockSpec((tm, D), lambda i, f: (i, 0)),
        out_shape=jax.ShapeDtypeStruct((T, D), F32),
        compiler_params=_params("arbitrary", "arbitrary"),
        name="dense_ffn",
    )(*args)


def _route_tile(hb, w_ref, eid_ref, gate_ref, cnt_ref, n_exp):
    tm = hb.shape[0]
    logits = jnp.dot(hb, w_ref[...], preferred_element_type=F32)
    lane = lax.broadcasted_iota(I32, logits.shape, 1)
    big = logits.shape[1]
    l1 = jnp.where(lane < n_exp, logits, NEG_INF)
    m1 = jnp.max(l1, axis=1, keepdims=True)
    i1 = jnp.min(jnp.where(l1 == m1, lane, big), axis=1, keepdims=True)
    l2 = jnp.where(lane == i1, NEG_INF, l1)
    m2 = jnp.max(l2, axis=1, keepdims=True)
    i2 = jnp.min(jnp.where(l2 == m2, lane, big), axis=1, keepdims=True)
    e2 = jnp.exp(m2 - m1)
    p1 = 1.0 / (1.0 + e2)
    p2 = e2 / (1.0 + e2)
    gate_ref[...] = jnp.where(lane == 0, p1, jnp.where(lane == 1, p2, 0.0))

    @pl.when(pl.program_id(0) == 0)
    def _():
        cnt_ref[...] = jnp.zeros(cnt_ref.shape, F32)

    hit1 = (lane == i1).astype(F32)
    hit2 = (lane == i2).astype(F32)
    r = lax.broadcasted_iota(I32, (tm, tm), 0)
    c = lax.broadcasted_iota(I32, (tm, tm), 1)
    before = jnp.dot((c < r).astype(BF16), (hit1 + hit2).astype(BF16), preferred_element_type=F32)
    before = before + cnt_ref[...]
    rank1 = jnp.sum(hit1 * before, axis=1, keepdims=True).astype(I32)
    rank2 = jnp.sum(hit2 * before, axis=1, keepdims=True).astype(I32)
    cnt_ref[...] += jnp.sum(hit1 + hit2, axis=0, keepdims=True)
    eid_ref[...] = jnp.where(lane == 0, i1, jnp.where(lane == 1, i2,
                             jnp.where(lane == TOP_K, rank1, jnp.where(lane == TOP_K + 1, rank2, 0))))


def _row_copy(src_hbm, row, dst_vmem, r, sem):
    return pltpu.make_async_copy(src_hbm.at[pl.ds(row, 1)], dst_vmem.at[pl.ds(r, 1)], sem)


def _moe_body(ie_ref, ifirst_ref, inb_ref, cnt_ref, src_ref,
              h_hbm, wg_ref, wu_ref, wd_ref, y_hbm,
              xbuf, acc, stage, sem_in, sem_out, *, sub, unroll):
    del ie_ref
    w = pl.program_id(0)
    f = pl.program_id(1)
    n_f = pl.num_programs(1)
    n_items, n_used = cnt_ref[0], cnt_ref[1]
    n_total = y_hbm.shape[0] // sub

    def block_rows(r):
        return pl.ds(pl.multiple_of(r * sub, sub), sub)

    def out_copy(r_local, r_global):
        return pltpu.make_async_copy(acc.at[block_rows(r_local)], y_hbm.at[block_rows(r_global)], sem_out)

    def drain_out(n):
        def wait(r, carry):
            out_copy(r, r).wait()
            return carry

        lax.fori_loop(0, n, wait, 0)

    @pl.when((w == 0) & (f == 0))
    def _():
        acc[block_rows(0)] = jnp.zeros((sub, acc.shape[1]), F32)

        def fill(r, carry):
            out_copy(0, r).start()
            return carry

        def drain(r, carry):
            out_copy(0, r).wait()
            return carry

        lax.fori_loop(n_used, n_total, fill, 0)
        lax.fori_loop(n_used, n_total, drain, 0)

    @pl.when(w < n_items)
    def _():
        first = ifirst_ref[w]
        nb = inb_ref[w]

        @pl.when(f == 0)
        def _():
            def issue(blk, slot):
                base = (first + blk) * sub

                def body(rr, carry):
                    for u in range(unroll):
                        r = rr * unroll + u
                        _row_copy(h_hbm, src_ref[base + r], stage.at[slot], r, sem_in.at[slot]).start(priority=u % 2)
                    return carry

                lax.fori_loop(0, sub // unroll, body, 0)

            n_slots = stage.shape[0]
            ahead = n_slots - 1
            for b0 in range(ahead):
                @pl.when(b0 < nb)
                def _():
                    issue(b0, b0)

            @pl.when(w > 0)
            def _():
                drain_out(inb_ref[w - 1])

            def per_block(blk, carry):
                slot = blk % n_slots

                @pl.when(blk + ahead < nb)
                def _():
                    issue(blk + ahead, (blk + ahead) % n_slots)

                pltpu.make_async_copy(h_hbm.at[pl.ds(0, sub)], stage.at[slot], sem_in.at[slot]).wait()
                xbuf[block_rows(blk)] = stage[slot].astype(BF16)
                acc[block_rows(blk)] = jnp.zeros((sub, acc.shape[1]), F32)
                return carry

            lax.fori_loop(0, nb, per_block, 0)

        def swiglu(chunks):
            spans = [pl.ds(start if isinstance(start, int) else pl.multiple_of(start, sub), size)
                     for start, size in chunks]
            xs = [xbuf[s] for s in spans]
            wgv, wuv, wdv = (ref[...].astype(BF16) for ref in (wg_ref, wu_ref, wd_ref))
            ab = [(jnp.dot(x, wgv, preferred_element_type=F32),
                   jnp.dot(x, wuv, preferred_element_type=F32)) for x in xs]
            hs = [(a * jax.nn.sigmoid(a) * b).astype(BF16) for a, b in ab]
            ds = [jnp.dot(hh, wdv, preferred_element_type=F32) for hh in hs]
            for s, d in zip(spans, ds):
                acc[s] += d

        wide = 2 * sub
        n_wide = nb // 2

        for rp in range(xbuf.shape[0] // (2 * wide)):
            @pl.when(rp < n_wide // 2)
            def _():
                swiglu([(2 * rp * wide, wide), ((2 * rp + 1) * wide, wide)])

        @pl.when(n_wide % 2 == 1)
        def _():
            swiglu([((n_wide - 1) * wide, wide)])

        @pl.when(nb % 2 == 1)
        def _():
            swiglu([((nb - 1) * sub, sub)])

        @pl.when(f == n_f - 1)
        def _():
            def start(r, carry):
                out_copy(r, first + r).start()
                return carry

            lax.fori_loop(0, nb, start, 0)

            @pl.when(w == n_items - 1)
            def _():
                drain_out(nb)


def _moe_ffn(tables, h, wg, wu, wd, *, tf, max_blocks):
    item_expert, item_first, item_nb, counts, src, n_rows = tables
    T, D = h.shape
    sub = MOE_SUB
    n_f = wg.shape[2] // tf
    last = n_f - 1

    def live(w, cnt):
        return jnp.minimum(w, cnt[0] - 1)

    def fidx(w, f, cnt):
        return jnp.where(w < cnt[0], f, last)

    grid_spec = pltpu.PrefetchScalarGridSpec(
        num_scalar_prefetch=5,
        grid=(item_expert.shape[0], n_f),
        in_specs=[
            pl.BlockSpec(memory_space=pl.ANY),
            pl.BlockSpec((None, D, tf), lambda w, f, ie, i1, i2, cnt, s: (ie[live(w, cnt)], 0, fidx(w, f, cnt))),
            pl.BlockSpec((None, D, tf), lambda w, f, ie, i1, i2, cnt, s: (ie[live(w, cnt)], 0, fidx(w, f, cnt))),
            pl.BlockSpec((None, tf, D), lambda w, f, ie, i1, i2, cnt, s: (ie[live(w, cnt)], fidx(w, f, cnt), 0)),
        ],
        out_specs=pl.BlockSpec(memory_space=pl.ANY),
        scratch_shapes=[
            pltpu.VMEM((max_blocks * sub, D), BF16),
            pltpu.VMEM((max_blocks * sub, D), F32),
            pltpu.VMEM((4, sub, D), F32),
            pltpu.SemaphoreType.DMA((4,)), pltpu.SemaphoreType.DMA(()),
        ],
    )
    return pl.pallas_call(
        functools.partial(_moe_body, sub=sub, unroll=8),
        grid_spec=grid_spec,
        out_shape=jax.ShapeDtypeStruct((n_rows, D), F32),
        compiler_params=_params("arbitrary", "arbitrary"),
        name="moe_ffn",
    )(item_expert, item_first, item_nb, counts, src, h, wg, wu, wd)


def _combine_body(pos_ref, x_ref, gate_ref, y_hbm, *rest, tm, unroll, split_tile):
    o_refs, (buf, sem) = rest[:-2], rest[-2:]
    i = pl.program_id(0)
    slot = i % 2

    def gather(tile, to_slot):
        base = tile * tm * TOP_K

        def issue(rr, carry):
            for u in range(unroll):
                r = rr * unroll + u
                for k in range(TOP_K):
                    _row_copy(y_hbm, pos_ref[base + r * TOP_K + k], buf.at[to_slot, k], r,
                              sem.at[to_slot, k]).start(priority=k)
            return carry

        lax.fori_loop(0, tm // unroll, issue, 0)

    @pl.when(i == 0)
    def _():
        gather(0, 0)

    @pl.when(i + 1 < pl.num_programs(0))
    def _():
        gather(i + 1, 1 - slot)

    out = x_ref[...]
    for k in range(TOP_K):
        pltpu.make_async_copy(y_hbm.at[pl.ds(0, tm)], buf.at[slot, k], sem.at[slot, k]).wait()
        out = out + gate_ref[:, k:k + 1] * buf[slot, k]
    if split_tile is None:
        o_refs[0][...] = out
    else:
        @pl.when(pl.program_id(0) < split_tile)
        def _():
            o_refs[0][...] = out

        @pl.when(pl.program_id(0) >= split_tile)
        def _():
            o_refs[1][...] = out


def _combine(pos, x, gate, y, *, tm, split_rows=None):
    T, D = x.shape
    if split_rows is None:
        split_tile = None
        out_specs = [pl.BlockSpec((tm, D), lambda i, p: (i, 0))]
        out_shape = [jax.ShapeDtypeStruct((T, D), F32)]
    else:
        split_tile = split_rows // tm
        out_specs = [pl.BlockSpec((tm, D), lambda i, p: (jnp.minimum(i, split_tile - 1), 0)),
                     pl.BlockSpec((tm, D), lambda i, p: (jnp.maximum(i - split_tile, 0), 0))]
        out_shape = [jax.ShapeDtypeStruct((split_rows, D), F32), jax.ShapeDtypeStruct((T - split_rows, D), F32)]
    grid_spec = pltpu.PrefetchScalarGridSpec(
        num_scalar_prefetch=1,
        grid=(T // tm,),
        in_specs=[pl.BlockSpec((tm, D), lambda i, p: (i, 0)),
                  pl.BlockSpec((tm, HEAD_DIM), lambda i, p: (i, 0)),
                  pl.BlockSpec(memory_space=pl.ANY)],
        out_specs=out_specs,
        scratch_shapes=[pltpu.VMEM((2, TOP_K, tm, D), F32), pltpu.SemaphoreType.DMA((2, TOP_K))],
    )
    return pl.pallas_call(
        functools.partial(_combine_body, tm=tm, unroll=4, split_tile=split_tile),
        grid_spec=grid_spec,
        out_shape=out_shape,
        compiler_params=_params("arbitrary"),
        name="combine_rows",
    )(pos, x, gate, y)


def _kv_pack_body(*refs, n_layers):
    z_refs, (ko_ref, vo_ref) = refs[:2 * n_layers], refs[2 * n_layers:]
    layer = pl.program_id(0)
    tm = z_refs[0].shape[0]
    for l in range(n_layers):
        @pl.when(layer == l)
        def _():
            for src, dst in ((z_refs[2 * l], ko_ref), (z_refs[2 * l + 1], vo_ref)):
                for h in range(N_HEADS):
                    dst[pl.ds(h, tm, stride=N_HEADS), :] = src[:, h * HEAD_DIM:(h + 1) * HEAD_DIM]


def _kv_pack(zs, n_rows, *, tm):
    n_layers = len(zs)
    n_tiles = n_rows // tm
    in_specs, args = [], []
    for l, z in enumerate(zs):
        for section in (1, 2):
            in_specs.append(pl.BlockSpec(
                (tm, HEAD_W),
                lambda lay, i, l=l, section=section: (jnp.where(lay == l, i, jnp.where(lay < l, 0, n_tiles - 1)), section)))
            args.append(z)
    out_spec = pl.BlockSpec((None, tm * N_HEADS, HEAD_DIM), lambda lay, i: (lay, i, 0))
    shape = jax.ShapeDtypeStruct((n_layers, n_rows * N_HEADS, HEAD_DIM), F32)
    return pl.pallas_call(
        functools.partial(_kv_pack_body, n_layers=n_layers),
        grid=(n_layers, n_tiles),
        in_specs=in_specs,
        out_specs=[out_spec, out_spec],
        out_shape=[shape, shape],
        compiler_params=_params("arbitrary", "arbitrary"),
        name="kv_pack",
    )(*args)


def _route_tables(eid, rank, counts, sub, max_blocks):
    flat = eid.reshape(-1)
    rank = rank.reshape(-1)
    n_assign = flat.shape[0]
    n_exp = counts.shape[0]
    nsub = (counts + sub - 1) // sub
    sub_end = jnp.cumsum(nsub)
    sub_start = sub_end - nsub
    pos = (sub_start[flat] * sub + rank).astype(I32)
    n_rows = (n_assign // sub + n_exp) * sub
    src = jnp.zeros((n_rows,), I32).at[pos].set(jnp.arange(n_assign, dtype=I32) // TOP_K)
    items = (nsub + max_blocks - 1) // max_blocks
    item_end = jnp.cumsum(items)
    item_start = item_end - items
    n_items_max = (n_rows // sub + n_exp * (max_blocks - 1)) // max_blocks
    w = jnp.arange(n_items_max, dtype=I32)
    item_expert = jnp.minimum(jnp.sum((item_end[None, :] <= w[:, None]).astype(I32), axis=1), n_exp - 1)
    k = w - item_start[item_expert]
    item_first = sub_start[item_expert] + k * max_blocks
    item_nb = jnp.clip(nsub[item_expert] - k * max_blocks, 0, max_blocks)
    counts2 = jnp.stack([item_end[-1], sub_end[-1]]).astype(I32)
    return pos, (item_expert.astype(I32), item_first.astype(I32), item_nb.astype(I32), counts2, src, n_rows)


def _row_tile_multiple(n, cap):
    best = ROW_TILE
    for m in range(ROW_TILE, cap + 1, ROW_TILE):
        if n % m == 0:
            best = m
    return best


def kernel(x_prompt, x_sample, cache_k, cache_v, cache_logf, page_table, norm_mix_g, w_in, b_f,
           q_norm_g, k_norm_g, sgu_norm_g, w_spatial, b_spatial, head_norm_g, w_out, norm_ffn_g,
           dense_w_gate, dense_w_up, dense_w_down, router_w, moe_w_gate, moe_w_up, moe_w_down):
    B, S, D = x_prompt.shape
    DB, DS, _ = x_sample.shape
    depth = w_in.shape[0]
    n_phys, page = cache_k.shape[1], cache_k.shape[2]
    chunk = w_spatial.shape[-1]
    n_exp = router_w.shape[-1]
    assert cache_k.shape[3] == N_HEADS and cache_k.shape[4] == HEAD_DIM
    assert w_in.shape[2] == N_SECTIONS * HEAD_W + N_HEADS and D == 2 * HEAD_W
    Tp, Ts = B * S, DB * DS
    T = Tp + Ts
    assert Ts == ROW_TILE and S % ROW_TILE == 0 and ROW_TILE % chunk == 0
    rows = N_HEADS * DS
    scale = HEAD_DIM ** -0.5
    f_lo = 3 * HEAD_W

    x = jnp.concatenate([x_prompt.reshape(Tp, D), x_sample.reshape(Ts, D)], axis=0)
    cache_lf2 = cache_logf.reshape(depth, n_phys, 1, page * N_HEADS)
    ones = jnp.ones((HEAD_W,), F32)

    outs = {name: [] for name in ("z", "fp", "ks", "vs", "fs", "ms")}
    for l in range(depth):
        w_main = jnp.concatenate([w_in[l][:, :f_lo], w_in[l][:, f_lo + N_HEADS:]], axis=1).astype(BF16)
        wf = jnp.pad(w_in[l][:, f_lo:f_lo + N_HEADS], ((0, 0), (0, HEAD_DIM - N_HEADS))).astype(BF16)
        bf = jnp.pad(b_f[l], (0, HEAD_DIM - N_HEADS)).reshape(1, HEAD_DIM)
        g_mix = norm_mix_g[l].reshape(1, D)
        gains = jnp.stack([(q_norm_g[l] * scale).reshape(-1), k_norm_g[l].reshape(-1), ones, ones,
                           sgu_norm_g[l].reshape(-1)]).reshape(N_SECTIONS, 1, HEAD_W)
        hg_att = head_norm_g[l][:N_HEADS]
        hg_mlp = head_norm_g[l][N_HEADS:].reshape(1, HEAD_W)
        wo = w_out[l].astype(BF16)
        g_ffn = norm_ffn_g[l].reshape(1, D)
        wm = jnp.tril(w_spatial[l])
        wc_p = wm.astype(BF16)
        bc_p = jnp.broadcast_to(b_spatial[l][:, :, None], (N_HEADS, chunk, HEAD_DIM))
        tok = jnp.arange(Ts, dtype=I32)
        pick = (tok[:, None] % DS == jnp.arange(DS, dtype=I32)[None, :]).astype(F32)
        same_seq = (tok[:, None] // DS == tok[None, :] // DS).astype(F32)
        wc_s = (jnp.einsum("ri,hij,cj->hrc", pick, wm[:, :DS, :DS], pick, precision=lax.Precision.HIGHEST)
                * same_seq).astype(BF16)
        bc_s = jnp.broadcast_to(jnp.tile(b_spatial[l][:, :DS], (1, DB))[:, :, None], (N_HEADS, Ts, HEAD_DIM))

        z, lf, c = _project(x, g_mix, w_main, wf, bf, gains, tm=_row_tile_multiple(T, 3 * ROW_TILE), seq=S)

        att_p = _fox_prompt(z, c, hg_att.reshape(N_HEADS, 1, HEAD_DIM), batch=B, seq=S, tq=ROW_TILE, hp=8)

        zs = z[Tp:]
        q_rows = zs[:, :HEAD_W].astype(BF16).reshape(DB, DS, N_HEADS, HEAD_DIM).transpose(0, 2, 1, 3).reshape(DB, rows, HEAD_DIM)
        kn = zs[:, HEAD_W:2 * HEAD_W].reshape(DB, rows, HEAD_DIM)
        vn = zs[:, 2 * HEAD_W:3 * HEAD_W].reshape(DB, rows, HEAD_DIM)
        lfs8 = lf[Tp:, :N_HEADS]
        att_rows = _fox_sample(l, page_table, q_rows, kn, vn, lfs8.reshape(DB, 1, rows), lfs8.reshape(DB, rows, 1),
                               jnp.repeat(hg_att, DS, axis=0), cache_k, cache_v, cache_lf2, pps=8)
        att_s = att_rows.reshape(DB, N_HEADS, DS, HEAD_DIM).transpose(0, 2, 1, 3).reshape(Ts, HEAD_W)

        moe_layer = l % 2 == 1
        i = l // 2
        if moe_layer:
            rw = jnp.pad(router_w[i], ((0, 0), (0, HEAD_DIM - n_exp))).astype(BF16)
            x, h, eid, gate, counts = _merge(att_p, att_s, z, wc_p, bc_p, wc_s, bc_s, hg_mlp, wo, x, g_ffn,
                                             h_dtype=F32, route_w=rw, n_exp=n_exp)
        else:
            x, h = _merge(att_p, att_s, z, wc_p, bc_p, wc_s, bc_s, hg_mlp, wo, x, g_ffn, h_dtype=BF16)

        if not moe_layer:
            x = _dense_ffn(h, dense_w_gate[i], dense_w_up[i], dense_w_down[i], x,
                           tm=_row_tile_multiple(T, 3 * ROW_TILE), tf=256)
        else:
            pos, tables = _route_tables(eid[:, :TOP_K], eid[:, TOP_K:2 * TOP_K], counts[0, :n_exp].astype(I32),
                                        MOE_SUB, MOE_ITEM_BLOCKS)
            y_sorted = _moe_ffn(tables, h, moe_w_gate[i], moe_w_up[i], moe_w_down[i], tf=256,
                                max_blocks=MOE_ITEM_BLOCKS)
            if l == depth - 1:
                x_split = _combine(pos, x, gate, y_sorted, tm=ROW_TILE, split_rows=Tp)
            else:
                x = _combine(pos, x, gate, y_sorted, tm=ROW_TILE)[0]

        outs["z"].append(z)
        outs["fp"].append(lf[:Tp, :N_HEADS].reshape(B, S, N_HEADS))
        outs["ks"].append(kn.reshape(DB, DS, N_HEADS, HEAD_DIM))
        outs["vs"].append(vn.reshape(DB, DS, N_HEADS, HEAD_DIM))
        outs["fs"].append(lfs8.reshape(DB, DS, N_HEADS))
        outs["ms"].append(zs[:, 4 * HEAD_W:].reshape(DB, DS, N_HEADS, HEAD_DIM))

    k_prompt, v_prompt = _kv_pack(outs["z"], Tp, tm=min(2 * ROW_TILE, S))
    if depth % 2 == 0:
        y_p, y_s = x_split
    else:
        y_p, y_s = x[:Tp], x[Tp:]
    return (y_p.reshape(B, S, D), y_s.reshape(DB, DS, D),
            k_prompt.reshape(depth, B, S, N_HEADS, HEAD_DIM), v_prompt.reshape(depth, B, S, N_HEADS, HEAD_DIM),
            jnp.stack(outs["fp"]),
            jnp.stack(outs["ks"]), jnp.stack(outs["vs"]), jnp.stack(outs["fs"]), jnp.stack(outs["ms"]))
```

```python
import functools

import jax
import jax.numpy as jnp
from jax import lax
from jax.experimental import pallas as pl
from jax.experimental.pallas import tpu as pltpu

F32 = jnp.float32
BF16 = jnp.bfloat16
I32 = jnp.int32

HEAD_DIM = 128
N_HEADS = 8
HEAD_W = N_HEADS * HEAD_DIM
N_SECTIONS = 5
EPS = 1e-6
NEG_INF = float("-inf")
TOP_K = 2
ROW_TILE = 256
MOE_SUB = ROW_TILE // 2
MOE_ITEM_BLOCKS = 20
VMEM_LIMIT = 56 * 1024 * 1024


def _params(*sem):
    return pltpu.CompilerParams(dimension_semantics=sem, vmem_limit_bytes=VMEM_LIMIT)


def _nt_dot(a, b):
    return lax.dot_general(a, b, (((1,), (1,)), ((), ())), preferred_element_type=F32)


def _rms(x):
    return x * lax.rsqrt(jnp.mean(x * x, axis=-1, keepdims=True) + EPS)


def _split3(x):
    x1 = x.astype(BF16).astype(F32)
    r1 = x - x1
    x2 = r1.astype(BF16).astype(F32)
    x3 = (r1 - x2).astype(BF16).astype(F32)
    return x1, x2, x3


def _pick_lane(x, lane_idx):
    lane = lax.broadcasted_iota(I32, x.shape, 1)
    return jnp.sum(jnp.where(lane == lane_idx, x, 0.0), axis=1, keepdims=True)


def _proj_body(x_ref, g_ref, w_ref, wf_ref, bf_ref, gain_ref, z_ref, lf_ref, c_ref, h_scr, carry_scr,
               *, chunk, chunks_per_seq):
    i = pl.program_id(0)
    j = pl.program_id(1)
    n_chunks = x_ref.shape[0] // chunk

    def section(jj):
        first = jj == 0
        normed = jj in (0, 1, 4)
        w = w_ref[...]
        if first:
            r = lax.broadcasted_iota(I32, (chunk, chunk), 0)
            c = lax.broadcasted_iota(I32, (chunk, chunk), 1)
            lower = (c <= r).astype(BF16)

            @pl.when(i == 0)
            def _():
                carry_scr[...] = jnp.zeros(carry_scr.shape, F32)

            carry = carry_scr[...]
        for n in range(n_chunks):
            rows = slice(n * chunk, (n + 1) * chunk)
            if first:
                hb = (_rms(x_ref[rows, :]) * g_ref[...]).astype(BF16)
                h_scr[rows, :] = hb
                f = jnp.dot(hb, wf_ref[...], preferred_element_type=F32) + bf_ref[...]
                lf = jnp.minimum(f, 0.0) - jnp.log1p(jnp.exp(-jnp.abs(f)))
                lf_ref[rows, :] = lf
                first_of_seq = (i * n_chunks + n) % chunks_per_seq == 0
                cs = jnp.where(first_of_seq, 0.0, carry)
                for part in _split3(lf):
                    cs = cs + jnp.dot(lower, part.astype(BF16), preferred_element_type=F32)
                c_ref[rows, :] = cs
                carry = cs[chunk - 1:chunk, :]
            else:
                hb = h_scr[rows, :]
            z = jnp.dot(hb, w, preferred_element_type=F32)
            if normed:
                for h in range(N_HEADS):
                    sl = slice(h * HEAD_DIM, (h + 1) * HEAD_DIM)
                    z_ref[rows, sl] = _rms(z[:, sl]) * gain_ref[:, sl]
            else:
                z_ref[rows, :] = z
        if first:
            carry_scr[...] = carry

    for jj in range(N_SECTIONS):
        pl.when(j == jj)(functools.partial(section, jj))


def _project(x, g, w_main, wf, bf, gains, *, tm, seq):
    T, D = x.shape
    row = lambda i, j: (i, 0)
    const = lambda i, j: (0, 0)
    return pl.pallas_call(
        functools.partial(_proj_body, chunk=ROW_TILE, chunks_per_seq=seq // ROW_TILE),
        grid=(T // tm, N_SECTIONS),
        in_specs=[
            pl.BlockSpec((tm, D), row),
            pl.BlockSpec((1, D), const),
            pl.BlockSpec((D, HEAD_W), lambda i, j: (0, j)),
            pl.BlockSpec((D, HEAD_DIM), const),
            pl.BlockSpec((1, HEAD_DIM), const),
            pl.BlockSpec((None, 1, HEAD_W), lambda i, j: (j, 0, 0)),
        ],
        out_specs=[pl.BlockSpec((tm, HEAD_W), lambda i, j: (i, j)),
                   pl.BlockSpec((tm, HEAD_DIM), row),
                   pl.BlockSpec((tm, HEAD_DIM), row)],
        out_shape=[
            jax.ShapeDtypeStruct((T, N_SECTIONS * HEAD_W), F32),
            jax.ShapeDtypeStruct((T, HEAD_DIM), F32),
            jax.ShapeDtypeStruct((T, HEAD_DIM), F32),
        ],
        scratch_shapes=[pltpu.VMEM((tm, D), BF16), pltpu.VMEM((1, HEAD_DIM), F32)],
        compiler_params=_params("arbitrary", "arbitrary"),
        name="in_proj",
    )(x, g, w_main, wf, bf, gains)


def _bias_lanes(c_col, key_side):
    lane = lax.broadcasted_iota(I32, (1, HEAD_DIM), 1)
    c1, c2, c3 = _split3(c_col)
    if key_side:
        terms = (1.0, 1.0, 1.0, -c1, -c2, -c3)
    else:
        terms = (c1, c2, c3, 1.0, 1.0, 1.0)
    out = jnp.zeros((c_col.shape[0], HEAD_DIM), F32)
    for idx, t in enumerate(terms):
        out = jnp.where(lane == idx, t, out)
    return out.astype(BF16)


def _fox_prompt_body(q_ref, k_ref, v_ref, cq_ref, ck_ref, hg_ref, o_ref, ka, vt, m_scr, acc_scr, *, tq, hp):
    hgrp = pl.program_id(1)
    qi = pl.program_id(2)
    n_kv = vt.shape[1]
    ext = vt.shape[2] - HEAD_DIM

    @pl.when(qi == 0)
    def _():
        row = lax.broadcasted_iota(I32, (ext, tq), 0)
        ones_ext = jnp.where(row == 0, 1.0, 0.0).astype(BF16)
        for hh in range(hp):
            sl = slice(hh * HEAD_DIM, (hh + 1) * HEAD_DIM)
            ka[hh, :, :HEAD_DIM] = k_ref[:, sl].astype(BF16)
            ka[hh, :, HEAD_DIM:] = _bias_lanes(_pick_lane(ck_ref[...], hgrp * hp + hh), True)
            for n in range(n_kv):
                vt[hh, n, :HEAD_DIM, :] = v_ref[n * tq:(n + 1) * tq, sl].T.astype(BF16)
                vt[hh, n, HEAD_DIM:, :] = ones_ext

    qa = []
    for hh in range(hp):
        sl = slice(hh * HEAD_DIM, (hh + 1) * HEAD_DIM)
        bias = _bias_lanes(_pick_lane(cq_ref[...], hgrp * hp + hh), False)
        qa.append(jnp.concatenate([q_ref[:, sl].astype(BF16), bias], axis=1))
        m_scr[hh] = jnp.full((1, tq), NEG_INF, F32)
        acc_scr[hh] = jnp.zeros(acc_scr.shape[1:], F32)

    def block(kj, diagonal):
        start = pl.multiple_of(kj * tq, tq)
        st = [_nt_dot(ka[hh, pl.ds(start, tq), :], qa[hh]) for hh in range(hp)]
        vts = [vt[hh, kj] for hh in range(hp)]
        ms = [m_scr[hh] for hh in range(hp)]
        accs = [acc_scr[hh] for hh in range(hp)]
        if diagonal:
            key = lax.broadcasted_iota(I32, (tq, tq), 0)
            qry = lax.broadcasted_iota(I32, (tq, tq), 1)
            st = [jnp.where(key <= qry, s, NEG_INF) for s in st]
        m_new = [jnp.maximum(m, jnp.max(s, axis=0, keepdims=True)) for m, s in zip(ms, st)]
        pt = [jnp.exp(s - mn).astype(BF16) for s, mn in zip(st, m_new)]
        pv = [jnp.dot(v, p, preferred_element_type=F32) for v, p in zip(vts, pt)]
        new_acc = [jnp.exp(m - mn) * a + d for m, mn, a, d in zip(ms, m_new, accs, pv)]
        for hh in range(hp):
            acc_scr[hh] = new_acc[hh]
            m_scr[hh] = m_new[hh]

    def body(kj, carry):
        block(kj, False)
        return carry

    lax.fori_loop(0, qi, body, 0)
    block(qi, True)
    for hh in range(hp):
        acc = acc_scr[hh]
        ot = acc[:HEAD_DIM, :] / acc[HEAD_DIM:HEAD_DIM + 1, :]
        ot = ot * lax.rsqrt(jnp.mean(ot * ot, axis=0, keepdims=True) + EPS)
        o_ref[:, hh * HEAD_DIM:(hh + 1) * HEAD_DIM] = (ot.T * hg_ref[hh]).astype(o_ref.dtype)


def _fox_prompt(z, c, hg, *, batch, seq, tq, hp):
    nq = seq // tq
    ngrp = N_HEADS // hp
    w = hp * HEAD_DIM
    ext = 16
    return pl.pallas_call(
        functools.partial(_fox_prompt_body, tq=tq, hp=hp),
        grid=(batch, ngrp, nq),
        in_specs=[
            pl.BlockSpec((tq, w), lambda b, h, i: (b * nq + i, h)),
            pl.BlockSpec((seq, w), lambda b, h, i: (b, ngrp + h)),
            pl.BlockSpec((seq, w), lambda b, h, i: (b, 2 * ngrp + h)),
            pl.BlockSpec((tq, HEAD_DIM), lambda b, h, i: (b * nq + i, 0)),
            pl.BlockSpec((seq, HEAD_DIM), lambda b, h, i: (b, 0)),
            pl.BlockSpec((hp, 1, HEAD_DIM), lambda b, h, i: (h, 0, 0)),
        ],
        out_specs=pl.BlockSpec((tq, w), lambda b, h, i: (b * nq + i, h)),
        out_shape=jax.ShapeDtypeStruct((batch * seq, HEAD_W), BF16),
        scratch_shapes=[pltpu.VMEM((hp, seq, 2 * HEAD_DIM), BF16),
                        pltpu.VMEM((hp, nq, HEAD_DIM + ext, tq), BF16),
                        pltpu.VMEM((hp, 1, tq), F32), pltpu.VMEM((hp, HEAD_DIM + ext, tq), F32)],
        compiler_params=_params("arbitrary", "arbitrary", "arbitrary"),
        name="fox_prompt",
    )(z, z, z, c, c, hg)


def _fox_sample_body(pt_ref, q_ref, kn_ref, vn_ref, lfr_ref, lfc_ref, hg_ref, ck_hbm, cv_hbm, *rest,
                     layer, pps, page, ring):
    lp = rest[:pps]
    o_ref, m_scr, l_scr, acc_scr, carry_scr, negq_scr, kbuf, vbuf, sem = rest[pps:]
    j = pl.program_id(1)
    n_j = pl.num_programs(1)
    rows = q_ref.shape[0]
    dsq = rows // N_HEADS
    cols = page * N_HEADS
    n_pages = n_j * pps

    g = pl.program_id(0) * n_j + j
    n_g = pl.num_programs(0) * n_j

    def fetch(gg):
        slot = gg % ring
        seq = gg // n_j
        first = n_pages - 1 - (gg % n_j) * pps
        for p in range(pps):
            pg = pt_ref[seq, first - p]
            pltpu.make_async_copy(ck_hbm.at[layer, pg], kbuf.at[slot, p], sem.at[slot]).start()
            pltpu.make_async_copy(cv_hbm.at[layer, pg], vbuf.at[slot, p], sem.at[slot]).start()

    @pl.when(g == 0)
    def _():
        for ahead in range(ring - 1):
            @pl.when(ahead < n_g)
            def _():
                fetch(ahead)

    @pl.when(g + ring - 1 < n_g)
    def _():
        fetch(g + ring - 1)

    slot = g % ring
    pltpu.make_async_copy(ck_hbm.at[layer, pl.ds(0, pps)], kbuf.at[slot], sem.at[slot]).wait()
    pltpu.make_async_copy(cv_hbm.at[layer, pl.ds(0, pps)], vbuf.at[slot], sem.at[slot]).wait()

    rn = lax.broadcasted_iota(I32, (rows, rows), 0)
    cn = lax.broadcasted_iota(I32, (rows, rows), 1)
    keep_new = ((rn // dsq) == (cn % N_HEADS)) & ((cn // N_HEADS) <= (rn % dsq))
    cq_col = jnp.sum(jnp.where(keep_new, lfr_ref[...], 0.0), axis=1, keepdims=True)

    @pl.when(j == 0)
    def _():
        m_scr[...] = jnp.full(m_scr.shape, NEG_INF, F32)
        l_scr[...] = jnp.zeros(l_scr.shape, F32)
        acc_scr[...] = jnp.zeros(acc_scr.shape, F32)
        carry_scr[...] = jnp.zeros(carry_scr.shape, F32)
        rp = lax.broadcasted_iota(I32, (rows, cols), 0)
        cp = lax.broadcasted_iota(I32, (rows, cols), 1)
        negq_scr[...] = jnp.where((rp // dsq) == (cp % N_HEADS), cq_col, NEG_INF)

    def update(scores, values):
        m, l, acc = m_scr[...], l_scr[...], acc_scr[...]
        mx = functools.reduce(jnp.maximum, scores)
        m_new = jnp.maximum(m, jnp.max(mx, axis=1, keepdims=True))
        alpha = jnp.exp(m - m_new)
        ps = [jnp.exp(s - m_new) for s in scores]
        pv = [jnp.dot(p.astype(BF16), v, preferred_element_type=F32) for p, v in zip(ps, values)]
        l_scr[...] = alpha * l + jnp.sum(functools.reduce(jnp.add, ps), axis=1, keepdims=True)
        acc_scr[...] = alpha * acc + functools.reduce(jnp.add, pv)
        m_scr[...] = m_new

    q = q_ref[...]
    raw = [_nt_dot(q, kbuf[slot, p].reshape(cols, HEAD_DIM).astype(BF16)) for p in range(pps)]

    lpp = jnp.concatenate([lp[p][...] for p in range(pps)], axis=0)
    lane = lax.broadcasted_iota(I32, (pps, cols), 1)
    sub = lax.broadcasted_iota(I32, (pps, cols), 0)
    n_steps = (page - 1).bit_length()
    tot = lpp
    for t in range(n_steps):
        tot = tot + pltpu.roll(tot, N_HEADS << t, axis=1)
    suf = jnp.where(lane < cols - N_HEADS, pltpu.roll(lpp, cols - N_HEADS, axis=1), 0.0)
    for t in range(n_steps):
        sh = N_HEADS << t
        suf = suf + jnp.where(lane < cols - sh, pltpu.roll(suf, cols - sh, axis=1), 0.0)
    pre = jnp.where(sub >= 1, pltpu.roll(tot, 1, axis=0), 0.0)
    for t in range((pps - 1).bit_length()):
        sh = 1 << t
        pre = pre + jnp.where(sub >= sh, pltpu.roll(pre, sh, axis=0), 0.0)
    carry = carry_scr[...]
    bias = suf + pre + carry

    negq = negq_scr[...]
    v2 = [vbuf[slot, p].reshape(cols, HEAD_DIM).astype(BF16) for p in range(pps)]
    update([raw[p] + negq + bias[p:p + 1, :] for p in range(pps)], v2)
    carry_scr[...] = carry + jnp.sum(tot, axis=0, keepdims=True)

    @pl.when(j == pl.num_programs(1) - 1)
    def _():
        cn_row = jnp.sum(jnp.where(((rn % N_HEADS) == (cn % N_HEADS)) & ((rn // N_HEADS) <= (cn // N_HEADS)),
                                   lfc_ref[...], 0.0), axis=0, keepdims=True)
        s = jnp.where(keep_new, _nt_dot(q, kn_ref[...].astype(BF16)) + cq_col - cn_row, NEG_INF)
        update([s], [vn_ref[...].astype(BF16)])
        o = acc_scr[...] / l_scr[...]
        o_ref[...] = (_rms(o) * hg_ref[...]).astype(o_ref.dtype)


def _fox_sample(layer, page_table, q, kn, vn, lfr, lfc, hg_rows, cache_k, cache_v, cache_lf2, *, pps):
    DB, n_pages = page_table.shape
    rows = q.shape[1]
    page = cache_k.shape[2]
    cols = page * N_HEADS
    n_steps = n_pages // pps

    def seq_map(b, j, pt):
        return (b, 0, 0)

    def lf_map(p):
        def f(b, j, pt):
            return (layer, pt[b, n_pages - 1 - (j * pps + p)], 0, 0)
        return f

    ring = 3
    grid_spec = pltpu.PrefetchScalarGridSpec(
        num_scalar_prefetch=1,
        grid=(DB, n_steps),
        in_specs=[
            pl.BlockSpec((None, rows, HEAD_DIM), seq_map),
            pl.BlockSpec((None, rows, HEAD_DIM), seq_map),
            pl.BlockSpec((None, rows, HEAD_DIM), seq_map),
            pl.BlockSpec((None, 1, rows), seq_map),
            pl.BlockSpec((None, rows, 1), seq_map),
            pl.BlockSpec((rows, HEAD_DIM), lambda b, j, pt: (0, 0)),
            pl.BlockSpec(memory_space=pl.ANY),
            pl.BlockSpec(memory_space=pl.ANY),
        ] + [pl.BlockSpec((None, None, 1, cols), lf_map(p)) for p in range(pps)],
        out_specs=pl.BlockSpec((None, rows, HEAD_DIM), seq_map),
        scratch_shapes=[pltpu.VMEM((rows, 1), F32), pltpu.VMEM((rows, 1), F32),
                        pltpu.VMEM((rows, HEAD_DIM), F32), pltpu.VMEM((1, cols), F32),
                        pltpu.VMEM((rows, cols), F32),
                        pltpu.VMEM((ring, pps, page, N_HEADS, HEAD_DIM), F32),
                        pltpu.VMEM((ring, pps, page, N_HEADS, HEAD_DIM), F32),
                        pltpu.SemaphoreType.DMA((ring,))],
    )
    return pl.pallas_call(
        functools.partial(_fox_sample_body, layer=layer, pps=pps, page=page, ring=ring),
        grid_spec=grid_spec,
        out_shape=jax.ShapeDtypeStruct((DB, rows, HEAD_DIM), BF16),
        compiler_params=_params("arbitrary", "arbitrary"),
        name="fox_sample",
    )(page_table, q, kn, vn, lfr, lfc, hg_rows, cache_k, cache_v, *([cache_lf2] * pps))


def _merge_body(attp_ref, atts_ref, u_ref, vm_ref, wcp_ref, bcp_ref, wcs_ref, bcs_ref, hg_ref, wo_ref,
                x_ref, g_ref, *rest, n_prompt_tiles, n_exp):
    if n_exp:
        rw_ref, xo_ref, ho_ref, eid_ref, gate_ref, cnt_ref, att_scr, mlp_scr = rest
    else:
        xo_ref, ho_ref, att_scr, mlp_scr = rest
    i = pl.program_id(0)
    tm = x_ref.shape[0]

    def gate_heads(wc_ref, bc_ref):
        chunk = wc_ref.shape[1]
        for h in range(N_HEADS):
            sl = slice(h * HEAD_DIM, (h + 1) * HEAD_DIM)
            for c in range(tm // chunk):
                rows = slice(c * chunk, (c + 1) * chunk)
                s = jnp.dot(wc_ref[h], vm_ref[rows, sl].astype(BF16), preferred_element_type=F32) + bc_ref[h]
                o = u_ref[rows, sl] * s
                mlp_scr[rows, sl] = (_rms(o) * hg_ref[:, sl]).astype(BF16)

    @pl.when(i < n_prompt_tiles)
    def _():
        att_scr[...] = attp_ref[...]
        gate_heads(wcp_ref, bcp_ref)

    @pl.when(i >= n_prompt_tiles)
    def _():
        att_scr[...] = atts_ref[...]
        gate_heads(wcs_ref, bcs_ref)

    y = jnp.dot(att_scr[...], wo_ref[:HEAD_W, :], preferred_element_type=F32)
    y = y + jnp.dot(mlp_scr[...], wo_ref[HEAD_W:, :], preferred_element_type=F32)
    xn = x_ref[...] + y
    xo_ref[...] = xn
    hn = _rms(xn) * g_ref[...]
    ho_ref[...] = hn.astype(ho_ref.dtype)
    if n_exp:
        _route_tile(hn.astype(BF16), rw_ref, eid_ref, gate_ref, cnt_ref, n_exp)


def _merge(att_p, att_s, z, wc_p, bc_p, wc_s, bc_s, hg_mlp, wo, x, g, *, h_dtype, route_w=None, n_exp=0):
    T, D = x.shape
    tm = ROW_TILE
    n_prompt_tiles = att_p.shape[0] // tm
    assert att_s.shape[0] == tm and wc_s.shape[1] == tm
    chunk = wc_p.shape[1]
    row = lambda i: (i, 0)
    c2 = lambda i: (0, 0)
    c3 = lambda i: (0, 0, 0)
    route_in = [] if route_w is None else [pl.BlockSpec((D, HEAD_DIM), c2)]
    route_args = [] if route_w is None else [route_w]
    route_out = [] if route_w is None else [pl.BlockSpec((tm, HEAD_DIM), row), pl.BlockSpec((tm, HEAD_DIM), row),
                                            pl.BlockSpec((1, HEAD_DIM), c2)]
    route_shape = [] if route_w is None else [jax.ShapeDtypeStruct((T, HEAD_DIM), I32),
                                              jax.ShapeDtypeStruct((T, HEAD_DIM), F32),
                                              jax.ShapeDtypeStruct((1, HEAD_DIM), F32)]
    return pl.pallas_call(
        functools.partial(_merge_body, n_prompt_tiles=n_prompt_tiles, n_exp=n_exp if route_w is not None else 0),
        grid=(T // tm,),
        in_specs=[
            pl.BlockSpec((tm, HEAD_W), lambda i: (jnp.minimum(i, n_prompt_tiles - 1), 0)),
            pl.BlockSpec((tm, HEAD_W), c2),
            pl.BlockSpec((tm, HEAD_W), lambda i: (i, 3)),
            pl.BlockSpec((tm, HEAD_W), lambda i: (i, 4)),
            pl.BlockSpec((N_HEADS, chunk, chunk), c3),
            pl.BlockSpec((N_HEADS, chunk, HEAD_DIM), c3),
            pl.BlockSpec((N_HEADS, tm, tm), c3),
            pl.BlockSpec((N_HEADS, tm, HEAD_DIM), c3),
            pl.BlockSpec((1, HEAD_W), c2),
            pl.BlockSpec((2 * HEAD_W, D), c2),
            pl.BlockSpec((tm, D), row),
            pl.BlockSpec((1, D), c2),
        ] + route_in,
        out_specs=[pl.BlockSpec((tm, D), row), pl.BlockSpec((tm, D), row)] + route_out,
        out_shape=[jax.ShapeDtypeStruct((T, D), F32), jax.ShapeDtypeStruct((T, D), h_dtype)] + route_shape,
        scratch_shapes=[pltpu.VMEM((tm, HEAD_W), BF16), pltpu.VMEM((tm, HEAD_W), BF16)],
        compiler_params=_params("arbitrary"),
        name="mlp_merge_out",
    )(att_p, att_s, z, z, wc_p, bc_p, wc_s, bc_s, hg_mlp, wo, x, g, *route_args)


def _dense_body(h_ref, wg_ref, wu_ref, wd_ref, *rest, has_tail):
    if has_tail:
        wgt_ref, wut_ref, wdt_ref, x_ref, o_ref = rest
    else:
        x_ref, o_ref = rest
    f = pl.program_id(1)

    @pl.when(f == 0)
    def _():
        o_ref[...] = x_ref[...]

    n_chunks = h_ref.shape[0] // ROW_TILE

    def swiglu(wg, wu, wd):
        wgv, wuv, wdv = (ref[...].astype(BF16) for ref in (wg, wu, wd))
        hs = [h_ref[n * ROW_TILE:(n + 1) * ROW_TILE, :] for n in range(n_chunks)]
        ab = [(jnp.dot(h, wgv, preferred_element_type=F32), jnp.dot(h, wuv, preferred_element_type=F32)) for h in hs]
        gs = [(a * jax.nn.sigmoid(a) * b).astype(BF16) for a, b in ab]
        ds = [jnp.dot(g, wdv, preferred_element_type=F32) for g in gs]
        for n, d in enumerate(ds):
            o_ref[n * ROW_TILE:(n + 1) * ROW_TILE, :] += d

    swiglu(wg_ref, wu_ref, wd_ref)

    if has_tail:
        @pl.when(f == pl.num_programs(1) - 1)
        def _():
            swiglu(wgt_ref, wut_ref, wdt_ref)


def _dense_ffn(h, wg, wu, wd, x, *, tm, tf):
    T, D = x.shape
    d_ff = wg.shape[1]
    n_main = d_ff // tf
    tail = d_ff - n_main * tf
    assert tail % HEAD_DIM == 0 and (n_main * tf) % max(tail, 1) == 0
    in_specs = [
        pl.BlockSpec((tm, D), lambda i, f: (i, 0)),
        pl.BlockSpec((D, tf), lambda i, f: (0, f)),
        pl.BlockSpec((D, tf), lambda i, f: (0, f)),
        pl.BlockSpec((tf, D), lambda i, f: (f, 0)),
    ]
    args = [h, wg, wu, wd]
    if tail:
        t_idx = n_main * tf // tail
        in_specs += [pl.BlockSpec((D, tail), lambda i, f: (0, t_idx)),
                     pl.BlockSpec((D, tail), lambda i, f: (0, t_idx)),
                     pl.BlockSpec((tail, D), lambda i, f: (t_idx, 0))]
        args += [wg, wu, wd]
    in_specs.append(pl.BlockSpec((tm, D), lambda i, f: (i, 0)))
    args.append(x)
    return pl.pallas_call(
        functools.partial(_dense_body, has_tail=bool(tail)),
        grid=(T // tm, n_main),
        in_specs=in_specs,
        out_specs=pl.BlockSpec((tm, D), lambda i, f: (i, 0)),
        out_shape=jax.ShapeDtypeStruct((T, D), F32),
        compiler_params=_params("arbitrary", "arbitrary"),
        name="dense_ffn",
    )(*args)


def _route_tile(hb, w_ref, eid_ref, gate_ref, cnt_ref, n_exp):
    tm = hb.shape[0]
    logits = jnp.dot(hb, w_ref[...], preferred_element_type=F32)
    lane = lax.broadcasted_iota(I32, logits.shape, 1)
    big = logits.shape[1]
    l1 = jnp.where(lane < n_exp, logits, NEG_INF)
    m1 = jnp.max(l1, axis=1, keepdims=True)
    i1 = jnp.min(jnp.where(l1 == m1, lane, big), axis=1, keepdims=True)
    l2 = jnp.where(lane == i1, NEG_INF, l1)
    m2 = jnp.max(l2, axis=1, keepdims=True)
    i2 = jnp.min(jnp.where(l2 == m2, lane, big), axis=1, keepdims=True)
    e2 = jnp.exp(m2 - m1)
    p1 = 1.0 / (1.0 + e2)
    p2 = e2 / (1.0 + e2)
    gate_ref[...] = jnp.where(lane == 0, p1, jnp.where(lane == 1, p2, 0.0))

    @pl.when(pl.program_id(0) == 0)
    def _():
        cnt_ref[...] = jnp.zeros(cnt_ref.shape, F32)

    hit1 = (lane == i1).astype(F32)
    hit2 = (lane == i2).astype(F32)
    r = lax.broadcasted_iota(I32, (tm, tm), 0)
    c = lax.broadcasted_iota(I32, (tm, tm), 1)
    before = jnp.dot((c < r).astype(BF16), (hit1 + hit2).astype(BF16), preferred_element_type=F32)
    before = before + cnt_ref[...]
    rank1 = jnp.sum(hit1 * before, axis=1, keepdims=True).astype(I32)
    rank2 = jnp.sum(hit2 * before, axis=1, keepdims=True).astype(I32)
    cnt_ref[...] += jnp.sum(hit1 + hit2, axis=0, keepdims=True)
    eid_ref[...] = jnp.where(lane == 0, i1, jnp.where(lane == 1, i2,
                             jnp.where(lane == TOP_K, rank1, jnp.where(lane == TOP_K + 1, rank2, 0))))


def _row_copy(src_hbm, row, dst_vmem, r, sem):
    return pltpu.make_async_copy(src_hbm.at[pl.ds(row, 1)], dst_vmem.at[pl.ds(r, 1)], sem)


def _moe_body(ie_ref, ifirst_ref, inb_ref, cnt_ref, src_ref,
              h_hbm, wg_ref, wu_ref, wd_ref, y_hbm,
              xbuf, acc, stage, sem_in, sem_out, *, sub, unroll):
    del ie_ref
    w = pl.program_id(0)
    f = pl.program_id(1)
    n_f = pl.num_programs(1)
    n_items, n_used = cnt_ref[0], cnt_ref[1]
    n_total = y_hbm.shape[0] // sub

    def block_rows(r):
        return pl.ds(pl.multiple_of(r * sub, sub), sub)

    def out_copy(r_local, r_global):
        return pltpu.make_async_copy(acc.at[block_rows(r_local)], y_hbm.at[block_rows(r_global)], sem_out)

    def drain_out(n):
        def wait(r, carry):
            out_copy(r, r).wait()
            return carry

        lax.fori_loop(0, n, wait, 0)

    @pl.when((w == 0) & (f == 0))
    def _():
        acc[block_rows(0)] = jnp.zeros((sub, acc.shape[1]), F32)

        def fill(r, carry):
            out_copy(0, r).start()
            return carry

        def drain(r, carry):
            out_copy(0, r).wait()
            return carry

        lax.fori_loop(n_used, n_total, fill, 0)
        lax.fori_loop(n_used, n_total, drain, 0)

    @pl.when(w < n_items)
    def _():
        first = ifirst_ref[w]
        nb = inb_ref[w]

        @pl.when(f == 0)
        def _():
            def issue(blk, slot):
                base = (first + blk) * sub

                def body(rr, carry):
                    for u in range(unroll):
                        r = rr * unroll + u
                        _row_copy(h_hbm, src_ref[base + r], stage.at[slot], r, sem_in.at[slot]).start(priority=1)
                    return carry

                lax.fori_loop(0, sub // unroll, body, 0)

            n_slots = stage.shape[0]
            ahead = n_slots - 1
            for b0 in range(ahead):
                @pl.when(b0 < nb)
                def _():
                    issue(b0, b0)

            @pl.when(w > 0)
            def _():
                drain_out(inb_ref[w - 1])

            def per_block(blk, carry):
                slot = blk % n_slots

                @pl.when(blk + ahead < nb)
                def _():
                    issue(blk + ahead, (blk + ahead) % n_slots)

                pltpu.make_async_copy(h_hbm.at[pl.ds(0, sub)], stage.at[slot], sem_in.at[slot]).wait()
                xbuf[block_rows(blk)] = stage[slot].astype(BF16)
                acc[block_rows(blk)] = jnp.zeros((sub, acc.shape[1]), F32)
                return carry

            lax.fori_loop(0, nb, per_block, 0)

        def swiglu(chunks):
            spans = [pl.ds(start if isinstance(start, int) else pl.multiple_of(start, sub), size)
                     for start, size in chunks]
            xs = [xbuf[s] for s in spans]
            wgv, wuv, wdv = (ref[...].astype(BF16) for ref in (wg_ref, wu_ref, wd_ref))
            ab = [(jnp.dot(x, wgv, preferred_element_type=F32),
                   jnp.dot(x, wuv, preferred_element_type=F32)) for x in xs]
            hs = [(a * jax.nn.sigmoid(a) * b).astype(BF16) for a, b in ab]
            ds = [jnp.dot(hh, wdv, preferred_element_type=F32) for hh in hs]
            for s, d in zip(spans, ds):
                acc[s] += d

        wide = 2 * sub
        n_wide = nb // 2

        for rp in range(xbuf.shape[0] // (2 * wide)):
            @pl.when(rp < n_wide // 2)
            def _():
                swiglu([(2 * rp * wide, wide), ((2 * rp + 1) * wide, wide)])

        @pl.when(n_wide % 2 == 1)
        def _():
            swiglu([((n_wide - 1) * wide, wide)])

        @pl.when(nb % 2 == 1)
        def _():
            swiglu([((nb - 1) * sub, sub)])

        @pl.when(f == n_f - 1)
        def _():
            def start(r, carry):
                out_copy(r, first + r).start()
                return carry

            lax.fori_loop(0, nb, start, 0)

            @pl.when(w == n_items - 1)
            def _():
                drain_out(nb)


def _moe_ffn(tables, h, wg, wu, wd, *, tf, max_blocks):
    item_expert, item_first, item_nb, counts, src, n_rows = tables
    T, D = h.shape
    sub = MOE_SUB
    n_f = wg.shape[2] // tf
    last = n_f - 1

    def live(w, cnt):
        return jnp.minimum(w, cnt[0] - 1)

    def fidx(w, f, cnt):
        return jnp.where(w < cnt[0], f, last)

    grid_spec = pltpu.PrefetchScalarGridSpec(
        num_scalar_prefetch=5,
        grid=(item_expert.shape[0], n_f),
        in_specs=[
            pl.BlockSpec(memory_space=pl.ANY),
            pl.BlockSpec((None, D, tf), lambda w, f, ie, i1, i2, cnt, s: (ie[live(w, cnt)], 0, fidx(w, f, cnt))),
            pl.BlockSpec((None, D, tf), lambda w, f, ie, i1, i2, cnt, s: (ie[live(w, cnt)], 0, fidx(w, f, cnt))),
            pl.BlockSpec((None, tf, D), lambda w, f, ie, i1, i2, cnt, s: (ie[live(w, cnt)], fidx(w, f, cnt), 0)),
        ],
        out_specs=pl.BlockSpec(memory_space=pl.ANY),
        scratch_shapes=[
            pltpu.VMEM((max_blocks * sub, D), BF16),
            pltpu.VMEM((max_blocks * sub, D), F32),
            pltpu.VMEM((4, sub, D), F32),
            pltpu.SemaphoreType.DMA((4,)), pltpu.SemaphoreType.DMA(()),
        ],
    )
    return pl.pallas_call(
        functools.partial(_moe_body, sub=sub, unroll=8),
        grid_spec=grid_spec,
        out_shape=jax.ShapeDtypeStruct((n_rows, D), F32),
        compiler_params=_params("arbitrary", "arbitrary"),
        name="moe_ffn",
    )(item_expert, item_first, item_nb, counts, src, h, wg, wu, wd)


def _combine_body(pos_ref, x_ref, gate_ref, y_hbm, *rest, tm, unroll, split_tile):
    o_refs, (buf, sem) = rest[:-2], rest[-2:]
    i = pl.program_id(0)
    slot = i % 2

    def gather(tile, to_slot):
        base = tile * tm * TOP_K

        def issue(rr, carry):
            for u in range(unroll):
                r = rr * unroll + u
                for k in range(TOP_K):
                    _row_copy(y_hbm, pos_ref[base + r * TOP_K + k], buf.at[to_slot, k], r,
                              sem.at[to_slot, k]).start(priority=k)
            return carry

        lax.fori_loop(0, tm // unroll, issue, 0)

    @pl.when(i == 0)
    def _():
        gather(0, 0)

    @pl.when(i + 1 < pl.num_programs(0))
    def _():
        gather(i + 1, 1 - slot)

    out = x_ref[...]
    for k in range(TOP_K):
        pltpu.make_async_copy(y_hbm.at[pl.ds(0, tm)], buf.at[slot, k], sem.at[slot, k]).wait()
        out = out + gate_ref[:, k:k + 1] * buf[slot, k]
    if split_tile is None:
        o_refs[0][...] = out
    else:
        @pl.when(pl.program_id(0) < split_tile)
        def _():
            o_refs[0][...] = out

        @pl.when(pl.program_id(0) >= split_tile)
        def _():
            o_refs[1][...] = out


def _combine(pos, x, gate, y, *, tm, split_rows=None):
    T, D = x.shape
    if split_rows is None:
        split_tile = None
        out_specs = [pl.BlockSpec((tm, D), lambda i, p: (i, 0))]
        out_shape = [jax.ShapeDtypeStruct((T, D), F32)]
    else:
        split_tile = split_rows // tm
        out_specs = [pl.BlockSpec((tm, D), lambda i, p: (jnp.minimum(i, split_tile - 1), 0)),
                     pl.BlockSpec((tm, D), lambda i, p: (jnp.maximum(i - split_tile, 0), 0))]
        out_shape = [jax.ShapeDtypeStruct((split_rows, D), F32), jax.ShapeDtypeStruct((T - split_rows, D), F32)]
    grid_spec = pltpu.PrefetchScalarGridSpec(
        num_scalar_prefetch=1,
        grid=(T // tm,),
        in_specs=[pl.BlockSpec((tm, D), lambda i, p: (i, 0)),
                  pl.BlockSpec((tm, HEAD_DIM), lambda i, p: (i, 0)),
                  pl.BlockSpec(memory_space=pl.ANY)],
        out_specs=out_specs,
        scratch_shapes=[pltpu.VMEM((2, TOP_K, tm, D), F32), pltpu.SemaphoreType.DMA((2, TOP_K))],
    )
    return pl.pallas_call(
        functools.partial(_combine_body, tm=tm, unroll=4, split_tile=split_tile),
        grid_spec=grid_spec,
        out_shape=out_shape,
        compiler_params=_params("arbitrary"),
        name="combine_rows",
    )(pos, x, gate, y)


def _kv_pack_body(*refs, n_layers):
    z_refs, (ko_ref, vo_ref) = refs[:2 * n_layers], refs[2 * n_layers:]
    layer = pl.program_id(0)
    tm = z_refs[0].shape[0]
    for l in range(n_layers):
        @pl.when(layer == l)
        def _():
            for src, dst in ((z_refs[2 * l], ko_ref), (z_refs[2 * l + 1], vo_ref)):
                for h in range(N_HEADS):
                    dst[pl.ds(h, tm, stride=N_HEADS), :] = src[:, h * HEAD_DIM:(h + 1) * HEAD_DIM]


def _kv_pack(zs, n_rows, *, tm):
    n_layers = len(zs)
    n_tiles = n_rows // tm
    in_specs, args = [], []
    for l, z in enumerate(zs):
        for section in (1, 2):
            in_specs.append(pl.BlockSpec(
                (tm, HEAD_W),
                lambda lay, i, l=l, section=section: (jnp.where(lay == l, i, jnp.where(lay < l, 0, n_tiles - 1)), section)))
            args.append(z)
    out_spec = pl.BlockSpec((None, tm * N_HEADS, HEAD_DIM), lambda lay, i: (lay, i, 0))
    shape = jax.ShapeDtypeStruct((n_layers, n_rows * N_HEADS, HEAD_DIM), F32)
    return pl.pallas_call(
        functools.partial(_kv_pack_body, n_layers=n_layers),
        grid=(n_layers, n_tiles),
        in_specs=in_specs,
        out_specs=[out_spec, out_spec],
        out_shape=[shape, shape],
        compiler_params=_params("arbitrary", "arbitrary"),
        name="kv_pack",
    )(*args)


def _route_tables(eid, rank, counts, sub, max_blocks):
    flat = eid.reshape(-1)
    rank = rank.reshape(-1)
    n_assign = flat.shape[0]
    n_exp = counts.shape[0]
    nsub = (counts + sub - 1) // sub
    sub_end = jnp.cumsum(nsub)
    sub_start = sub_end - nsub
    pos = (sub_start[flat] * sub + rank).astype(I32)
    n_rows = (n_assign // sub + n_exp) * sub
    src = jnp.zeros((n_rows,), I32).at[pos].set(jnp.arange(n_assign, dtype=I32) // TOP_K)
    items = (nsub + max_blocks - 1) // max_blocks
    item_end = jnp.cumsum(items)
    item_start = item_end - items
    n_items_max = (n_rows // sub + n_exp * (max_blocks - 1)) // max_blocks
    w = jnp.arange(n_items_max, dtype=I32)
    item_expert = jnp.minimum(jnp.sum((item_end[None, :] <= w[:, None]).astype(I32), axis=1), n_exp - 1)
    k = w - item_start[item_expert]
    item_first = sub_start[item_expert] + k * max_blocks
    item_nb = jnp.clip(nsub[item_expert] - k * max_blocks, 0, max_blocks)
    counts2 = jnp.stack([item_end[-1], sub_end[-1]]).astype(I32)
    return pos, (item_expert.astype(I32), item_first.astype(I32), item_nb.astype(I32), counts2, src, n_rows)


def _row_tile_multiple(n, cap):
    best = ROW_TILE
    for m in range(ROW_TILE, cap + 1, ROW_TILE):
        if n % m == 0:
            best = m
    return best


def kernel(x_prompt, x_sample, cache_k, cache_v, cache_logf, page_table, norm_mix_g, w_in, b_f,
           q_norm_g, k_norm_g, sgu_norm_g, w_spatial, b_spatial, head_norm_g, w_out, norm_ffn_g,
           dense_w_gate, dense_w_up, dense_w_down, router_w, moe_w_gate, moe_w_up, moe_w_down):
    B, S, D = x_prompt.shape
    DB, DS, _ = x_sample.shape
    depth = w_in.shape[0]
    n_phys, page = cache_k.shape[1], cache_k.shape[2]
    chunk = w_spatial.shape[-1]
    n_exp = router_w.shape[-1]
    assert cache_k.shape[3] == N_HEADS and cache_k.shape[4] == HEAD_DIM
    assert w_in.shape[2] == N_SECTIONS * HEAD_W + N_HEADS and D == 2 * HEAD_W
    Tp, Ts = B * S, DB * DS
    T = Tp + Ts
    assert Ts == ROW_TILE and S % ROW_TILE == 0 and ROW_TILE % chunk == 0
    rows = N_HEADS * DS
    scale = HEAD_DIM ** -0.5
    f_lo = 3 * HEAD_W

    x = jnp.concatenate([x_prompt.reshape(Tp, D), x_sample.reshape(Ts, D)], axis=0)
    cache_lf2 = cache_logf.reshape(depth, n_phys, 1, page * N_HEADS)
    ones = jnp.ones((HEAD_W,), F32)

    outs = {name: [] for name in ("z", "fp", "ks", "vs", "fs", "ms")}
    for l in range(depth):
        w_main = jnp.concatenate([w_in[l][:, :f_lo], w_in[l][:, f_lo + N_HEADS:]], axis=1).astype(BF16)
        wf = jnp.pad(w_in[l][:, f_lo:f_lo + N_HEADS], ((0, 0), (0, HEAD_DIM - N_HEADS))).astype(BF16)
        bf = jnp.pad(b_f[l], (0, HEAD_DIM - N_HEADS)).reshape(1, HEAD_DIM)
        g_mix = norm_mix_g[l].reshape(1, D)
        gains = jnp.stack([(q_norm_g[l] * scale).reshape(-1), k_norm_g[l].reshape(-1), ones, ones,
                           sgu_norm_g[l].reshape(-1)]).reshape(N_SECTIONS, 1, HEAD_W)
        hg_att = head_norm_g[l][:N_HEADS]
        hg_mlp = head_norm_g[l][N_HEADS:].reshape(1, HEAD_W)
        wo = w_out[l].astype(BF16)
        g_ffn = norm_ffn_g[l].reshape(1, D)
        wm = jnp.tril(w_spatial[l])
        wc_p = wm.astype(BF16)
        bc_p = jnp.broadcast_to(b_spatial[l][:, :, None], (N_HEADS, chunk, HEAD_DIM))
        tok = jnp.arange(Ts, dtype=I32)
        pick = (tok[:, None] % DS == jnp.arange(DS, dtype=I32)[None, :]).astype(F32)
        same_seq = (tok[:, None] // DS == tok[None, :] // DS).astype(F32)
        wc_s = (jnp.einsum("ri,hij,cj->hrc", pick, wm[:, :DS, :DS], pick, precision=lax.Precision.HIGHEST)
                * same_seq).astype(BF16)
        bc_s = jnp.broadcast_to(jnp.tile(b_spatial[l][:, :DS], (1, DB))[:, :, None], (N_HEADS, Ts, HEAD_DIM))

        z, lf, c = _project(x, g_mix, w_main, wf, bf, gains, tm=_row_tile_multiple(T, 3 * ROW_TILE), seq=S)

        att_p = _fox_prompt(z, c, hg_att.reshape(N_HEADS, 1, HEAD_DIM), batch=B, seq=S, tq=ROW_TILE, hp=8)

        zs = z[Tp:]
        q_rows = zs[:, :HEAD_W].astype(BF16).reshape(DB, DS, N_HEADS, HEAD_DIM).transpose(0, 2, 1, 3).reshape(DB, rows, HEAD_DIM)
        kn = zs[:, HEAD_W:2 * HEAD_W].reshape(DB, rows, HEAD_DIM)
        vn = zs[:, 2 * HEAD_W:3 * HEAD_W].reshape(DB, rows, HEAD_DIM)
        lfs8 = lf[Tp:, :N_HEADS]
        att_rows = _fox_sample(l, page_table, q_rows, kn, vn, lfs8.reshape(DB, 1, rows), lfs8.reshape(DB, rows, 1),
                               jnp.repeat(hg_att, DS, axis=0), cache_k, cache_v, cache_lf2, pps=8)
        att_s = att_rows.reshape(DB, N_HEADS, DS, HEAD_DIM).transpose(0, 2, 1, 3).reshape(Ts, HEAD_W)

        moe_layer = l % 2 == 1
        i = l // 2
        if moe_layer:
            rw = jnp.pad(router_w[i], ((0, 0), (0, HEAD_DIM - n_exp))).astype(BF16)
            x, h, eid, gate, counts = _merge(att_p, att_s, z, wc_p, bc_p, wc_s, bc_s, hg_mlp, wo, x, g_ffn,
                                             h_dtype=F32, route_w=rw, n_exp=n_exp)
        else:
            x, h = _merge(att_p, att_s, z, wc_p, bc_p, wc_s, bc_s, hg_mlp, wo, x, g_ffn, h_dtype=BF16)

        if not moe_layer:
            x = _dense_ffn(h, dense_w_gate[i], dense_w_up[i], dense_w_down[i], x,
                           tm=_row_tile_multiple(T, 3 * ROW_TILE), tf=256)
        else:
            pos, tables = _route_tables(eid[:, :TOP_K], eid[:, TOP_K:2 * TOP_K], counts[0, :n_exp].astype(I32),
                                        MOE_SUB, MOE_ITEM_BLOCKS)
            y_sorted = _moe_ffn(tables, h, moe_w_gate[i], moe_w_up[i], moe_w_down[i], tf=256,
                                max_blocks=MOE_ITEM_BLOCKS)
            if l == depth - 1:
                x_split = _combine(pos, x, gate, y_sorted, tm=ROW_TILE, split_rows=Tp)
            else:
                x = _combine(pos, x, gate, y_sorted, tm=ROW_TILE)[0]

        outs["z"].append(z)
        outs["fp"].append(lf[:Tp, :N_HEADS].reshape(B, S, N_HEADS))
        outs["ks"].append(kn.reshape(DB, DS, N_HEADS, HEAD_DIM))
        outs["vs"].append(vn.reshape(DB, DS, N_HEADS, HEAD_DIM))
        outs["fs"].append(lfs8.reshape(DB, DS, N_HEADS))
        outs["ms"].append(zs[:, 4 * HEAD_W:].reshape(DB, DS, N_HEADS, HEAD_DIM))

    k_prompt, v_prompt = _kv_pack(outs["z"], Tp, tm=min(2 * ROW_TILE, S))
    if depth % 2 == 0:
        y_p, y_s = x_split
    else:
        y_p, y_s = x[:Tp], x[Tp:]
    return (y_p.reshape(B, S, D), y_s.reshape(DB, DS, D),
            k_prompt.reshape(depth, B, S, N_HEADS, HEAD_DIM), v_prompt.reshape(depth, B, S, N_HEADS, HEAD_DIM),
            jnp.stack(outs["fp"]),
            jnp.stack(outs["ks"]), jnp.stack(outs["vs"]), jnp.stack(outs["fs"]), jnp.stack(outs["ms"]))
```

```python
import functools

import jax
import jax.numpy as jnp
from jax import lax
from jax.experimental import pallas as pl
from jax.experimental.pallas import tpu as pltpu

F32 = jnp.float32
BF16 = jnp.bfloat16
I32 = jnp.int32

HEAD_DIM = 128
N_HEADS = 8
HEAD_W = N_HEADS * HEAD_DIM
N_SECTIONS = 5
EPS = 1e-6
NEG_INF = float("-inf")
TOP_K = 2
ROW_TILE = 256
MOE_SUB = ROW_TILE // 2
MOE_ITEM_BLOCKS = 20
VMEM_LIMIT = 56 * 1024 * 1024


def _params(*sem):
    return pltpu.CompilerParams(dimension_semantics=sem, vmem_limit_bytes=VMEM_LIMIT)


def _nt_dot(a, b):
    return lax.dot_general(a, b, (((1,), (1,)), ((), ())), preferred_element_type=F32)


def _rms(x):
    return x * lax.rsqrt(jnp.mean(x * x, axis=-1, keepdims=True) + EPS)


def _split3(x):
    x1 = x.astype(BF16).astype(F32)
    r1 = x - x1
    x2 = r1.astype(BF16).astype(F32)
    x3 = (r1 - x2).astype(BF16).astype(F32)
    return x1, x2, x3


def _pick_lane(x, lane_idx):
    lane = lax.broadcasted_iota(I32, x.shape, 1)
    return jnp.sum(jnp.where(lane == lane_idx, x, 0.0), axis=1, keepdims=True)


def _proj_body(x_ref, g_ref, w_ref, wf_ref, bf_ref, gain_ref, z_ref, lf_ref, c_ref, h_scr, carry_scr,
               *, chunk, chunks_per_seq):
    i = pl.program_id(0)
    j = pl.program_id(1)
    n_chunks = x_ref.shape[0] // chunk

    def section(jj):
        first = jj == 0
        normed = jj in (0, 1, 4)
        w = w_ref[...]
        if first:
            r = lax.broadcasted_iota(I32, (chunk, chunk), 0)
            c = lax.broadcasted_iota(I32, (chunk, chunk), 1)
            lower = (c <= r).astype(BF16)

            @pl.when(i == 0)
            def _():
                carry_scr[...] = jnp.zeros(carry_scr.shape, F32)

            carry = carry_scr[...]
        for n in range(n_chunks):
            rows = slice(n * chunk, (n + 1) * chunk)
            if first:
                hb = (_rms(x_ref[rows, :]) * g_ref[...]).astype(BF16)
                h_scr[rows, :] = hb
                f = jnp.dot(hb, wf_ref[...], preferred_element_type=F32) + bf_ref[...]
                lf = jnp.minimum(f, 0.0) - jnp.log1p(jnp.exp(-jnp.abs(f)))
                lf_ref[rows, :] = lf
                first_of_seq = (i * n_chunks + n) % chunks_per_seq == 0
                cs = jnp.where(first_of_seq, 0.0, carry)
                for part in _split3(lf):
                    cs = cs + jnp.dot(lower, part.astype(BF16), preferred_element_type=F32)
                c_ref[rows, :] = cs
                carry = cs[chunk - 1:chunk, :]
            else:
                hb = h_scr[rows, :]
            z = jnp.dot(hb, w, preferred_element_type=F32)
            if normed:
                for h in range(N_HEADS):
                    sl = slice(h * HEAD_DIM, (h + 1) * HEAD_DIM)
                    z_ref[rows, sl] = _rms(z[:, sl]) * gain_ref[:, sl]
            else:
                z_ref[rows, :] = z
        if first:
            carry_scr[...] = carry

    for jj in range(N_SECTIONS):
        pl.when(j == jj)(functools.partial(section, jj))


def _project(x, g, w_main, wf, bf, gains, *, tm, seq):
    T, D = x.shape
    row = lambda i, j: (i, 0)
    const = lambda i, j: (0, 0)
    return pl.pallas_call(
        functools.partial(_proj_body, chunk=ROW_TILE, chunks_per_seq=seq // ROW_TILE),
        grid=(T // tm, N_SECTIONS),
        in_specs=[
            pl.BlockSpec((tm, D), row),
            pl.BlockSpec((1, D), const),
            pl.BlockSpec((D, HEAD_W), lambda i, j: (0, j)),
            pl.BlockSpec((D, HEAD_DIM), const),
            pl.BlockSpec((1, HEAD_DIM), const),
            pl.BlockSpec((None, 1, HEAD_W), lambda i, j: (j, 0, 0)),
        ],
        out_specs=[pl.BlockSpec((tm, HEAD_W), lambda i, j: (i, j)),
                   pl.BlockSpec((tm, HEAD_DIM), row),
                   pl.BlockSpec((tm, HEAD_DIM), row)],
        out_shape=[
            jax.ShapeDtypeStruct((T, N_SECTIONS * HEAD_W), F32),
            jax.ShapeDtypeStruct((T, HEAD_DIM), F32),
            jax.ShapeDtypeStruct((T, HEAD_DIM), F32),
        ],
        scratch_shapes=[pltpu.VMEM((tm, D), BF16), pltpu.VMEM((1, HEAD_DIM), F32)],
        compiler_params=_params("arbitrary", "arbitrary"),
        name="in_proj",
    )(x, g, w_main, wf, bf, gains)


def _bias_lanes(c_col, key_side):
    lane = lax.broadcasted_iota(I32, (1, HEAD_DIM), 1)
    c1, c2, c3 = _split3(c_col)
    if key_side:
        terms = (1.0, 1.0, 1.0, -c1, -c2, -c3)
    else:
        terms = (c1, c2, c3, 1.0, 1.0, 1.0)
    out = jnp.zeros((c_col.shape[0], HEAD_DIM), F32)
    for idx, t in enumerate(terms):
        out = jnp.where(lane == idx, t, out)
    return out.astype(BF16)


def _fox_prompt_body(q_ref, k_ref, v_ref, cq_ref, ck_ref, hg_ref, o_ref, ka, vt, m_scr, acc_scr, *, tq, hp):
    hgrp = pl.program_id(1)
    qi = pl.program_id(2)
    n_kv = vt.shape[1]
    ext = vt.shape[2] - HEAD_DIM

    @pl.when(qi == 0)
    def _():
        row = lax.broadcasted_iota(I32, (ext, tq), 0)
        ones_ext = jnp.where(row == 0, 1.0, 0.0).astype(BF16)
        for hh in range(hp):
            sl = slice(hh * HEAD_DIM, (hh + 1) * HEAD_DIM)
            ka[hh, :, :HEAD_DIM] = k_ref[:, sl].astype(BF16)
            ka[hh, :, HEAD_DIM:] = _bias_lanes(_pick_lane(ck_ref[...], hgrp * hp + hh), True)
            for n in range(n_kv):
                vt[hh, n, :HEAD_DIM, :] = v_ref[n * tq:(n + 1) * tq, sl].T.astype(BF16)
                vt[hh, n, HEAD_DIM:, :] = ones_ext

    qa = []
    for hh in range(hp):
        sl = slice(hh * HEAD_DIM, (hh + 1) * HEAD_DIM)
        bias = _bias_lanes(_pick_lane(cq_ref[...], hgrp * hp + hh), False)
        qa.append(jnp.concatenate([q_ref[:, sl].astype(BF16), bias], axis=1))
        m_scr[hh] = jnp.full((1, tq), NEG_INF, F32)
        acc_scr[hh] = jnp.zeros(acc_scr.shape[1:], F32)

    def block(kj, diagonal):
        start = pl.multiple_of(kj * tq, tq)
        st = [_nt_dot(ka[hh, pl.ds(start, tq), :], qa[hh]) for hh in range(hp)]
        vts = [vt[hh, kj] for hh in range(hp)]
        ms = [m_scr[hh] for hh in range(hp)]
        accs = [acc_scr[hh] for hh in range(hp)]
        if diagonal:
            key = lax.broadcasted_iota(I32, (tq, tq), 0)
            qry = lax.broadcasted_iota(I32, (tq, tq), 1)
            st = [jnp.where(key <= qry, s, NEG_INF) for s in st]
        m_new = [jnp.maximum(m, jnp.max(s, axis=0, keepdims=True)) for m, s in zip(ms, st)]
        pt = [jnp.exp(s - mn).astype(BF16) for s, mn in zip(st, m_new)]
        pv = [jnp.dot(v, p, preferred_element_type=F32) for v, p in zip(vts, pt)]
        new_acc = [jnp.exp(m - mn) * a + d for m, mn, a, d in zip(ms, m_new, accs, pv)]
        for hh in range(hp):
            acc_scr[hh] = new_acc[hh]
            m_scr[hh] = m_new[hh]

    def body(kj, carry):
        block(kj, False)
        return carry

    lax.fori_loop(0, qi, body, 0)
    block(qi, True)
    for hh in range(hp):
        acc = acc_scr[hh]
        ot = acc[:HEAD_DIM, :] / acc[HEAD_DIM:HEAD_DIM + 1, :]
        ot = ot * lax.rsqrt(jnp.mean(ot * ot, axis=0, keepdims=True) + EPS)
        o_ref[:, hh * HEAD_DIM:(hh + 1) * HEAD_DIM] = (ot.T * hg_ref[hh]).astype(o_ref.dtype)


def _fox_prompt(z, c, hg, *, batch, seq, tq, hp):
    nq = seq // tq
    ngrp = N_HEADS // hp
    w = hp * HEAD_DIM
    ext = 16
    return pl.pallas_call(
        functools.partial(_fox_prompt_body, tq=tq, hp=hp),
        grid=(batch, ngrp, nq),
        in_specs=[
            pl.BlockSpec((tq, w), lambda b, h, i: (b * nq + i, h)),
            pl.BlockSpec((seq, w), lambda b, h, i: (b, ngrp + h)),
            pl.BlockSpec((seq, w), lambda b, h, i: (b, 2 * ngrp + h)),
            pl.BlockSpec((tq, HEAD_DIM), lambda b, h, i: (b * nq + i, 0)),
            pl.BlockSpec((seq, HEAD_DIM), lambda b, h, i: (b, 0)),
            pl.BlockSpec((hp, 1, HEAD_DIM), lambda b, h, i: (h, 0, 0)),
        ],
        out_specs=pl.BlockSpec((tq, w), lambda b, h, i: (b * nq + i, h)),
        out_shape=jax.ShapeDtypeStruct((batch * seq, HEAD_W), BF16),
        scratch_shapes=[pltpu.VMEM((hp, seq, 2 * HEAD_DIM), BF16),
                        pltpu.VMEM((hp, nq, HEAD_DIM + ext, tq), BF16),
                        pltpu.VMEM((hp, 1, tq), F32), pltpu.VMEM((hp, HEAD_DIM + ext, tq), F32)],
        compiler_params=_params("arbitrary", "arbitrary", "arbitrary"),
        name="fox_prompt",
    )(z, z, z, c, c, hg)


def _fox_sample_body(pt_ref, q_ref, kn_ref, vn_ref, lfr_ref, lfc_ref, hg_ref, ck_hbm, cv_hbm, *rest,
                     layer, pps, page, ring):
    lp = rest[:pps]
    o_ref, m_scr, l_scr, acc_scr, carry_scr, negq_scr, kbuf, vbuf, sem = rest[pps:]
    j = pl.program_id(1)
    n_j = pl.num_programs(1)
    rows = q_ref.shape[0]
    dsq = rows // N_HEADS
    cols = page * N_HEADS
    n_pages = n_j * pps

    g = pl.program_id(0) * n_j + j
    n_g = pl.num_programs(0) * n_j

    def fetch(gg):
        slot = gg % ring
        seq = gg // n_j
        first = n_pages - 1 - (gg % n_j) * pps
        for p in range(pps):
            pg = pt_ref[seq, first - p]
            pltpu.make_async_copy(ck_hbm.at[layer, pg], kbuf.at[slot, p], sem.at[slot]).start()
            pltpu.make_async_copy(cv_hbm.at[layer, pg], vbuf.at[slot, p], sem.at[slot]).start()

    @pl.when(g == 0)
    def _():
        for ahead in range(ring - 1):
            @pl.when(ahead < n_g)
            def _():
                fetch(ahead)

    @pl.when(g + ring - 1 < n_g)
    def _():
        fetch(g + ring - 1)

    slot = g % ring
    pltpu.make_async_copy(ck_hbm.at[layer, pl.ds(0, pps)], kbuf.at[slot], sem.at[slot]).wait()
    pltpu.make_async_copy(cv_hbm.at[layer, pl.ds(0, pps)], vbuf.at[slot], sem.at[slot]).wait()

    rn = lax.broadcasted_iota(I32, (rows, rows), 0)
    cn = lax.broadcasted_iota(I32, (rows, rows), 1)
    keep_new = ((rn // dsq) == (cn % N_HEADS)) & ((cn // N_HEADS) <= (rn % dsq))
    cq_col = jnp.sum(jnp.where(keep_new, lfr_ref[...], 0.0), axis=1, keepdims=True)

    @pl.when(j == 0)
    def _():
        m_scr[...] = jnp.full(m_scr.shape, NEG_INF, F32)
        l_scr[...] = jnp.zeros(l_scr.shape, F32)
        acc_scr[...] = jnp.zeros(acc_scr.shape, F32)
        carry_scr[...] = jnp.zeros(carry_scr.shape, F32)
        rp = lax.broadcasted_iota(I32, (rows, cols), 0)
        cp = lax.broadcasted_iota(I32, (rows, cols), 1)
        negq_scr[...] = jnp.where((rp // dsq) == (cp % N_HEADS), cq_col, NEG_INF)

    def update(scores, values):
        m, l, acc = m_scr[...], l_scr[...], acc_scr[...]
        mx = functools.reduce(jnp.maximum, scores)
        m_new = jnp.maximum(m, jnp.max(mx, axis=1, keepdims=True))
        alpha = jnp.exp(m - m_new)
        ps = [jnp.exp(s - m_new) for s in scores]
        pv = [jnp.dot(p.astype(BF16), v, preferred_element_type=F32) for p, v in zip(ps, values)]
        l_scr[...] = alpha * l + jnp.sum(functools.reduce(jnp.add, ps), axis=1, keepdims=True)
        acc_scr[...] = alpha * acc + functools.reduce(jnp.add, pv)
        m_scr[...] = m_new

    q = q_ref[...]
    raw = [_nt_dot(q, kbuf[slot, p].reshape(cols, HEAD_DIM).astype(BF16)) for p in range(pps)]

    lpp = jnp.concatenate([lp[p][...] for p in range(pps)], axis=0)
    lane = lax.broadcasted_iota(I32, (pps, cols), 1)
    sub = lax.broadcasted_iota(I32, (pps, cols), 0)
    n_steps = (page - 1).bit_length()
    tot = lpp
    for t in range(n_steps):
        tot = tot + pltpu.roll(tot, N_HEADS << t, axis=1)
    suf = jnp.where(lane < cols - N_HEADS, pltpu.roll(lpp, cols - N_HEADS, axis=1), 0.0)
    for t in range(n_steps):
        sh = N_HEADS << t
        suf = suf + jnp.where(lane < cols - sh, pltpu.roll(suf, cols - sh, axis=1), 0.0)
    pre = jnp.where(sub >= 1, pltpu.roll(tot, 1, axis=0), 0.0)
    for t in range((pps - 1).bit_length()):
        sh = 1 << t
        pre = pre + jnp.where(sub >= sh, pltpu.roll(pre, sh, axis=0), 0.0)
    carry = carry_scr[...]
    bias = suf + pre + carry

    negq = negq_scr[...]
    v2 = [vbuf[slot, p].reshape(cols, HEAD_DIM).astype(BF16) for p in range(pps)]
    update([raw[p] + negq + bias[p:p + 1, :] for p in range(pps)], v2)
    carry_scr[...] = carry + jnp.sum(tot, axis=0, keepdims=True)

    @pl.when(j == pl.num_programs(1) - 1)
    def _():
        cn_row = jnp.sum(jnp.where(((rn % N_HEADS) == (cn % N_HEADS)) & ((rn // N_HEADS) <= (cn // N_HEADS)),
                                   lfc_ref[...], 0.0), axis=0, keepdims=True)
        s = jnp.where(keep_new, _nt_dot(q, kn_ref[...].astype(BF16)) + cq_col - cn_row, NEG_INF)
        update([s], [vn_ref[...].astype(BF16)])
        o = acc_scr[...] / l_scr[...]
        o_ref[...] = (_rms(o) * hg_ref[...]).astype(o_ref.dtype)


def _fox_sample(layer, page_table, q, kn, vn, lfr, lfc, hg_rows, cache_k, cache_v, cache_lf2, *, pps):
    DB, n_pages = page_table.shape
    rows = q.shape[1]
    page = cache_k.shape[2]
    cols = page * N_HEADS
    n_steps = n_pages // pps

    def seq_map(b, j, pt):
        return (b, 0, 0)

    def lf_map(p):
        def f(b, j, pt):
            return (layer, pt[b, n_pages - 1 - (j * pps + p)], 0, 0)
        return f

    ring = 4
    grid_spec = pltpu.PrefetchScalarGridSpec(
        num_scalar_prefetch=1,
        grid=(DB, n_steps),
        in_specs=[
            pl.BlockSpec((None, rows, HEAD_DIM), seq_map),
            pl.BlockSpec((None, rows, HEAD_DIM), seq_map),
            pl.BlockSpec((None, rows, HEAD_DIM), seq_map),
            pl.BlockSpec((None, 1, rows), seq_map),
            pl.BlockSpec((None, rows, 1), seq_map),
            pl.BlockSpec((rows, HEAD_DIM), lambda b, j, pt: (0, 0)),
            pl.BlockSpec(memory_space=pl.ANY),
            pl.BlockSpec(memory_space=pl.ANY),
        ] + [pl.BlockSpec((None, None, 1, cols), lf_map(p)) for p in range(pps)],
        out_specs=pl.BlockSpec((None, rows, HEAD_DIM), seq_map),
        scratch_shapes=[pltpu.VMEM((rows, 1), F32), pltpu.VMEM((rows, 1), F32),
                        pltpu.VMEM((rows, HEAD_DIM), F32), pltpu.VMEM((1, cols), F32),
                        pltpu.VMEM((rows, cols), F32),
                        pltpu.VMEM((ring, pps, page, N_HEADS, HEAD_DIM), F32),
                        pltpu.VMEM((ring, pps, page, N_HEADS, HEAD_DIM), F32),
                        pltpu.SemaphoreType.DMA((ring,))],
    )
    return pl.pallas_call(
        functools.partial(_fox_sample_body, layer=layer, pps=pps, page=page, ring=ring),
        grid_spec=grid_spec,
        out_shape=jax.ShapeDtypeStruct((DB, rows, HEAD_DIM), BF16),
        compiler_params=_params("arbitrary", "arbitrary"),
        name="fox_sample",
    )(page_table, q, kn, vn, lfr, lfc, hg_rows, cache_k, cache_v, *([cache_lf2] * pps))


def _merge_body(attp_ref, atts_ref, u_ref, vm_ref, wcp_ref, bcp_ref, wcs_ref, bcs_ref, hg_ref, wo_ref,
                x_ref, g_ref, *rest, n_prompt_tiles, n_exp):
    if n_exp:
        rw_ref, xo_ref, ho_ref, eid_ref, gate_ref, cnt_ref, att_scr, mlp_scr = rest
    else:
        xo_ref, ho_ref, att_scr, mlp_scr = rest
    i = pl.program_id(0)
    tm = x_ref.shape[0]

    def gate_heads(wc_ref, bc_ref):
        chunk = wc_ref.shape[1]
        for h in range(N_HEADS):
            sl = slice(h * HEAD_DIM, (h + 1) * HEAD_DIM)
            for c in range(tm // chunk):
                rows = slice(c * chunk, (c + 1) * chunk)
                s = jnp.dot(wc_ref[h], vm_ref[rows, sl].astype(BF16), preferred_element_type=F32) + bc_ref[h]
                o = u_ref[rows, sl] * s
                mlp_scr[rows, sl] = (_rms(o) * hg_ref[:, sl]).astype(BF16)

    @pl.when(i < n_prompt_tiles)
    def _():
        att_scr[...] = attp_ref[...]
        gate_heads(wcp_ref, bcp_ref)

    @pl.when(i >= n_prompt_tiles)
    def _():
        att_scr[...] = atts_ref[...]
        gate_heads(wcs_ref, bcs_ref)

    y = jnp.dot(att_scr[...], wo_ref[:HEAD_W, :], preferred_element_type=F32)
    y = y + jnp.dot(mlp_scr[...], wo_ref[HEAD_W:, :], preferred_element_type=F32)
    xn = x_ref[...] + y
    xo_ref[...] = xn
    hn = _rms(xn) * g_ref[...]
    ho_ref[...] = hn.astype(ho_ref.dtype)
    if n_exp:
        _route_tile(hn.astype(BF16), rw_ref, eid_ref, gate_ref, cnt_ref, n_exp)


def _merge(att_p, att_s, z, wc_p, bc_p, wc_s, bc_s, hg_mlp, wo, x, g, *, h_dtype, route_w=None, n_exp=0):
    T, D = x.shape
    tm = ROW_TILE
    n_prompt_tiles = att_p.shape[0] // tm
    assert att_s.shape[0] == tm and wc_s.shape[1] == tm
    chunk = wc_p.shape[1]
    row = lambda i: (i, 0)
    c2 = lambda i: (0, 0)
    c3 = lambda i: (0, 0, 0)
    route_in = [] if route_w is None else [pl.BlockSpec((D, HEAD_DIM), c2)]
    route_args = [] if route_w is None else [route_w]
    route_out = [] if route_w is None else [pl.BlockSpec((tm, HEAD_DIM), row), pl.BlockSpec((tm, HEAD_DIM), row),
                                            pl.BlockSpec((1, HEAD_DIM), c2)]
    route_shape = [] if route_w is None else [jax.ShapeDtypeStruct((T, HEAD_DIM), I32),
                                              jax.ShapeDtypeStruct((T, HEAD_DIM), F32),
                                              jax.ShapeDtypeStruct((1, HEAD_DIM), F32)]
    return pl.pallas_call(
        functools.partial(_merge_body, n_prompt_tiles=n_prompt_tiles, n_exp=n_exp if route_w is not None else 0),
        grid=(T // tm,),
        in_specs=[
            pl.BlockSpec((tm, HEAD_W), lambda i: (jnp.minimum(i, n_prompt_tiles - 1), 0)),
            pl.BlockSpec((tm, HEAD_W), c2),
            pl.BlockSpec((tm, HEAD_W), lambda i: (i, 3)),
            pl.BlockSpec((tm, HEAD_W), lambda i: (i, 4)),
            pl.BlockSpec((N_HEADS, chunk, chunk), c3),
            pl.BlockSpec((N_HEADS, chunk, HEAD_DIM), c3),
            pl.BlockSpec((N_HEADS, tm, tm), c3),
            pl.BlockSpec((N_HEADS, tm, HEAD_DIM), c3),
            pl.BlockSpec((1, HEAD_W), c2),
            pl.BlockSpec((2 * HEAD_W, D), c2),
            pl.BlockSpec((tm, D), row),
            pl.BlockSpec((1, D), c2),
        ] + route_in,
        out_specs=[pl.BlockSpec((tm, D), row), pl.BlockSpec((tm, D), row)] + route_out,
        out_shape=[jax.ShapeDtypeStruct((T, D), F32), jax.ShapeDtypeStruct((T, D), h_dtype)] + route_shape,
        scratch_shapes=[pltpu.VMEM((tm, HEAD_W), BF16), pltpu.VMEM((tm, HEAD_W), BF16)],
        compiler_params=_params("arbitrary"),
        name="mlp_merge_out",
    )(att_p, att_s, z, z, wc_p, bc_p, wc_s, bc_s, hg_mlp, wo, x, g, *route_args)


def _dense_body(h_ref, wg_ref, wu_ref, wd_ref, *rest, has_tail):
    if has_tail:
        wgt_ref, wut_ref, wdt_ref, x_ref, o_ref = rest
    else:
        x_ref, o_ref = rest
    f = pl.program_id(1)

    @pl.when(f == 0)
    def _():
        o_ref[...] = x_ref[...]

    n_chunks = h_ref.shape[0] // ROW_TILE

    def swiglu(wg, wu, wd):
        wgv, wuv, wdv = (ref[...].astype(BF16) for ref in (wg, wu, wd))
        hs = [h_ref[n * ROW_TILE:(n + 1) * ROW_TILE, :] for n in range(n_chunks)]
        ab = [(jnp.dot(h, wgv, preferred_element_type=F32), jnp.dot(h, wuv, preferred_element_type=F32)) for h in hs]
        gs = [(a * jax.nn.sigmoid(a) * b).astype(BF16) for a, b in ab]
        ds = [jnp.dot(g, wdv, preferred_element_type=F32) for g in gs]
        for n, d in enumerate(ds):
            o_ref[n * ROW_TILE:(n + 1) * ROW_TILE, :] += d

    swiglu(wg_ref, wu_ref, wd_ref)

    if has_tail:
        @pl.when(f == pl.num_programs(1) - 1)
        def _():
            swiglu(wgt_ref, wut_ref, wdt_ref)


def _dense_ffn(h, wg, wu, wd, x, *, tm, tf):
    T, D = x.shape
    d_ff = wg.shape[1]
    n_main = d_ff // tf
    tail = d_ff - n_main * tf
    assert tail % HEAD_DIM == 0 and (n_main * tf) % max(tail, 1) == 0
    in_specs = [
        pl.BlockSpec((tm, D), lambda i, f: (i, 0)),
        pl.BlockSpec((D, tf), lambda i, f: (0, f)),
        pl.BlockSpec((D, tf), lambda i, f: (0, f)),
        pl.BlockSpec((tf, D), lambda i, f: (f, 0)),
    ]
    args = [h, wg, wu, wd]
    if tail:
        t_idx = n_main * tf // tail
        in_specs += [pl.BlockSpec((D, tail), lambda i, f: (0, t_idx)),
                     pl.BlockSpec((D, tail), lambda i, f: (0, t_idx)),
                     pl.BlockSpec((tail, D), lambda i, f: (t_idx, 0))]
        args += [wg, wu, wd]
    in_specs.append(pl.BlockSpec((tm, D), lambda i, f: (i, 0)))
    args.append(x)
    return pl.pallas_call(
        functools.partial(_dense_body, has_tail=bool(tail)),
        grid=(T // tm, n_main),
        in_specs=in_specs,
        out_specs=pl.BlockSpec((tm, D), lambda i, f: (i, 0)),
        out_shape=jax.ShapeDtypeStruct((T, D), F32),
        compiler_params=_params("arbitrary", "arbitrary"),
        name="dense_ffn",
    )(*args)


def _route_tile(hb, w_ref, eid_ref, gate_ref, cnt_ref, n_exp):
    tm = hb.shape[0]
    logits = jnp.dot(hb, w_ref[...], preferred_element_type=F32)
    lane = lax.broadcasted_iota(I32, logits.shape, 1)
    big = logits.shape[1]
    l1 = jnp.where(lane < n_exp, logits, NEG_INF)
    m1 = jnp.max(l1, axis=1, keepdims=True)
    i1 = jnp.min(jnp.where(l1 == m1, lane, big), axis=1, keepdims=True)
    l2 = jnp.where(lane == i1, NEG_INF, l1)
    m2 = jnp.max(l2, axis=1, keepdims=True)
    i2 = jnp.min(jnp.where(l2 == m2, lane, big), axis=1, keepdims=True)
    e2 = jnp.exp(m2 - m1)
    p1 = 1.0 / (1.0 + e2)
    p2 = e2 / (1.0 + e2)
    gate_ref[...] = jnp.where(lane == 0, p1, jnp.where(lane == 1, p2, 0.0))

    @pl.when(pl.program_id(0) == 0)
    def _():
        cnt_ref[...] = jnp.zeros(cnt_ref.shape, F32)

    hit1 = (lane == i1).astype(F32)
    hit2 = (lane == i2).astype(F32)
    r = lax.broadcasted_iota(I32, (tm, tm), 0)
    c = lax.broadcasted_iota(I32, (tm, tm), 1)
    before = jnp.dot((c < r).astype(BF16), (hit1 + hit2).astype(BF16), preferred_element_type=F32)
    before = before + cnt_ref[...]
    rank1 = jnp.sum(hit1 * before, axis=1, keepdims=True).astype(I32)
    rank2 = jnp.sum(hit2 * before, axis=1, keepdims=True).astype(I32)
    cnt_ref[...] += jnp.sum(hit1 + hit2, axis=0, keepdims=True)
    eid_ref[...] = jnp.where(lane == 0, i1, jnp.where(lane == 1, i2,
                             jnp.where(lane == TOP_K, rank1, jnp.where(lane == TOP_K + 1, rank2, 0))))


def _row_copy(src_hbm, row, dst_vmem, r, sem):
    return pltpu.make_async_copy(src_hbm.at[pl.ds(row, 1)], dst_vmem.at[pl.ds(r, 1)], sem)


def _moe_body(ie_ref, ifirst_ref, inb_ref, cnt_ref, src_ref,
              h_hbm, wg_ref, wu_ref, wd_ref, y_hbm,
              xbuf, acc, stage, sem_in, sem_out, *, sub, unroll):
    del ie_ref
    w = pl.program_id(0)
    f = pl.program_id(1)
    n_f = pl.num_programs(1)
    n_items, n_used = cnt_ref[0], cnt_ref[1]
    n_total = y_hbm.shape[0] // sub

    def block_rows(r):
        return pl.ds(pl.multiple_of(r * sub, sub), sub)

    def out_copy(r_local, r_global):
        return pltpu.make_async_copy(acc.at[block_rows(r_local)], y_hbm.at[block_rows(r_global)], sem_out)

    def drain_out(n):
        def wait(r, carry):
            out_copy(r, r).wait()
            return carry

        lax.fori_loop(0, n, wait, 0)

    @pl.when((w == 0) & (f == 0))
    def _():
        acc[block_rows(0)] = jnp.zeros((sub, acc.shape[1]), F32)

        def fill(r, carry):
            out_copy(0, r).start()
            return carry

        def drain(r, carry):
            out_copy(0, r).wait()
            return carry

        lax.fori_loop(n_used, n_total, fill, 0)
        lax.fori_loop(n_used, n_total, drain, 0)

    @pl.when(w < n_items)
    def _():
        first = ifirst_ref[w]
        nb = inb_ref[w]

        @pl.when(f == 0)
        def _():
            def issue(blk, slot):
                base = (first + blk) * sub

                def body(rr, carry):
                    for u in range(unroll):
                        r = rr * unroll + u
                        _row_copy(h_hbm, src_ref[base + r], stage.at[slot], r, sem_in.at[slot]).start(priority=u % 2)
                    return carry

                lax.fori_loop(0, sub // unroll, body, 0)

            n_slots = stage.shape[0]
            ahead = n_slots - 1
            for b0 in range(ahead):
                @pl.when(b0 < nb)
                def _():
                    issue(b0, b0)

            @pl.when(w > 0)
            def _():
                drain_out(inb_ref[w - 1])

            def per_block(blk, carry):
                slot = blk % n_slots

                @pl.when(blk + ahead < nb)
                def _():
                    issue(blk + ahead, (blk + ahead) % n_slots)

                pltpu.make_async_copy(h_hbm.at[pl.ds(0, sub)], stage.at[slot], sem_in.at[slot]).wait()
                xbuf[block_rows(blk)] = stage[slot].astype(BF16)
                acc[block_rows(blk)] = jnp.zeros((sub, acc.shape[1]), F32)
                return carry

            lax.fori_loop(0, nb, per_block, 0)

        def swiglu(chunks):
            spans = [pl.ds(start if isinstance(start, int) else pl.multiple_of(start, sub), size)
                     for start, size in chunks]
            xs = [xbuf[s] for s in spans]
            wgv, wuv, wdv = (ref[...].astype(BF16) for ref in (wg_ref, wu_ref, wd_ref))
            ab = [(jnp.dot(x, wgv, preferred_element_type=F32),
                   jnp.dot(x, wuv, preferred_element_type=F32)) for x in xs]
            hs = [(a * jax.nn.sigmoid(a) * b).astype(BF16) for a, b in ab]
            ds = [jnp.dot(hh, wdv, preferred_element_type=F32) for hh in hs]
            for s, d in zip(spans, ds):
                acc[s] += d

        wide = 2 * sub
        n_wide = nb // 2

        for rp in range(xbuf.shape[0] // (2 * wide)):
            @pl.when(rp < n_wide // 2)
            def _():
                swiglu([(2 * rp * wide, wide), ((2 * rp + 1) * wide, wide)])

        @pl.when(n_wide % 2 == 1)
        def _():
            swiglu([((n_wide - 1) * wide, wide)])

        @pl.when(nb % 2 == 1)
        def _():
            swiglu([((nb - 1) * sub, sub)])

        @pl.when(f == n_f - 1)
        def _():
            def start(r, carry):
                out_copy(r, first + r).start()
                return carry

            lax.fori_loop(0, nb, start, 0)

            @pl.when(w == n_items - 1)
            def _():
                drain_out(nb)


def _moe_ffn(tables, h, wg, wu, wd, *, tf, max_blocks):
    item_expert, item_first, item_nb, counts, src, n_rows = tables
    T, D = h.shape
    sub = MOE_SUB
    n_f = wg.shape[2] // tf
    last = n_f - 1

    def live(w, cnt):
        return jnp.minimum(w, cnt[0] - 1)

    def fidx(w, f, cnt):
        return jnp.where(w < cnt[0], f, last)

    grid_spec = pltpu.PrefetchScalarGridSpec(
        num_scalar_prefetch=5,
        grid=(item_expert.shape[0], n_f),
        in_specs=[
            pl.BlockSpec(memory_space=pl.ANY),
            pl.BlockSpec((None, D, tf), lambda w, f, ie, i1, i2, cnt, s: (ie[live(w, cnt)], 0, fidx(w, f, cnt))),
            pl.BlockSpec((None, D, tf), lambda w, f, ie, i1, i2, cnt, s: (ie[live(w, cnt)], 0, fidx(w, f, cnt))),
            pl.BlockSpec((None, tf, D), lambda w, f, ie, i1, i2, cnt, s: (ie[live(w, cnt)], fidx(w, f, cnt), 0)),
        ],
        out_specs=pl.BlockSpec(memory_space=pl.ANY),
        scratch_shapes=[
            pltpu.VMEM((max_blocks * sub, D), BF16),
            pltpu.VMEM((max_blocks * sub, D), F32),
            pltpu.VMEM((4, sub, D), F32),
            pltpu.SemaphoreType.DMA((4,)), pltpu.SemaphoreType.DMA(()),
        ],
    )
    return pl.pallas_call(
        functools.partial(_moe_body, sub=sub, unroll=8),
        grid_spec=grid_spec,
        out_shape=jax.ShapeDtypeStruct((n_rows, D), F32),
        compiler_params=_params("arbitrary", "arbitrary"),
        name="moe_ffn",
    )(item_expert, item_first, item_nb, counts, src, h, wg, wu, wd)


def _combine_body(pos_ref, x_ref, gate_ref, y_hbm, *rest, tm, unroll, split_tile):
    o_refs, (buf, sem) = rest[:-2], rest[-2:]
    i = pl.program_id(0)
    slot = i % 2

    def gather(tile, to_slot):
        base = tile * tm * TOP_K

        def issue(rr, carry):
            for u in range(unroll):
                r = rr * unroll + u
                for k in range(TOP_K):
                    _row_copy(y_hbm, pos_ref[base + r * TOP_K + k], buf.at[to_slot, k], r,
                              sem.at[to_slot, k]).start(priority=k)
            return carry

        lax.fori_loop(0, tm // unroll, issue, 0)

    @pl.when(i == 0)
    def _():
        gather(0, 0)

    @pl.when(i + 1 < pl.num_programs(0))
    def _():
        gather(i + 1, 1 - slot)

    out = x_ref[...]
    for k in range(TOP_K):
        pltpu.make_async_copy(y_hbm.at[pl.ds(0, tm)], buf.at[slot, k], sem.at[slot, k]).wait()
        out = out + gate_ref[:, k:k + 1] * buf[slot, k]
    if split_tile is None:
        o_refs[0][...] = out
    else:
        @pl.when(pl.program_id(0) < split_tile)
        def _():
            o_refs[0][...] = out

        @pl.when(pl.program_id(0) >= split_tile)
        def _():
            o_refs[1][...] = out


def _combine(pos, x, gate, y, *, tm, split_rows=None):
    T, D = x.shape
    if split_rows is None:
        split_tile = None
        out_specs = [pl.BlockSpec((tm, D), lambda i, p: (i, 0))]
        out_shape = [jax.ShapeDtypeStruct((T, D), F32)]
    else:
        split_tile = split_rows // tm
        out_specs = [pl.BlockSpec((tm, D), lambda i, p: (jnp.minimum(i, split_tile - 1), 0)),
                     pl.BlockSpec((tm, D), lambda i, p: (jnp.maximum(i - split_tile, 0), 0))]
        out_shape = [jax.ShapeDtypeStruct((split_rows, D), F32), jax.ShapeDtypeStruct((T - split_rows, D), F32)]
    grid_spec = pltpu.PrefetchScalarGridSpec(
        num_scalar_prefetch=1,
        grid=(T // tm,),
        in_specs=[pl.BlockSpec((tm, D), lambda i, p: (i, 0)),
                  pl.BlockSpec((tm, HEAD_DIM), lambda i, p: (i, 0)),
                  pl.BlockSpec(memory_space=pl.ANY)],
        out_specs=out_specs,
        scratch_shapes=[pltpu.VMEM((2, TOP_K, tm, D), F32), pltpu.SemaphoreType.DMA((2, TOP_K))],
    )
    return pl.pallas_call(
        functools.partial(_combine_body, tm=tm, unroll=4, split_tile=split_tile),
        grid_spec=grid_spec,
        out_shape=out_shape,
        compiler_params=_params("arbitrary"),
        name="combine_rows",
    )(pos, x, gate, y)


def _kv_pack_body(*refs, n_layers):
    z_refs, (ko_ref, vo_ref) = refs[:2 * n_layers], refs[2 * n_layers:]
    layer = pl.program_id(0)
    tm = z_refs[0].shape[0]
    for l in range(n_layers):
        @pl.when(layer == l)
        def _():
            for src, dst in ((z_refs[2 * l], ko_ref), (z_refs[2 * l + 1], vo_ref)):
                for h in range(N_HEADS):
                    dst[pl.ds(h, tm, stride=N_HEADS), :] = src[:, h * HEAD_DIM:(h + 1) * HEAD_DIM]


def _kv_pack(zs, n_rows, *, tm):
    n_layers = len(zs)
    n_tiles = n_rows // tm
    in_specs, args = [], []
    for l, z in enumerate(zs):
        for section in (1, 2):
            in_specs.append(pl.BlockSpec(
                (tm, HEAD_W),
                lambda lay, i, l=l, section=section: (jnp.where(lay == l, i, jnp.where(lay < l, 0, n_tiles - 1)), section)))
            args.append(z)
    out_spec = pl.BlockSpec((None, tm * N_HEADS, HEAD_DIM), lambda lay, i: (lay, i, 0))
    shape = jax.ShapeDtypeStruct((n_layers, n_rows * N_HEADS, HEAD_DIM), F32)
    return pl.pallas_call(
        functools.partial(_kv_pack_body, n_layers=n_layers),
        grid=(n_layers, n_tiles),
        in_specs=in_specs,
        out_specs=[out_spec, out_spec],
        out_shape=[shape, shape],
        compiler_params=_params("arbitrary", "arbitrary"),
        name="kv_pack",
    )(*args)


def _route_tables(eid, rank, counts, sub, max_blocks):
    flat = eid.reshape(-1)
    rank = rank.reshape(-1)
    n_assign = flat.shape[0]
    n_exp = counts.shape[0]
    nsub = (counts + sub - 1) // sub
    sub_end = jnp.cumsum(nsub)
    sub_start = sub_end - nsub
    pos = (sub_start[flat] * sub + rank).astype(I32)
    n_rows = (n_assign // sub + n_exp) * sub
    src = jnp.zeros((n_rows,), I32).at[pos].set(jnp.arange(n_assign, dtype=I32) // TOP_K)
    items = (nsub + max_blocks - 1) // max_blocks
    item_end = jnp.cumsum(items)
    item_start = item_end - items
    n_items_max = (n_rows // sub + n_exp * (max_blocks - 1)) // max_blocks
    w = jnp.arange(n_items_max, dtype=I32)
    item_expert = jnp.minimum(jnp.sum((item_end[None, :] <= w[:, None]).astype(I32), axis=1), n_exp - 1)
    k = w - item_start[item_expert]
    item_first = sub_start[item_expert] + k * max_blocks
    item_nb = jnp.clip(nsub[item_expert] - k * max_blocks, 0, max_blocks)
    counts2 = jnp.stack([item_end[-1], sub_end[-1]]).astype(I32)
    return pos, (item_expert.astype(I32), item_first.astype(I32), item_nb.astype(I32), counts2, src, n_rows)


def _row_tile_multiple(n, cap):
    best = ROW_TILE
    for m in range(ROW_TILE, cap + 1, ROW_TILE):
        if n % m == 0:
            best = m
    return best


def kernel(x_prompt, x_sample, cache_k, cache_v, cache_logf, page_table, norm_mix_g, w_in, b_f,
           q_norm_g, k_norm_g, sgu_norm_g, w_spatial, b_spatial, head_norm_g, w_out, norm_ffn_g,
           dense_w_gate, dense_w_up, dense_w_down, router_w, moe_w_gate, moe_w_up, moe_w_down):
    B, S, D = x_prompt.shape
    DB, DS, _ = x_sample.shape
    depth = w_in.shape[0]
    n_phys, page = cache_k.shape[1], cache_k.shape[2]
    chunk = w_spatial.shape[-1]
    n_exp = router_w.shape[-1]
    assert cache_k.shape[3] == N_HEADS and cache_k.shape[4] == HEAD_DIM
    assert w_in.shape[2] == N_SECTIONS * HEAD_W + N_HEADS and D == 2 * HEAD_W
    Tp, Ts = B * S, DB * DS
    T = Tp + Ts
    assert Ts == ROW_TILE and S % ROW_TILE == 0 and ROW_TILE % chunk == 0
    rows = N_HEADS * DS
    scale = HEAD_DIM ** -0.5
    f_lo = 3 * HEAD_W

    x = jnp.concatenate([x_prompt.reshape(Tp, D), x_sample.reshape(Ts, D)], axis=0)
    cache_lf2 = cache_logf.reshape(depth, n_phys, 1, page * N_HEADS)
    ones = jnp.ones((HEAD_W,), F32)

    outs = {name: [] for name in ("z", "fp", "ks", "vs", "fs", "ms")}
    for l in range(depth):
        w_main = jnp.concatenate([w_in[l][:, :f_lo], w_in[l][:, f_lo + N_HEADS:]], axis=1).astype(BF16)
        wf = jnp.pad(w_in[l][:, f_lo:f_lo + N_HEADS], ((0, 0), (0, HEAD_DIM - N_HEADS))).astype(BF16)
        bf = jnp.pad(b_f[l], (0, HEAD_DIM - N_HEADS)).reshape(1, HEAD_DIM)
        g_mix = norm_mix_g[l].reshape(1, D)
        gains = jnp.stack([(q_norm_g[l] * scale).reshape(-1), k_norm_g[l].reshape(-1), ones, ones,
                           sgu_norm_g[l].reshape(-1)]).reshape(N_SECTIONS, 1, HEAD_W)
        hg_att = head_norm_g[l][:N_HEADS]
        hg_mlp = head_norm_g[l][N_HEADS:].reshape(1, HEAD_W)
        wo = w_out[l].astype(BF16)
        g_ffn = norm_ffn_g[l].reshape(1, D)
        wm = jnp.tril(w_spatial[l])
        wc_p = wm.astype(BF16)
        bc_p = jnp.broadcast_to(b_spatial[l][:, :, None], (N_HEADS, chunk, HEAD_DIM))
        tok = jnp.arange(Ts, dtype=I32)
        pick = (tok[:, None] % DS == jnp.arange(DS, dtype=I32)[None, :]).astype(F32)
        same_seq = (tok[:, None] // DS == tok[None, :] // DS).astype(F32)
        wc_s = (jnp.einsum("ri,hij,cj->hrc", pick, wm[:, :DS, :DS], pick, precision=lax.Precision.HIGHEST)
                * same_seq).astype(BF16)
        bc_s = jnp.broadcast_to(jnp.tile(b_spatial[l][:, :DS], (1, DB))[:, :, None], (N_HEADS, Ts, HEAD_DIM))

        z, lf, c = _project(x, g_mix, w_main, wf, bf, gains, tm=_row_tile_multiple(T, 3 * ROW_TILE), seq=S)

        att_p = _fox_prompt(z, c, hg_att.reshape(N_HEADS, 1, HEAD_DIM), batch=B, seq=S, tq=ROW_TILE, hp=8)

        zs = z[Tp:]
        q_rows = zs[:, :HEAD_W].astype(BF16).reshape(DB, DS, N_HEADS, HEAD_DIM).transpose(0, 2, 1, 3).reshape(DB, rows, HEAD_DIM)
        kn = zs[:, HEAD_W:2 * HEAD_W].reshape(DB, rows, HEAD_DIM)
        vn = zs[:, 2 * HEAD_W:3 * HEAD_W].reshape(DB, rows, HEAD_DIM)
        lfs8 = lf[Tp:, :N_HEADS]
        att_rows = _fox_sample(l, page_table, q_rows, kn, vn, lfs8.reshape(DB, 1, rows), lfs8.reshape(DB, rows, 1),
                               jnp.repeat(hg_att, DS, axis=0), cache_k, cache_v, cache_lf2, pps=8)
        att_s = att_rows.reshape(DB, N_HEADS, DS, HEAD_DIM).transpose(0, 2, 1, 3).reshape(Ts, HEAD_W)

        moe_layer = l % 2 == 1
        i = l // 2
        if moe_layer:
            rw = jnp.pad(router_w[i], ((0, 0), (0, HEAD_DIM - n_exp))).astype(BF16)
            x, h, eid, gate, counts = _merge(att_p, att_s, z, wc_p, bc_p, wc_s, bc_s, hg_mlp, wo, x, g_ffn,
                                             h_dtype=F32, route_w=rw, n_exp=n_exp)
        else:
            x, h = _merge(att_p, att_s, z, wc_p, bc_p, wc_s, bc_s, hg_mlp, wo, x, g_ffn, h_dtype=BF16)

        if not moe_layer:
            x = _dense_ffn(h, dense_w_gate[i], dense_w_up[i], dense_w_down[i], x,
                           tm=_row_tile_multiple(T, 3 * ROW_TILE), tf=256)
        else:
            pos, tables = _route_tables(eid[:, :TOP_K], eid[:, TOP_K:2 * TOP_K], counts[0, :n_exp].astype(I32),
                                        MOE_SUB, MOE_ITEM_BLOCKS)
            y_sorted = _moe_ffn(tables, h, moe_w_gate[i], moe_w_up[i], moe_w_down[i], tf=256,
                                max_blocks=MOE_ITEM_BLOCKS)
            if l == depth - 1:
                x_split = _combine(pos, x, gate, y_sorted, tm=ROW_TILE, split_rows=Tp)
            else:
                x = _combine(pos, x, gate, y_sorted, tm=ROW_TILE)[0]

        outs["z"].append(z)
        outs["fp"].append(lf[:Tp, :N_HEADS].reshape(B, S, N_HEADS))
        outs["ks"].append(kn.reshape(DB, DS, N_HEADS, HEAD_DIM))
        outs["vs"].append(vn.reshape(DB, DS, N_HEADS, HEAD_DIM))
        outs["fs"].append(lfs8.reshape(DB, DS, N_HEADS))
        outs["ms"].append(zs[:, 4 * HEAD_W:].reshape(DB, DS, N_HEADS, HEAD_DIM))

    k_prompt, v_prompt = _kv_pack(outs["z"], Tp, tm=min(2 * ROW_TILE, S))
    if depth % 2 == 0:
        y_p, y_s = x_split
    else:
        y_p, y_s = x[:Tp], x[Tp:]
    return (y_p.reshape(B, S, D), y_s.reshape(DB, DS, D),
            k_prompt.reshape(depth, B, S, N_HEADS, HEAD_DIM), v_prompt.reshape(depth, B, S, N_HEADS, HEAD_DIM),
            jnp.stack(outs["fp"]),
            jnp.stack(outs["ks"]), jnp.stack(outs["vs"]), jnp.stack(outs["fs"]), jnp.stack(outs["ms"]))
```
